```python
import math
import jax, jax.numpy as jnp
from jax import lax
import numpy as np

D_MODEL = 2048
BATCH = 8
SEQ = 4096
DEPTH = 2

N_A = max(1, DEPTH // 2)
N_B = DEPTH - N_A
A_HEAD_DIM = 128
A_HEADS = D_MODEL // A_HEAD_DIM
A_WIDTH = A_HEADS * A_HEAD_DIM
A_CHUNK = 64
B_HEAD_DIM = 128
B_HEADS = D_MODEL // (2 * B_HEAD_DIM)
B_WIDTH = B_HEADS * 2 * B_HEAD_DIM
Q_BLOCK = 128
D_FF = 4 * D_MODEL
PLE_DIM = 256
EPS = 1e-6

kernel_name = "yoco_hgrn2_diffattn_hybrid"


def rms_norm(x, gain):
    xf = x.astype(jnp.float32)
    y = xf * lax.rsqrt(jnp.mean(xf * xf, axis=-1, keepdims=True) + EPS)
    return (y * gain.astype(jnp.float32)).astype(x.dtype)


def hgrn2_chunkwise(q, k, v, log_f):
    b, s, h, dk = q.shape
    dv = v.shape[-1]
    n_chunks = s // A_CHUNK

    def to_chunks(t):
        return t.astype(jnp.float32).reshape(b, n_chunks, A_CHUNK, h, t.shape[-1]).transpose(1, 0, 3, 2, 4)

    qc, kc, vc, gc = (to_chunks(t) for t in (q, k, v, log_f))
    causal = jnp.tril(jnp.ones((A_CHUNK, A_CHUNK), dtype=bool))[:, :, None]

    def step(state, inp):
        qb, kb, vb, gb = inp
        g_cum = jnp.cumsum(gb, axis=2)
        rel = g_cum[:, :, :, None, :] - g_cum[:, :, None, :, :]
        decay = jnp.exp(jnp.where(causal, rel, -jnp.inf))
        scores = jnp.einsum('bhtd,bhsd,bhtsd->bhts', qb, kb, decay)
        o = (jnp.einsum('bhts,bhse->bhte', scores, vb)
             + jnp.einsum('bhtd,bhde->bhte', qb * jnp.exp(g_cum), state))
        g_last = g_cum[:, :, -1:, :]
        new_state = (jnp.exp(g_last)[:, :, 0, :, None] * state
                     + jnp.einsum('bhsd,bhse->bhde', kb * jnp.exp(g_last - g_cum), vb))
        return new_state, o

    state0 = jnp.zeros((b, h, dk, dv), jnp.float32)
    _, o = lax.scan(step, state0, (qc, kc, vc, gc))
    return o.transpose(1, 0, 3, 2, 4).reshape(b, s, h, dv)


def diff_attention(q, k, v, lam):
    b, s, h, _, dh = q.shape
    n_blocks = s // Q_BLOCK
    scale = 1.0 / math.sqrt(dh)
    kf = k.astype(jnp.float32)
    vf = v.astype(jnp.float32)
    key_pos = jnp.arange(s)
    q_blocks = q.astype(jnp.float32).reshape(b, n_blocks, Q_BLOCK, h, 2, dh).transpose(1, 0, 2, 3, 4, 5)

    def one_block(args):
        q_blk, blk = args
        logits = jnp.einsum('bqhcd,bkhcd->bhcqk', q_blk, kf) * scale
        q_pos = blk * Q_BLOCK + jnp.arange(Q_BLOCK)
        allowed = key_pos[None, :] <= q_pos[:, None]
        probs = jax.nn.softmax(jnp.where(allowed, logits, -jnp.inf), axis=-1)
        w = probs[:, :, 0] - lam * probs[:, :, 1]
        return jnp.einsum('bhqk,bkhe->bqhe', w, vf)

    out = lax.map(one_block, (q_blocks, jnp.arange(n_blocks)))
    return out.transpose(1, 0, 2, 3, 4).reshape(b, s, h, v.shape[-1]).astype(v.dtype)


def setup_inputs(seed: int = 0) -> dict:
    key = jax.random.key(seed)
    ks = jax.random.split(key, 25)

    def normal(k, shape, scale):
        return scale * jax.random.normal(k, shape, jnp.float32)

    def gain(k, shape):
        return 1.0 + 0.02 * jax.random.normal(k, shape, jnp.float32)

    out_scale = (2.0 * DEPTH) ** -0.5
    return {
        "x": normal(ks[0], (BATCH, SEQ, D_MODEL), 1.0),
        "p": normal(ks[1], (DEPTH, BATCH, SEQ, PLE_DIM), 1.0),
        "ln_mix": gain(ks[2], (DEPTH, D_MODEL)),
        "ln_mlp": gain(ks[3], (DEPTH, D_MODEL)),
        "ln_ple": gain(ks[4], (DEPTH, D_MODEL)),
        "a_w_in": normal(ks[5], (N_A, D_MODEL, 4 * A_WIDTH), D_MODEL ** -0.5),
        "a_lb": normal(ks[6], (DEPTH + 1, A_WIDTH), 0.1),
        "a_onorm": gain(ks[7], (N_A, A_HEAD_DIM)),
        "a_w_out": normal(ks[8], (N_A, A_WIDTH, D_MODEL), A_WIDTH ** -0.5 * out_scale),
        "kv_norm": gain(ks[9], (D_MODEL,)),
        "w_k": normal(ks[10], (D_MODEL, B_WIDTH), D_MODEL ** -0.5),
        "w_v": normal(ks[11], (D_MODEL, B_WIDTH), D_MODEL ** -0.5),
        "k_norm": gain(ks[12], (2, B_HEAD_DIM)),
        "b_w_q": normal(ks[13], (N_B, D_MODEL, B_WIDTH), D_MODEL ** -0.5),
        "q_norm": gain(ks[14], (N_B, 2, B_HEAD_DIM)),
        "lam_q1": normal(ks[15], (N_B, B_HEAD_DIM), 0.1),
        "lam_k1": normal(ks[16], (N_B, B_HEAD_DIM), 0.1),
        "lam_q2": normal(ks[17], (N_B, B_HEAD_DIM), 0.1),
        "lam_k2": normal(ks[18], (N_B, B_HEAD_DIM), 0.1),
        "b_subln": gain(ks[19], (N_B, 2 * B_HEAD_DIM)),
        "b_w_out": normal(ks[20], (N_B, B_WIDTH, D_MODEL), B_WIDTH ** -0.5 * out_scale),
        "mlp_up": normal(ks[21], (DEPTH, D_MODEL, D_FF), D_MODEL ** -0.5),
        "mlp_down": normal(ks[22], (DEPTH, D_FF, D_MODEL), D_FF ** -0.5 * out_scale),
        "ple_proj": normal(ks[23], (DEPTH, PLE_DIM, D_MODEL), PLE_DIM ** -0.5 * out_scale),
        "ple_gate": normal(ks[24], (DEPTH, D_MODEL, D_MODEL), D_MODEL ** -0.5),
    }


def reference(x, p, ln_mix, ln_mlp, ln_ple, a_w_in, a_lb, a_onorm, a_w_out, kv_norm, w_k, w_v, k_norm,
              b_w_q, q_norm, lam_q1, lam_k1, lam_q2, lam_k2, b_subln, b_w_out, mlp_up, mlp_down,
              ple_proj, ple_gate):
    b, s, _ = x.shape
    lower_bounds = jnp.cumsum(jax.nn.softmax(a_lb.astype(jnp.float32), axis=0), axis=0)
    k_shared = None
    v_shared = None
    for i in range(DEPTH):
        h = rms_norm(x, ln_mix[i])
        if i < N_A:
            j = i
            proj = h @ a_w_in[j]
            q_a, f_a, in_a, g_a = (t.reshape(b, s, A_HEADS, A_HEAD_DIM) for t in jnp.split(proj, 4, axis=-1))
            lb = lower_bounds[i].reshape(A_HEADS, A_HEAD_DIM)
            f = lb + (1.0 - lb) * jax.nn.sigmoid(f_a.astype(jnp.float32))
            o = hgrn2_chunkwise(jax.nn.silu(q_a), 1.0 - f, in_a, jnp.log(f))
            o = rms_norm(o, a_onorm[j]) * jax.nn.silu(g_a.astype(jnp.float32))
            x = x + o.reshape(b, s, A_WIDTH).astype(x.dtype) @ a_w_out[j]
        else:
            j = i - N_A
            q = rms_norm((h @ b_w_q[j]).reshape(b, s, B_HEADS, 2, B_HEAD_DIM), q_norm[j])
            lam_init = 0.8 - 0.6 * math.exp(-0.3 * i)
            lam = (jnp.exp(jnp.sum(lam_q1[j].astype(jnp.float32) * lam_k1[j].astype(jnp.float32)))
                   - jnp.exp(jnp.sum(lam_q2[j].astype(jnp.float32) * lam_k2[j].astype(jnp.float32)))
                   + lam_init)
            o = diff_attention(q, k_shared, v_shared, lam)
            o = rms_norm(o, b_subln[j]) * (1.0 - lam_init)
            x = x + o.reshape(b, s, B_WIDTH) @ b_w_out[j]
        h = rms_norm(x, ln_mlp[i])
        x = x + jnp.square(jax.nn.relu(h @ mlp_up[i])) @ mlp_down[i]
        gate = jax.nn.sigmoid(rms_norm(x, ln_ple[i]) @ ple_gate[i])
        x = x + gate * (p[i].astype(x.dtype) @ ple_proj[i])
        if i == N_A - 1:
            hk = rms_norm(x, kv_norm)
            k_shared = rms_norm((hk @ w_k).reshape(b, s, B_HEADS, 2, B_HEAD_DIM), k_norm)
            v_shared = (hk @ w_v).reshape(b, s, B_HEADS, 2 * B_HEAD_DIM)
    return x
```

```python
import functools
import math

import numpy as np
import jax
import jax.numpy as jnp
from jax import lax
from jax.experimental import pallas as pl
from jax.experimental.pallas import tpu as pltpu

EPS = 1e-6
LANES = 128
HGRN_CHUNK = 128
F32 = jnp.float32
BF16 = jnp.bfloat16
VMEM_LIMIT_BYTES = 56 * 1024 * 1024

_NT = (((1,), (1,)), ((), ()))
_TN = (((0,), (0,)), ((), ()))


def _params(*sem):
    return pltpu.CompilerParams(dimension_semantics=sem, vmem_limit_bytes=VMEM_LIMIT_BYTES)


def _rms_bf16(x, gain):
    ms = jnp.mean(x * x, axis=-1, keepdims=True)
    return (x * lax.rsqrt(ms + EPS) * gain).astype(BF16)


def _silu(x):
    return x * jax.nn.sigmoid(x)


def _store_group_rms(o_ref, y, gain, scale):
    for c in range(y.shape[1] // LANES):
        sl = slice(c * LANES, (c + 1) * LANES)
        yc = y[:, sl]
        ms = jnp.mean(yc * yc, axis=-1, keepdims=True)
        o_ref[:, sl] = (yc * lax.rsqrt(ms + EPS) * (gain[:, sl] * scale)).astype(o_ref.dtype)


def _norm_linear_kernel(x_ref, g_ref, w_ref, aux_ref, o_ref, h_scr, *, mode, sec_blocks, lb_rows, scale):
    j = pl.program_id(1)

    @pl.when(j == 0)
    def _():
        h_scr[...] = _rms_bf16(x_ref[...], g_ref[...])

    acc = jnp.dot(h_scr[...], w_ref[...], preferred_element_type=F32)

    def in_section(s):
        return jnp.logical_and(j >= s * sec_blocks, j < (s + 1) * sec_blocks)

    if mode == "hgrn_in":
        @pl.when(jnp.logical_or(in_section(0), in_section(3)))
        def _():
            o_ref[...] = _silu(acc)

        @pl.when(in_section(1))
        def _():
            a = aux_ref[...]
            e = jnp.exp(a - jnp.max(a, axis=0, keepdims=True))
            lb = jnp.sum(e[:lb_rows], axis=0, keepdims=True) / jnp.sum(e, axis=0, keepdims=True)
            o_ref[...] = jnp.log(lb + (1.0 - lb) * jax.nn.sigmoid(acc))

        @pl.when(in_section(2))
        def _():
            o_ref[...] = acc
    elif mode == "kv":
        @pl.when(in_section(0))
        def _():
            _store_group_rms(o_ref, acc, aux_ref[...], 1.0)

        @pl.when(in_section(1))
        def _():
            o_ref[...] = acc.astype(o_ref.dtype)
    elif mode == "q":
        _store_group_rms(o_ref, acc, aux_ref[...], scale)
    else:
        raise ValueError(mode)


def _norm_linear(x2, gain, w, aux, *, mode, sec_width, out_dtype, name, lb_rows=1, scale=1.0, tm=512, tn=512):
    t, k = x2.shape
    n = w.shape[1]
    tm, tn = min(tm, t), min(tn, sec_width)
    sec_blocks = sec_width // tn
    kern = functools.partial(_norm_linear_kernel, mode=mode, sec_blocks=sec_blocks, lb_rows=lb_rows, scale=scale)
    return pl.pallas_call(
        kern,
        grid=(t // tm, n // tn),
        in_specs=[
            pl.BlockSpec((tm, k), lambda i, j: (i, 0)),
            pl.BlockSpec((1, k), lambda i, j: (0, 0)),
            pl.BlockSpec((k, tn), lambda i, j: (0, j)),
            pl.BlockSpec((aux.shape[0], tn), lambda i, j: (0, j % sec_blocks)),
        ],
        out_specs=pl.BlockSpec((tm, tn), lambda i, j: (i, j)),
        out_shape=jax.ShapeDtypeStruct((t, n), out_dtype),
        scratch_shapes=[pltpu.VMEM((tm, k), BF16)],
        compiler_params=_params("parallel", "arbitrary"),
        name=name,
    )(x2, gain.reshape(1, k), w.astype(BF16), aux)


def _resid_linear_kernel(x_ref, a_ref, w_ref, o_ref):
    o_ref[...] = x_ref[...] + jnp.dot(a_ref[...], w_ref[...], preferred_element_type=F32)


def _resid_linear(x2, a, w, *, name, tm=512, tn=1024):
    t, n = x2.shape
    k = a.shape[1]
    tm, tn = min(tm, t), min(tn, n)
    return pl.pallas_call(
        _resid_linear_kernel,
        grid=(t // tm, n // tn),
        in_specs=[
            pl.BlockSpec((tm, tn), lambda i, j: (i, j)),
            pl.BlockSpec((tm, k), lambda i, j: (i, 0)),
            pl.BlockSpec((k, tn), lambda i, j: (0, j)),
        ],
        out_specs=pl.BlockSpec((tm, tn), lambda i, j: (i, j)),
        out_shape=jax.ShapeDtypeStruct((t, n), F32),
        compiler_params=_params("parallel", "arbitrary"),
        name=name,
    )(x2, a, w.astype(BF16))


def _mlp_kernel(x_ref, g_ref, up_ref, down_ref, o_ref, h_scr, acc_scr):
    j = pl.program_id(1)

    @pl.when(j == 0)
    def _():
        h_scr[...] = _rms_bf16(x_ref[...], g_ref[...])
        acc_scr[...] = jnp.zeros_like(acc_scr)

    u = jnp.dot(h_scr[...], up_ref[...], preferred_element_type=F32)
    a = jnp.square(jnp.maximum(u, 0.0)).astype(BF16)
    acc_scr[...] += jnp.dot(a, down_ref[...], preferred_element_type=F32)

    @pl.when(j == pl.num_programs(1) - 1)
    def _():
        o_ref[...] = x_ref[...] + acc_scr[...]


def _mlp(x2, gain, up, down, *, name, tm=512, tf=512):
    t, d = x2.shape
    f = up.shape[1]
    tm, tf = min(tm, t), min(tf, f)
    return pl.pallas_call(
        _mlp_kernel,
        grid=(t // tm, f // tf),
        in_specs=[
            pl.BlockSpec((tm, d), lambda i, j: (i, 0)),
            pl.BlockSpec((1, d), lambda i, j: (0, 0)),
            pl.BlockSpec((d, tf), lambda i, j: (0, j)),
            pl.BlockSpec((tf, d), lambda i, j: (j, 0)),
        ],
        out_specs=pl.BlockSpec((tm, d), lambda i, j: (i, 0)),
        out_shape=jax.ShapeDtypeStruct((t, d), F32),
        scratch_shapes=[pltpu.VMEM((tm, d), BF16), pltpu.VMEM((tm, d), F32)],
        compiler_params=_params("parallel", "arbitrary"),
        name=name,
    )(x2, gain.reshape(1, d), up.astype(BF16), down.astype(BF16))


def _ple_kernel(x_ref, g_ref, p_ref, gw_ref, pw_ref, o_ref, h_scr, p_scr, *, tn):
    j = pl.program_id(1)

    @pl.when(j == 0)
    def _():
        h_scr[...] = _rms_bf16(x_ref[...], g_ref[...])
        p_scr[...] = p_ref[...].astype(BF16)

    gate = jax.nn.sigmoid(jnp.dot(h_scr[...], gw_ref[...], preferred_element_type=F32))
    emb = jnp.dot(p_scr[...], pw_ref[...], preferred_element_type=F32)
    cols = pl.ds(pl.multiple_of(j * tn, tn), tn)
    o_ref[...] = x_ref[:, cols] + gate * emb


def _ple(x2, gain, p2, gate_w, proj_w, *, name, tm=512, tn=512):
    t, d = x2.shape
    pd = p2.shape[1]
    tm, tn = min(tm, t), min(tn, d)
    return pl.pallas_call(
        functools.partial(_ple_kernel, tn=tn),
        grid=(t // tm, d // tn),
        in_specs=[
            pl.BlockSpec((tm, d), lambda i, j: (i, 0)),
            pl.BlockSpec((1, d), lambda i, j: (0, 0)),
            pl.BlockSpec((tm, pd), lambda i, j: (i, 0)),
            pl.BlockSpec((d, tn), lambda i, j: (0, j)),
            pl.BlockSpec((pd, tn), lambda i, j: (0, j)),
        ],
        out_specs=pl.BlockSpec((tm, tn), lambda i, j: (i, j)),
        out_shape=jax.ShapeDtypeStruct((t, d), F32),
        scratch_shapes=[pltpu.VMEM((tm, d), BF16), pltpu.VMEM((tm, pd), BF16)],
        compiler_params=_params("parallel", "arbitrary"),
        name=name,
    )(x2, gain.reshape(1, d), p2, gate_w.astype(BF16), proj_w.astype(BF16))


def _hgrn_levels(c):
    return [c >> (l + 1) for l in range(int(math.log2(c)))]


def _hgrn_constants(c):
    idx = np.arange(c)
    blocks, masks = [], []
    for m in _hgrn_levels(c):
        mat = np.zeros((c, c), np.float32)
        for r in range(c):
            mid = (r // (2 * m)) * 2 * m + m
            if r & m:
                mat[r, mid:r + 1] = 1.0
            else:
                mat[r, r + 1:mid] = 1.0
        blocks.append(mat)
        x = idx[:, None] ^ idx[None, :]
        masks.append(((x >= m) & (x < 2 * m) & ((idx[:, None] & m) != 0)).astype(np.float32))
    blocks.append(np.tril(np.ones((c, c), np.float32)))
    blocks.append(np.triu(np.ones((c, c), np.float32), 1))
    return np.concatenate(blocks, 0), np.stack(masks, 0)


def _hgrn_kernel(q_ref, f_ref, v_ref, gate_ref, mst_ref, msk_ref, on_ref, o_ref, st_scr, *, c, hb, nchunks):
    @pl.when(pl.program_id(2) == 0)
    def _():
        st_scr[...] = jnp.zeros_like(st_scr)

    levels = _hgrn_levels(c)
    nl = len(levels)
    rowid = lax.broadcasted_iota(jnp.int32, (c, LANES), 0)

    def chunk(ci, carry):
        rows = pl.ds(pl.multiple_of(ci * c, c), c)
        g_all = f_ref[0, rows, :]
        g_hi = g_all.astype(BF16)
        g_lo = (g_all - g_hi.astype(F32)).astype(BF16)
        mst = mst_ref[...]
        e_all = (jnp.dot(mst, g_hi, preferred_element_type=F32)
                 + jnp.dot(mst, g_lo, preferred_element_type=F32))
        for h in range(hb):
            hl = slice(h * LANES, (h + 1) * LANES)
            qs = q_ref[0, rows, hl]
            v = v_ref[0, rows, hl]
            k = 1.0 - jnp.exp(g_all[:, hl])
            e_h = e_all[:, hl]
            a = jnp.zeros((c, c), F32)
            for l, m in enumerate(levels):
                x = (jnp.where((rowid & m) != 0, qs, k) * jnp.exp(e_h[l * c:(l + 1) * c])).astype(BF16)
                a = a + msk_ref[l] * lax.dot_general(x, x, _NT, preferred_element_type=F32)
            vb = v.astype(BF16)
            o = jnp.dot(a.astype(BF16), vb, preferred_element_type=F32)
            o = o + jnp.sum(qs * k, axis=-1, keepdims=True) * v
            g_cum = e_h[nl * c:(nl + 1) * c]
            st = st_scr[h]
            qg = (qs * jnp.exp(g_cum)).astype(BF16)
            o = o + lax.dot_general(qg, st.astype(BF16), _NT, preferred_element_type=F32)
            kt = (k * jnp.exp(e_h[(nl + 1) * c:(nl + 2) * c])).astype(BF16)
            st_scr[h] = (st * jnp.exp(g_cum[c - 1:c, :])
                         + lax.dot_general(vb, kt, _TN, preferred_element_type=F32))
            ms = jnp.mean(o * o, axis=-1, keepdims=True)
            y = o * lax.rsqrt(ms + EPS) * on_ref[...] * gate_ref[0, rows, hl]
            o_ref[0, rows, hl] = y.astype(o_ref.dtype)
        return carry

    lax.fori_loop(0, nchunks, chunk, 0)


def _hgrn(proj, onorm, *, width, name, rows=512, hb=2):
    b, s, _ = proj.shape
    c = HGRN_CHUNK
    rows = min(rows, s)
    hw = hb * LANES
    nsb = width // hw
    mstack, masks = _hgrn_constants(c)
    nl = masks.shape[0]

    def sec(k):
        return pl.BlockSpec((1, rows, hw), lambda bi, hg, r: (bi, r, k * nsb + hg))

    kern = functools.partial(_hgrn_kernel, c=c, hb=hb, nchunks=rows // c)
    return pl.pallas_call(
        kern,
        grid=(b, nsb, s // rows),
        in_specs=[
            sec(0), sec(1), sec(2), sec(3),
            pl.BlockSpec(((nl + 2) * c, c), lambda bi, hg, r: (0, 0)),
            pl.BlockSpec((nl, c, c), lambda bi, hg, r: (0, 0, 0)),
            pl.BlockSpec((1, LANES), lambda bi, hg, r: (0, 0)),
        ],
        out_specs=pl.BlockSpec((1, rows, hw), lambda bi, hg, r: (bi, r, hg)),
        out_shape=jax.ShapeDtypeStruct((b, s, width), BF16),
        scratch_shapes=[pltpu.VMEM((hb, LANES, LANES), F32)],
        compiler_params=_params("parallel", "parallel", "arbitrary"),
        name=name,
    )(proj, proj, proj, proj, jnp.asarray(mstack, BF16), jnp.asarray(masks, F32), onorm.reshape(1, LANES))


def _attn_kernel(q_ref, k_ref, v_ref, sub_ref, lq1_ref, lk1_ref, lq2_ref, lk2_ref, o_ref,
                 m_scr, l_scr, acc_scr, *, tq, lam_init):
    qi = pl.program_id(2)
    m_scr[...] = jnp.full_like(m_scr, -1e30)
    l_scr[...] = jnp.zeros_like(l_scr)
    acc_scr[...] = jnp.zeros_like(acc_scr)
    q = q_ref[0]

    def step(j, masked):
        rows = pl.ds(pl.multiple_of(j * tq, tq), tq)
        kb = k_ref[0, rows, :]
        vb = v_ref[0, rows, :]
        for c in range(2):
            sl = slice(c * LANES, (c + 1) * LANES)
            s = lax.dot_general(q[:, sl], kb[:, sl], _NT, preferred_element_type=F32)
            if masked:
                row = lax.broadcasted_iota(jnp.int32, s.shape, 0)
                col = lax.broadcasted_iota(jnp.int32, s.shape, 1)
                s = jnp.where(col <= row, s, -jnp.inf)
            m_prev = m_scr[c]
            m_new = jnp.maximum(m_prev, jnp.max(s, axis=-1, keepdims=True))
            alpha = jnp.exp(m_prev - m_new)
            p = jnp.exp(s - m_new)
            l_scr[c] = alpha * l_scr[c] + jnp.sum(p, axis=-1, keepdims=True)
            acc_scr[c] = alpha * acc_scr[c] + jnp.dot(p.astype(BF16), vb, preferred_element_type=F32)
            m_scr[c] = m_new

    def body(j, carry):
        step(j, False)
        return carry

    lax.fori_loop(0, qi, body, 0)
    step(qi, True)

    lam = (jnp.exp(jnp.sum(lq1_ref[...] * lk1_ref[...], axis=-1, keepdims=True))
           - jnp.exp(jnp.sum(lq2_ref[...] * lk2_ref[...], axis=-1, keepdims=True))
           + lam_init)
    o = acc_scr[0] * (1.0 / l_scr[0]) - lam * (acc_scr[1] * (1.0 / l_scr[1]))
    ms = jnp.mean(o * o, axis=-1, keepdims=True)
    o_ref[0] = (o * lax.rsqrt(ms + EPS) * (sub_ref[...] * (1.0 - lam_init))).astype(o_ref.dtype)


def _diff_attention(q, kv, subln, lq1, lk1, lq2, lk2, *, lam_init, name, tq=256):
    b, s, width = q.shape
    hw = 2 * LANES
    nh = width // hw
    tq = min(tq, s)
    vec = pl.BlockSpec((1, LANES), lambda bi, h, i: (0, 0))
    return pl.pallas_call(
        functools.partial(_attn_kernel, tq=tq, lam_init=lam_init),
        grid=(b, nh, s // tq),
        in_specs=[
            pl.BlockSpec((1, tq, hw), lambda bi, h, i: (bi, i, h)),
            pl.BlockSpec((1, s, hw), lambda bi, h, i: (bi, 0, h)),
            pl.BlockSpec((1, s, hw), lambda bi, h, i: (bi, 0, nh + h)),
            pl.BlockSpec((1, hw), lambda bi, h, i: (0, 0)),
            vec, vec, vec, vec,
        ],
        out_specs=pl.BlockSpec((1, tq, hw), lambda bi, h, i: (bi, i, h)),
        out_shape=jax.ShapeDtypeStruct((b, s, width), BF16),
        scratch_shapes=[
            pltpu.VMEM((2, tq, 1), F32),
            pltpu.VMEM((2, tq, 1), F32),
            pltpu.VMEM((2, tq, hw), F32),
        ],
        compiler_params=_params("parallel", "parallel", "arbitrary"),
        name=name,
    )(q, kv, kv, subln.reshape(1, hw), lq1.reshape(1, LANES), lk1.reshape(1, LANES),
      lq2.reshape(1, LANES), lk2.reshape(1, LANES))


def kernel(x, p, ln_mix, ln_mlp, ln_ple, a_w_in, a_lb, a_onorm, a_w_out, kv_norm, w_k, w_v, k_norm, b_w_q, q_norm, lam_q1, lam_k1, lam_q2, lam_k2, b_subln, b_w_out, mlp_up, mlp_down, ple_proj, ple_gate):
    b, s, d = x.shape
    depth = ln_mix.shape[0]
    n_a = a_w_in.shape[0]
    t = b * s
    x2 = x.reshape(t, d)
    kv = None
    for i in range(depth):
        if i < n_a:
            j = i
            width = a_w_out.shape[1]
            proj = _norm_linear(x2, ln_mix[i], a_w_in[j], a_lb, mode="hgrn_in", sec_width=width,
                                out_dtype=F32, lb_rows=i + 1, name=f"hgrn_in_{i}")
            o = _hgrn(proj.reshape(b, s, 4 * width), a_onorm[j], width=width, name=f"hgrn_{i}")
            x2 = _resid_linear(x2, o.reshape(t, width), a_w_out[j], name=f"hgrn_out_{i}")
        else:
            j = i - n_a
            width = b_w_q.shape[2]
            head_gain = jnp.tile(q_norm[j].reshape(1, -1), (1, width // (2 * LANES)))
            q = _norm_linear(x2, ln_mix[i], b_w_q[j], head_gain, mode="q", sec_width=width, out_dtype=BF16,
                             scale=1.0 / math.sqrt(LANES), name=f"attn_q_{i}")
            lam_init = 0.8 - 0.6 * math.exp(-0.3 * i)
            o = _diff_attention(q.reshape(b, s, width), kv, b_subln[j], lam_q1[j], lam_k1[j], lam_q2[j],
                                lam_k2[j], lam_init=lam_init, name=f"attn_{i}")
            x2 = _resid_linear(x2, o.reshape(t, width), b_w_out[j], name=f"attn_out_{i}")
        x2 = _mlp(x2, ln_mlp[i], mlp_up[i], mlp_down[i], name=f"mlp_{i}")
        x2 = _ple(x2, ln_ple[i], p[i].reshape(t, -1), ple_gate[i], ple_proj[i], name=f"ple_{i}")
        if i == n_a - 1:
            width = w_k.shape[1]
            head_gain = jnp.tile(k_norm.reshape(1, -1), (1, width // (2 * LANES)))
            kv = _norm_linear(x2, kv_norm, jnp.concatenate([w_k, w_v], axis=1), head_gain, mode="kv",
                              sec_width=width, out_dtype=BF16, name="shared_kv").reshape(b, s, 2 * width)
    return x2.reshape(b, s, d)
```

```python
import functools
import math

import numpy as np
import jax
import jax.numpy as jnp
from jax import lax
from jax.experimental import pallas as pl
from jax.experimental.pallas import tpu as pltpu

EPS = 1e-6
LANES = 128
HGRN_CHUNK = 128
ROW_CHUNK = 256
F32 = jnp.float32
BF16 = jnp.bfloat16
VMEM_LIMIT_BYTES = 56 * 1024 * 1024

_NT = (((1,), (1,)), ((), ()))
_TN = (((0,), (0,)), ((), ()))


def _params(*sem):
    return pltpu.CompilerParams(dimension_semantics=sem, vmem_limit_bytes=VMEM_LIMIT_BYTES)


def _rms_bf16(x, gain):
    ms = jnp.mean(x * x, axis=-1, keepdims=True)
    return (x * lax.rsqrt(ms + EPS) * gain).astype(BF16)


def _silu(x):
    return x * jax.nn.sigmoid(x)


def _group_rms(y, gain, scale):
    outs = []
    for c in range(y.shape[1] // LANES):
        sl = slice(c * LANES, (c + 1) * LANES)
        yc = y[:, sl]
        ms = jnp.mean(yc * yc, axis=-1, keepdims=True)
        outs.append(yc * lax.rsqrt(ms + EPS) * (gain[:, sl] * scale))
    return jnp.concatenate(outs, axis=1)


def _row_chunks(tm):
    rc = min(ROW_CHUNK, tm)
    return [slice(r * rc, (r + 1) * rc) for r in range(tm // rc)]


def _norm_linear_kernel(x_ref, g_ref, w_ref, aux_ref, o_ref, h_scr, *, mode, sec_blocks, lb_rows, scale):
    j = pl.program_id(1)

    def silu_epi(acc):
        return _silu(acc)

    def logf_epi(acc):
        a = aux_ref[...]
        e = jnp.exp(a - jnp.max(a, axis=0, keepdims=True))
        lb = jnp.sum(e[:lb_rows], axis=0, keepdims=True) / jnp.sum(e, axis=0, keepdims=True)
        return jnp.log(lb + (1.0 - lb) * jax.nn.sigmoid(acc))

    def plain_epi(acc):
        return acc

    def head_rms_epi(acc):
        return _group_rms(acc, aux_ref[...], scale)

    def run(first, epilogue):
        for rows in _row_chunks(x_ref.shape[0]):
            if first:
                h = _rms_bf16(x_ref[rows, :], g_ref[...])
                h_scr[rows, :] = h
            else:
                h = h_scr[rows, :]
            acc = jnp.dot(h, w_ref[...], preferred_element_type=F32)
            o_ref[rows, :] = epilogue(acc).astype(o_ref.dtype)

    def in_section(s):
        return jnp.logical_and(j >= max(s * sec_blocks, 1), j < (s + 1) * sec_blocks)

    if mode == "hgrn_in":
        pl.when(j == 0)(lambda: run(True, silu_epi))
        pl.when(jnp.logical_or(in_section(0), in_section(3)))(lambda: run(False, silu_epi))
        pl.when(in_section(1))(lambda: run(False, logf_epi))
        pl.when(in_section(2))(lambda: run(False, plain_epi))
    elif mode == "kv":
        pl.when(j == 0)(lambda: run(True, head_rms_epi))
        pl.when(in_section(0))(lambda: run(False, head_rms_epi))
        pl.when(in_section(1))(lambda: run(False, plain_epi))
    elif mode == "q":
        pl.when(j == 0)(lambda: run(True, head_rms_epi))
        pl.when(j > 0)(lambda: run(False, head_rms_epi))
    else:
        raise ValueError(mode)


def _norm_linear(x2, gain, w, aux, *, mode, sec_width, out_dtype, name, lb_rows=1, scale=1.0, tm=512, tn=1024):
    t, k = x2.shape
    n = w.shape[1]
    tm, tn = min(tm, t), min(tn, sec_width)
    sec_blocks = sec_width // tn
    kern = functools.partial(_norm_linear_kernel, mode=mode, sec_blocks=sec_blocks, lb_rows=lb_rows, scale=scale)
    return pl.pallas_call(
        kern,
        grid=(t // tm, n // tn),
        in_specs=[
            pl.BlockSpec((tm, k), lambda i, j: (i, 0)),
            pl.BlockSpec((1, k), lambda i, j: (0, 0)),
            pl.BlockSpec((k, tn), lambda i, j: (0, j)),
            pl.BlockSpec((aux.shape[0], tn), lambda i, j: (0, j % sec_blocks)),
        ],
        out_specs=pl.BlockSpec((tm, tn), lambda i, j: (i, j)),
        out_shape=jax.ShapeDtypeStruct((t, n), out_dtype),
        scratch_shapes=[pltpu.VMEM((tm, k), BF16)],
        compiler_params=_params("parallel", "arbitrary"),
        name=name,
    )(x2, gain.reshape(1, k), w.astype(BF16), aux)


def _resid_linear_kernel(x_ref, a_ref, w_ref, o_ref):
    for rows in _row_chunks(x_ref.shape[0]):
        o_ref[rows, :] = x_ref[rows, :] + jnp.dot(a_ref[rows, :], w_ref[...], preferred_element_type=F32)


def _resid_linear(x2, a, w, *, name, tm=512, tn=1024):
    t, n = x2.shape
    k = a.shape[1]
    tm, tn = min(tm, t), min(tn, n)
    return pl.pallas_call(
        _resid_linear_kernel,
        grid=(t // tm, n // tn),
        in_specs=[
            pl.BlockSpec((tm, tn), lambda i, j: (i, j)),
            pl.BlockSpec((tm, k), lambda i, j: (i, 0)),
            pl.BlockSpec((k, tn), lambda i, j: (0, j)),
        ],
        out_specs=pl.BlockSpec((tm, tn), lambda i, j: (i, j)),
        out_shape=jax.ShapeDtypeStruct((t, n), F32),
        compiler_params=_params("parallel", "arbitrary"),
        name=name,
    )(x2, a, w.astype(BF16))


def _mlp_kernel(x_ref, g_ref, up_ref, down_ref, o_ref, h_scr):
    def run(first):
        for rows in _row_chunks(x_ref.shape[0]):
            if first:
                h = _rms_bf16(x_ref[rows, :], g_ref[...])
                h_scr[rows, :] = h
                base = x_ref[rows, :]
            else:
                h = h_scr[rows, :]
                base = o_ref[rows, :]
            u = jnp.dot(h, up_ref[...], preferred_element_type=F32)
            a = jnp.square(jnp.maximum(u, 0.0)).astype(BF16)
            o_ref[rows, :] = base + jnp.dot(a, down_ref[...], preferred_element_type=F32)

    j = pl.program_id(1)
    pl.when(j == 0)(lambda: run(True))
    pl.when(j > 0)(lambda: run(False))


def _mlp(x2, gain, up, down, *, name, tm=512, tf=1024):
    t, d = x2.shape
    f = up.shape[1]
    tm, tf = min(tm, t), min(tf, f)
    return pl.pallas_call(
        _mlp_kernel,
        grid=(t // tm, f // tf),
        in_specs=[
            pl.BlockSpec((tm, d), lambda i, j: (i, 0)),
            pl.BlockSpec((1, d), lambda i, j: (0, 0)),
            pl.BlockSpec((d, tf), lambda i, j: (0, j)),
            pl.BlockSpec((tf, d), lambda i, j: (j, 0)),
        ],
        out_specs=pl.BlockSpec((tm, d), lambda i, j: (i, 0)),
        out_shape=jax.ShapeDtypeStruct((t, d), F32),
        scratch_shapes=[pltpu.VMEM((tm, d), BF16)],
        compiler_params=_params("parallel", "arbitrary"),
        name=name,
    )(x2, gain.reshape(1, d), up.astype(BF16), down.astype(BF16))


def _ple_kernel(x_ref, g_ref, p_ref, gw_ref, pw_ref, o_ref, h_scr, *, tn):
    j = pl.program_id(1)
    cols = pl.ds(pl.multiple_of(j * tn, tn), tn)

    def run(first):
        for rows in _row_chunks(x_ref.shape[0]):
            if first:
                h = _rms_bf16(x_ref[rows, :], g_ref[...])
                h_scr[rows, :] = h
            else:
                h = h_scr[rows, :]
            gate = jax.nn.sigmoid(jnp.dot(h, gw_ref[...], preferred_element_type=F32))
            emb = jnp.dot(p_ref[rows, :].astype(BF16), pw_ref[...], preferred_element_type=F32)
            o_ref[rows, :] = x_ref[rows, cols] + gate * emb

    pl.when(j == 0)(lambda: run(True))
    pl.when(j > 0)(lambda: run(False))


def _ple(x2, gain, p2, gate_w, proj_w, *, name, tm=512, tn=1024):
    t, d = x2.shape
    pd = p2.shape[1]
    tm, tn = min(tm, t), min(tn, d)
    return pl.pallas_call(
        functools.partial(_ple_kernel, tn=tn),
        grid=(t // tm, d // tn),
        in_specs=[
            pl.BlockSpec((tm, d), lambda i, j: (i, 0)),
            pl.BlockSpec((1, d), lambda i, j: (0, 0)),
            pl.BlockSpec((tm, pd), lambda i, j: (i, 0)),
            pl.BlockSpec((d, tn), lambda i, j: (0, j)),
            pl.BlockSpec((pd, tn), lambda i, j: (0, j)),
        ],
        out_specs=pl.BlockSpec((tm, tn), lambda i, j: (i, j)),
        out_shape=jax.ShapeDtypeStruct((t, d), F32),
        scratch_shapes=[pltpu.VMEM((tm, d), BF16)],
        compiler_params=_params("parallel", "arbitrary"),
        name=name,
    )(x2, gain.reshape(1, d), p2, gate_w.astype(BF16), proj_w.astype(BF16))


def _hgrn_levels(c):
    return [c >> (l + 1) for l in range(int(math.log2(c)))]


def _hgrn_constants(c):
    idx = np.arange(c)
    x = idx[:, None] ^ idx[None, :]
    masks = [((x >= m) & (x < 2 * m) & ((idx[:, None] & m) != 0)).astype(np.float32) for m in _hgrn_levels(c)]
    return np.tril(np.ones((c, c), np.float32)), np.stack(masks, 0)


def _level_log_decay(g_cum, g, m):
    c, w = g_cum.shape
    if m == 1:
        odd = (lax.broadcasted_iota(jnp.int32, (c, w), 0) & 1) != 0
        return jnp.where(odd, g, 0.0)
    if m == 2:
        g3 = g_cum.reshape(c // 8, 8, w)
        upper_block = lax.broadcasted_iota(jnp.int32, g3.shape, 1) < 4
        mid = jnp.where(upper_block, g3[:, 1:2, :], g3[:, 5:6, :])
    else:
        g3 = g_cum.reshape(c // (2 * m), 2 * m, w)
        mid = g3[:, m - 1:m, :]
    return (-jnp.abs(g3 - mid)).reshape(c, w)


def _hgrn_kernel(q_ref, f_ref, v_ref, gate_ref, tri_ref, msk_ref, on_ref, o_ref, st_scr, *, c, hb, nchunks):
    @pl.when(pl.program_id(2) == 0)
    def _():
        st_scr[...] = jnp.zeros_like(st_scr)

    levels = _hgrn_levels(c)
    rowid = lax.broadcasted_iota(jnp.int32, (c, LANES), 0)

    def chunk(ci, carry):
        rows = pl.ds(pl.multiple_of(ci * c, c), c)
        g_all = f_ref[0, rows, :]
        g_hi = g_all.astype(BF16)
        g_lo = (g_all - g_hi.astype(F32)).astype(BF16)
        tri = tri_ref[...]
        g_cum_all = (jnp.dot(tri, g_hi, preferred_element_type=F32)
                     + jnp.dot(tri, g_lo, preferred_element_type=F32))
        k_all = 1.0 - jnp.exp(g_all)
        g_last = g_cum_all[c - 1:c, :]
        q_dec_all = jnp.exp(g_cum_all)
        k_dec_all = jnp.exp(g_last - g_cum_all)
        st_dec_all = jnp.exp(g_last)
        lvl_dec_all = [jnp.exp(_level_log_decay(g_cum_all, g_all, m)) for m in levels]
        for h in range(hb):
            hl = slice(h * LANES, (h + 1) * LANES)
            qs = q_ref[0, rows, hl]
            v = v_ref[0, rows, hl]
            k = k_all[:, hl]
            a = jnp.zeros((c, c), F32)
            for l, m in enumerate(levels):
                x = (jnp.where((rowid & m) != 0, qs, k) * lvl_dec_all[l][:, hl]).astype(BF16)
                a = a + msk_ref[l] * lax.dot_general(x, x, _NT, preferred_element_type=F32)
            vb = v.astype(BF16)
            o = jnp.dot(a.astype(BF16), vb, preferred_element_type=F32)
            o = o + jnp.sum(qs * k, axis=-1, keepdims=True) * v
            st = st_scr[h]
            qg = (qs * q_dec_all[:, hl]).astype(BF16)
            o = o + lax.dot_general(qg, st.astype(BF16), _NT, preferred_element_type=F32)
            kt = (k * k_dec_all[:, hl]).astype(BF16)
            st_scr[h] = st * st_dec_all[:, hl] + lax.dot_general(vb, kt, _TN, preferred_element_type=F32)
            ms = jnp.mean(o * o, axis=-1, keepdims=True)
            y = o * lax.rsqrt(ms + EPS) * on_ref[...] * gate_ref[0, rows, hl]
            o_ref[0, rows, hl] = y.astype(o_ref.dtype)
        return carry

    lax.fori_loop(0, nchunks, chunk, 0)


def _hgrn(proj, onorm, *, width, name, rows=512, hb=4):
    b, s, _ = proj.shape
    c = HGRN_CHUNK
    rows = min(rows, s)
    hw = hb * LANES
    nsb = width // hw
    tri, masks = _hgrn_constants(c)
    nl = masks.shape[0]

    def sec(k):
        return pl.BlockSpec((1, rows, hw), lambda bi, hg, r: (bi, r, k * nsb + hg))

    kern = functools.partial(_hgrn_kernel, c=c, hb=hb, nchunks=rows // c)
    return pl.pallas_call(
        kern,
        grid=(b, nsb, s // rows),
        in_specs=[
            sec(0), sec(1), sec(2), sec(3),
            pl.BlockSpec((c, c), lambda bi, hg, r: (0, 0)),
            pl.BlockSpec((nl, c, c), lambda bi, hg, r: (0, 0, 0)),
            pl.BlockSpec((1, LANES), lambda bi, hg, r: (0, 0)),
        ],
        out_specs=pl.BlockSpec((1, rows, hw), lambda bi, hg, r: (bi, r, hg)),
        out_shape=jax.ShapeDtypeStruct((b, s, width), BF16),
        scratch_shapes=[pltpu.VMEM((hb, LANES, LANES), F32)],
        compiler_params=_params("parallel", "parallel", "arbitrary"),
        name=name,
    )(proj, proj, proj, proj, jnp.asarray(tri, BF16), jnp.asarray(masks, F32), onorm.reshape(1, LANES))


def _attn_kernel(q_ref, k_ref, v_ref, sub_ref, lq1_ref, lk1_ref, lq2_ref, lk2_ref, o_ref,
                 m_scr, l_scr, acc_scr, *, tq, lam_init):
    qi = pl.program_id(2)
    m_scr[...] = jnp.full_like(m_scr, -1e30)
    l_scr[...] = jnp.zeros_like(l_scr)
    acc_scr[...] = jnp.zeros_like(acc_scr)
    hw = acc_scr.shape[-1]

    def step(j, masked):
        rows = pl.ds(pl.multiple_of(j * tq, tq), tq)
        vb = v_ref[0, rows, :]
        for c in range(2):
            sl = slice(c * LANES, (c + 1) * LANES)
            s = lax.dot_general(q_ref[0, :, sl], k_ref[0, rows, sl], _NT, preferred_element_type=F32)
            if masked:
                row = lax.broadcasted_iota(jnp.int32, s.shape, 0)
                col = lax.broadcasted_iota(jnp.int32, s.shape, 1)
                s = jnp.where(col <= row, s, -jnp.inf)
            m_prev = m_scr[c]
            m_new = jnp.maximum(m_prev, jnp.max(s, axis=-1, keepdims=True))
            alpha = jnp.exp(m_prev - m_new)
            p = jnp.exp(s - pltpu.repeat(m_new, tq // LANES, axis=1))
            l_scr[c] = alpha * l_scr[c] + jnp.sum(p, axis=-1, keepdims=True)
            acc_scr[c] = (pltpu.repeat(alpha, hw // LANES, axis=1) * acc_scr[c]
                          + jnp.dot(p.astype(BF16), vb, preferred_element_type=F32))
            m_scr[c] = m_new

    def body(j, carry):
        step(j, False)
        return carry

    lax.fori_loop(0, qi, body, 0)
    step(qi, True)

    lam = (jnp.exp(jnp.sum(lq1_ref[...] * lk1_ref[...], axis=-1, keepdims=True))
           - jnp.exp(jnp.sum(lq2_ref[...] * lk2_ref[...], axis=-1, keepdims=True))
           + lam_init)
    inv1 = pltpu.repeat(1.0 / l_scr[0], hw // LANES, axis=1)
    inv2 = pltpu.repeat(1.0 / l_scr[1], hw // LANES, axis=1)
    o = acc_scr[0] * inv1 - lam * (acc_scr[1] * inv2)
    ms = jnp.mean(o * o, axis=-1, keepdims=True)
    o_ref[0] = (o * lax.rsqrt(ms + EPS) * (sub_ref[...] * (1.0 - lam_init))).astype(o_ref.dtype)


def _diff_attention(q, kv, subln, lq1, lk1, lq2, lk2, *, lam_init, name, tq=512):
    b, s, width = q.shape
    hw = 2 * LANES
    nh = width // hw
    tq = min(tq, s)
    vec = pl.BlockSpec((1, LANES), lambda bi, h, i: (0, 0))
    return pl.pallas_call(
        functools.partial(_attn_kernel, tq=tq, lam_init=lam_init),
        grid=(b, nh, s // tq),
        in_specs=[
            pl.BlockSpec((1, tq, hw), lambda bi, h, i: (bi, i, h)),
            pl.BlockSpec((1, s, hw), lambda bi, h, i: (bi, 0, h)),
            pl.BlockSpec((1, s, hw), lambda bi, h, i: (bi, 0, nh + h)),
            pl.BlockSpec((1, hw), lambda bi, h, i: (0, 0)),
            vec, vec, vec, vec,
        ],
        out_specs=pl.BlockSpec((1, tq, hw), lambda bi, h, i: (bi, i, h)),
        out_shape=jax.ShapeDtypeStruct((b, s, width), BF16),
        scratch_shapes=[
            pltpu.VMEM((2, tq, LANES), F32),
            pltpu.VMEM((2, tq, LANES), F32),
            pltpu.VMEM((2, tq, hw), F32),
        ],
        compiler_params=_params("parallel", "parallel", "arbitrary"),
        name=name,
    )(q, kv, kv, subln.reshape(1, hw), lq1.reshape(1, LANES), lk1.reshape(1, LANES),
      lq2.reshape(1, LANES), lk2.reshape(1, LANES))


def kernel(x, p, ln_mix, ln_mlp, ln_ple, a_w_in, a_lb, a_onorm, a_w_out, kv_norm, w_k, w_v, k_norm, b_w_q, q_norm, lam_q1, lam_k1, lam_q2, lam_k2, b_subln, b_w_out, mlp_up, mlp_down, ple_proj, ple_gate):
    b, s, d = x.shape
    depth = ln_mix.shape[0]
    n_a = a_w_in.shape[0]
    t = b * s
    x2 = x.reshape(t, d)
    kv = None
    for i in range(depth):
        if i < n_a:
            j = i
            width = a_w_out.shape[1]
            proj = _norm_linear(x2, ln_mix[i], a_w_in[j], a_lb, mode="hgrn_in", sec_width=width,
                                out_dtype=F32, lb_rows=i + 1, name=f"hgrn_in_{i}")
            o = _hgrn(proj.reshape(b, s, 4 * width), a_onorm[j], width=width, name=f"hgrn_{i}")
            x2 = _resid_linear(x2, o.reshape(t, width), a_w_out[j], name=f"hgrn_out_{i}")
        else:
            j = i - n_a
            width = b_w_q.shape[2]
            head_gain = jnp.tile(q_norm[j].reshape(1, -1), (1, width // (2 * LANES)))
            q = _norm_linear(x2, ln_mix[i], b_w_q[j], head_gain, mode="q", sec_width=width, out_dtype=BF16,
                             scale=1.0 / math.sqrt(LANES), name=f"attn_q_{i}")
            lam_init = 0.8 - 0.6 * math.exp(-0.3 * i)
            o = _diff_attention(q.reshape(b, s, width), kv, b_subln[j], lam_q1[j], lam_k1[j], lam_q2[j],
                                lam_k2[j], lam_init=lam_init, name=f"attn_{i}")
            x2 = _resid_linear(x2, o.reshape(t, width), b_w_out[j], name=f"attn_out_{i}")
        x2 = _mlp(x2, ln_mlp[i], mlp_up[i], mlp_down[i], name=f"mlp_{i}")
        x2 = _ple(x2, ln_ple[i], p[i].reshape(t, -1), ple_gate[i], ple_proj[i], name=f"ple_{i}")
        if i == n_a - 1:
            width = w_k.shape[1]
            head_gain = jnp.tile(k_norm.reshape(1, -1), (1, width // (2 * LANES)))
            kv = _norm_linear(x2, kv_norm, jnp.concatenate([w_k, w_v], axis=1), head_gain, mode="kv",
                              sec_width=width, out_dtype=BF16, name="shared_kv").reshape(b, s, 2 * width)
    return x2.reshape(b, s, d)
```

```python
import functools
import math

import numpy as np
import jax
import jax.numpy as jnp
from jax import lax
from jax.experimental import pallas as pl
from jax.experimental.pallas import tpu as pltpu

EPS = 1e-6
LANES = 128
HGRN_CHUNK = 128
ROW_CHUNK = 256
F32 = jnp.float32
BF16 = jnp.bfloat16
VMEM_LIMIT_BYTES = 56 * 1024 * 1024

_NT = (((1,), (1,)), ((), ()))
_TN = (((0,), (0,)), ((), ()))


def _params(*sem):
    return pltpu.CompilerParams(dimension_semantics=sem, vmem_limit_bytes=VMEM_LIMIT_BYTES)


def _rms_bf16(x, gain):
    ms = jnp.mean(x * x, axis=-1, keepdims=True)
    return (x * lax.rsqrt(ms + EPS) * gain).astype(BF16)


def _silu(x):
    return x * jax.nn.sigmoid(x)


def _group_rms(y, gain, scale):
    outs = []
    for c in range(y.shape[1] // LANES):
        sl = slice(c * LANES, (c + 1) * LANES)
        yc = y[:, sl]
        ms = jnp.mean(yc * yc, axis=-1, keepdims=True)
        outs.append(yc * lax.rsqrt(ms + EPS) * (gain[:, sl] * scale))
    return jnp.concatenate(outs, axis=1)


def _row_chunks(tm):
    rc = min(ROW_CHUNK, tm)
    return [slice(r * rc, (r + 1) * rc) for r in range(tm // rc)]


def _norm_linear_kernel(x_ref, g_ref, w_ref, aux_ref, *rest, sections, sec_blocks, lb_rows, scale):
    o_refs, h_scr = rest[:-1], rest[-1]
    j = pl.program_id(1)

    def epilogue(name, acc):
        if name == "silu":
            return _silu(acc)
        if name == "plain":
            return acc
        if name == "head_rms":
            return _group_rms(acc, aux_ref[...], scale)
        if name == "log_forget":
            a = aux_ref[...]
            e = jnp.exp(a - jnp.max(a, axis=0, keepdims=True))
            lb = jnp.sum(e[:lb_rows], axis=0, keepdims=True) / jnp.sum(e, axis=0, keepdims=True)
            return jnp.log(lb + (1.0 - lb) * jax.nn.sigmoid(acc))
        raise ValueError(name)

    def run(first, name, o_ref):
        for rows in _row_chunks(x_ref.shape[0]):
            if first:
                h = _rms_bf16(x_ref[rows, :], g_ref[...])
                h_scr[rows, :] = h
            else:
                h = h_scr[rows, :]
            acc = jnp.dot(h, w_ref[...], preferred_element_type=F32)
            o_ref[rows, :] = epilogue(name, acc).astype(o_ref.dtype)

    for s, (name, out_idx) in enumerate(sections):
        lo, hi = s * sec_blocks, (s + 1) * sec_blocks
        if s == 0:
            pl.when(j == 0)(functools.partial(run, True, name, o_refs[out_idx]))
            lo = 1
        if lo < hi:
            pl.when(jnp.logical_and(j >= lo, j < hi))(functools.partial(run, False, name, o_refs[out_idx]))


def _norm_linear(x2, gain, w, aux, *, sections, aux_section, sec_width, out_dtypes, name, lb_rows=1, scale=1.0,
                 tm=1024, tn=1024):
    t, k = x2.shape
    n = w.shape[1]
    tm, tn = min(tm, t), min(tn, sec_width)
    sb = sec_width // tn
    first_step = [None] * len(out_dtypes)
    n_blocks = [0] * len(out_dtypes)
    for s, (_, o) in enumerate(sections):
        if first_step[o] is None:
            first_step[o] = s * sb
        assert s * sb == first_step[o] + n_blocks[o], "an output's sections must be consecutive"
        n_blocks[o] += sb

    def out_spec(o):
        return pl.BlockSpec((tm, tn), lambda i, j: (i, jnp.clip(j - first_step[o], 0, n_blocks[o] - 1)))

    kern = functools.partial(_norm_linear_kernel, sections=tuple(sections), sec_blocks=sb, lb_rows=lb_rows, scale=scale)
    return pl.pallas_call(
        kern,
        grid=(t // tm, n // tn),
        in_specs=[
            pl.BlockSpec((tm, k), lambda i, j: (i, 0)),
            pl.BlockSpec((1, k), lambda i, j: (0, 0)),
            pl.BlockSpec((k, tn), lambda i, j: (0, j)),
            pl.BlockSpec((aux.shape[0], tn), lambda i, j: (0, jnp.clip(j - aux_section * sb, 0, sb - 1))),
        ],
        out_specs=[out_spec(o) for o in range(len(out_dtypes))],
        out_shape=[jax.ShapeDtypeStruct((t, n_blocks[o] * tn), out_dtypes[o]) for o in range(len(out_dtypes))],
        scratch_shapes=[pltpu.VMEM((tm, k), BF16)],
        compiler_params=_params("parallel", "arbitrary"),
        name=name,
    )(x2, gain.reshape(1, k), w.astype(BF16), aux)


def _mixer_out_mlp_kernel(x_ref, a_ref, wo_ref, g_ref, up_ref, down_ref, o_ref, h_scr):
    def run(first):
        for rows in _row_chunks(x_ref.shape[0]):
            if first:
                base = x_ref[rows, :] + jnp.dot(a_ref[rows, :], wo_ref[...], preferred_element_type=F32)
                h = _rms_bf16(base, g_ref[...])
                h_scr[rows, :] = h
            else:
                h = h_scr[rows, :]
                base = o_ref[rows, :]
            u = jnp.dot(h, up_ref[...], preferred_element_type=F32)
            a = jnp.square(jnp.maximum(u, 0.0)).astype(BF16)
            o_ref[rows, :] = base + jnp.dot(a, down_ref[...], preferred_element_type=F32)

    j = pl.program_id(1)
    pl.when(j == 0)(lambda: run(True))
    pl.when(j > 0)(lambda: run(False))


def _mixer_out_mlp(x2, a, w_out, gain, up, down, *, name, tm=512, tf=1024):
    t, d = x2.shape
    ka = a.shape[1]
    f = up.shape[1]
    tm, tf = min(tm, t), min(tf, f)
    return pl.pallas_call(
        _mixer_out_mlp_kernel,
        grid=(t // tm, f // tf),
        in_specs=[
            pl.BlockSpec((tm, d), lambda i, j: (i, 0)),
            pl.BlockSpec((tm, ka), lambda i, j: (i, 0)),
            pl.BlockSpec((ka, d), lambda i, j: (0, 0), pipeline_mode=pl.Buffered(1)),
            pl.BlockSpec((1, d), lambda i, j: (0, 0)),
            pl.BlockSpec((d, tf), lambda i, j: (0, j)),
            pl.BlockSpec((tf, d), lambda i, j: (j, 0)),
        ],
        out_specs=pl.BlockSpec((tm, d), lambda i, j: (i, 0)),
        out_shape=jax.ShapeDtypeStruct((t, d), F32),
        scratch_shapes=[pltpu.VMEM((tm, d), BF16)],
        compiler_params=_params("parallel", "arbitrary"),
        name=name,
    )(x2, a, w_out.astype(BF16), gain.reshape(1, d), up.astype(BF16), down.astype(BF16))


def _ple_kernel(x_ref, g_ref, p_ref, gw_ref, pw_ref, o_ref, h_scr, *, tn):
    j = pl.program_id(1)
    cols = pl.ds(pl.multiple_of(j * tn, tn), tn)

    def run(first):
        for rows in _row_chunks(x_ref.shape[0]):
            if first:
                h = _rms_bf16(x_ref[rows, :], g_ref[...])
                h_scr[rows, :] = h
            else:
                h = h_scr[rows, :]
            gate = jax.nn.sigmoid(jnp.dot(h, gw_ref[...], preferred_element_type=F32))
            emb = jnp.dot(p_ref[rows, :].astype(BF16), pw_ref[...], preferred_element_type=F32)
            o_ref[rows, :] = x_ref[rows, cols] + gate * emb

    pl.when(j == 0)(lambda: run(True))
    pl.when(j > 0)(lambda: run(False))


def _ple(x2, gain, p2, gate_w, proj_w, *, name, tm=1024, tn=1024):
    t, d = x2.shape
    pd = p2.shape[1]
    tm, tn = min(tm, t), min(tn, d)
    return pl.pallas_call(
        functools.partial(_ple_kernel, tn=tn),
        grid=(t // tm, d // tn),
        in_specs=[
            pl.BlockSpec((tm, d), lambda i, j: (i, 0)),
            pl.BlockSpec((1, d), lambda i, j: (0, 0)),
            pl.BlockSpec((tm, pd), lambda i, j: (i, 0)),
            pl.BlockSpec((d, tn), lambda i, j: (0, j)),
            pl.BlockSpec((pd, tn), lambda i, j: (0, j)),
        ],
        out_specs=pl.BlockSpec((tm, tn), lambda i, j: (i, j)),
        out_shape=jax.ShapeDtypeStruct((t, d), F32),
        scratch_shapes=[pltpu.VMEM((tm, d), BF16)],
        compiler_params=_params("parallel", "arbitrary"),
        name=name,
    )(x2, gain.reshape(1, d), p2, gate_w.astype(BF16), proj_w.astype(BF16))


def _hgrn_levels(c):
    return [c >> (l + 1) for l in range(int(math.log2(c)))]


def _hgrn_constants(c):
    idx = np.arange(c)
    x = idx[:, None] ^ idx[None, :]
    masks = [((x >= m) & (x < 2 * m) & ((idx[:, None] & m) != 0)).astype(np.float32) for m in _hgrn_levels(c)]
    return np.tril(np.ones((c, c), np.float32)), np.stack(masks, 0)


def _level_log_decay(g_cum, g, m):
    c, w = g_cum.shape
    if m == 1:
        odd = (lax.broadcasted_iota(jnp.int32, (c, w), 0) & 1) != 0
        return jnp.where(odd, g, 0.0)
    if m == 2:
        g3 = g_cum.reshape(c // 8, 8, w)
        upper_block = lax.broadcasted_iota(jnp.int32, g3.shape, 1) < 4
        mid = jnp.where(upper_block, g3[:, 1:2, :], g3[:, 5:6, :])
    else:
        g3 = g_cum.reshape(c // (2 * m), 2 * m, w)
        mid = g3[:, m - 1:m, :]
    return (-jnp.abs(g3 - mid)).reshape(c, w)


def _hgrn_kernel(q_ref, f_ref, v_ref, gate_ref, tri_ref, msk_ref, on_ref, o_ref, st_scr, *, c, hb, nchunks):
    @pl.when(pl.program_id(2) == 0)
    def _():
        st_scr[...] = jnp.zeros_like(st_scr)

    levels = _hgrn_levels(c)
    rowid = lax.broadcasted_iota(jnp.int32, (c, LANES), 0)

    def chunk(ci, carry):
        rows = pl.ds(pl.multiple_of(ci * c, c), c)
        g_all = f_ref[0, rows, :]
        g_hi = g_all.astype(BF16)
        g_lo = (g_all - g_hi.astype(F32)).astype(BF16)
        tri = tri_ref[...]
        g_cum_all = (jnp.dot(tri, g_hi, preferred_element_type=F32)
                     + jnp.dot(tri, g_lo, preferred_element_type=F32))
        k_all = 1.0 - jnp.exp(g_all)
        g_last = g_cum_all[c - 1:c, :]
        q_dec_all = jnp.exp(g_cum_all)
        k_dec_all = jnp.exp(g_last - g_cum_all)
        st_dec_all = jnp.exp(g_last)
        lvl_dec_all = [jnp.exp(_level_log_decay(g_cum_all, g_all, m)) for m in levels]
        for h in range(hb):
            hl = slice(h * LANES, (h + 1) * LANES)
            qs = q_ref[0, rows, hl].astype(F32)
            vb = v_ref[0, rows, hl]
            v = vb.astype(F32)
            k = k_all[:, hl]
            a = jnp.zeros((c, c), F32)
            for l, m in enumerate(levels):
                x = (jnp.where((rowid & m) != 0, qs, k) * lvl_dec_all[l][:, hl]).astype(BF16)
                a = a + msk_ref[l] * lax.dot_general(x, x, _NT, preferred_element_type=F32)
            o = jnp.dot(a.astype(BF16), vb, preferred_element_type=F32)
            o = o + jnp.sum(qs * k, axis=-1, keepdims=True) * v
            st = st_scr[h]
            qg = (qs * q_dec_all[:, hl]).astype(BF16)
            o = o + lax.dot_general(qg, st.astype(BF16), _NT, preferred_element_type=F32)
            kt = (k * k_dec_all[:, hl]).astype(BF16)
            st_scr[h] = st * st_dec_all[:, hl] + lax.dot_general(vb, kt, _TN, preferred_element_type=F32)
            ms = jnp.mean(o * o, axis=-1, keepdims=True)
            y = o * lax.rsqrt(ms + EPS) * on_ref[...] * gate_ref[0, rows, hl].astype(F32)
            o_ref[0, rows, hl] = y.astype(o_ref.dtype)
        return carry

    lax.fori_loop(0, nchunks, chunk, 0)


def _hgrn(qig, log_f, onorm, *, name, rows=512, hb=4):
    b, s, width = log_f.shape
    c = HGRN_CHUNK
    rows = min(rows, s)
    hw = hb * LANES
    nsb = width // hw
    tri, masks = _hgrn_constants(c)
    nl = masks.shape[0]

    def sec(k):
        return pl.BlockSpec((1, rows, hw), lambda bi, hg, r: (bi, r, k * nsb + hg))

    kern = functools.partial(_hgrn_kernel, c=c, hb=hb, nchunks=rows // c)
    return pl.pallas_call(
        kern,
        grid=(b, nsb, s // rows),
        in_specs=[
            sec(0), sec(0), sec(1), sec(2),
            pl.BlockSpec((c, c), lambda bi, hg, r: (0, 0)),
            pl.BlockSpec((nl, c, c), lambda bi, hg, r: (0, 0, 0)),
            pl.BlockSpec((1, LANES), lambda bi, hg, r: (0, 0)),
        ],
        out_specs=pl.BlockSpec((1, rows, hw), lambda bi, hg, r: (bi, r, hg)),
        out_shape=jax.ShapeDtypeStruct((b, s, width), BF16),
        scratch_shapes=[pltpu.VMEM((hb, LANES, LANES), F32)],
        compiler_params=_params("parallel", "parallel", "arbitrary"),
        name=name,
    )(qig, log_f, qig, qig, jnp.asarray(tri, BF16), jnp.asarray(masks, F32), onorm.reshape(1, LANES))


def _attn_kernel(q_ref, k_ref, v_ref, sub_ref, lq1_ref, lk1_ref, lq2_ref, lk2_ref, o_ref,
                 m_scr, l_scr, acc_scr, s_scr, *, tq, lam_init):
    qi = pl.program_id(2)
    m_scr[...] = jnp.full_like(m_scr, -1e30)
    l_scr[...] = jnp.zeros_like(l_scr)
    acc_scr[...] = jnp.zeros_like(acc_scr)
    hw = acc_scr.shape[-1]

    def scores(j, slot):
        keys = pl.ds(pl.multiple_of(j * tq, tq), tq)
        for c in range(2):
            sl = slice(c * LANES, (c + 1) * LANES)
            s_scr[slot, c] = lax.dot_general(q_ref[0, :, sl], k_ref[0, keys, sl], _NT,
                                             preferred_element_type=F32)

    def softmax_pv(j, slot, masked):
        base = pl.multiple_of(j * tq, tq)
        for qrows in _row_chunks(tq):
            nk = qrows.stop if masked else tq
            vb = v_ref[0, pl.ds(base, nk), :]
            for c in range(2):
                s = s_scr[slot, c, qrows, 0:nk]
                if masked:
                    row = lax.broadcasted_iota(jnp.int32, s.shape, 0) + qrows.start
                    col = lax.broadcasted_iota(jnp.int32, s.shape, 1)
                    s = jnp.where(col <= row, s, -jnp.inf)
                m_prev = m_scr[c, qrows, :]
                m_new = jnp.maximum(m_prev, jnp.max(s, axis=-1, keepdims=True))
                alpha = jnp.exp(m_prev - m_new)
                p = jnp.exp(s - jnp.tile(m_new, (1, nk // LANES)))
                l_scr[c, qrows, :] = alpha * l_scr[c, qrows, :] + jnp.sum(p, axis=-1, keepdims=True)
                acc_scr[c, qrows, :] = (jnp.tile(alpha, (1, hw // LANES)) * acc_scr[c, qrows, :]
                                        + jnp.dot(p.astype(BF16), vb, preferred_element_type=F32))
                m_scr[c, qrows, :] = m_new

    scores(0, 0)

    def body(u, carry):
        j = 2 * u
        scores(j + 1, 1)
        softmax_pv(j, 0, False)
        scores(j + 2, 0)
        softmax_pv(j + 1, 1, False)
        return carry

    lax.fori_loop(0, qi // 2, body, 0)

    @pl.when(qi % 2 == 1)
    def _():
        scores(qi, 1)
        softmax_pv(qi - 1, 0, False)
        softmax_pv(qi, 1, True)

    @pl.when(qi % 2 == 0)
    def _():
        softmax_pv(qi, 0, True)

    lam = (jnp.exp(jnp.sum(lq1_ref[...] * lk1_ref[...], axis=-1, keepdims=True))
           - jnp.exp(jnp.sum(lq2_ref[...] * lk2_ref[...], axis=-1, keepdims=True))
           + lam_init)
    inv1 = jnp.tile(1.0 / l_scr[0], (1, hw // LANES))
    inv2 = jnp.tile(1.0 / l_scr[1], (1, hw // LANES))
    o = acc_scr[0] * inv1 - lam * (acc_scr[1] * inv2)
    ms = jnp.mean(o * o, axis=-1, keepdims=True)
    o_ref[0] = (o * lax.rsqrt(ms + EPS) * (sub_ref[...] * (1.0 - lam_init))).astype(o_ref.dtype)


def _diff_attention(q, kv, subln, lq1, lk1, lq2, lk2, *, lam_init, name, tq=512):
    b, s, width = q.shape
    hw = 2 * LANES
    nh = width // hw
    tq = min(tq, s)
    vec = pl.BlockSpec((1, LANES), lambda bi, h, i: (0, 0))
    return pl.pallas_call(
        functools.partial(_attn_kernel, tq=tq, lam_init=lam_init),
        grid=(b, nh, s // tq),
        in_specs=[
            pl.BlockSpec((1, tq, hw), lambda bi, h, i: (bi, i, h)),
            pl.BlockSpec((1, s, hw), lambda bi, h, i: (bi, 0, h)),
            pl.BlockSpec((1, s, hw), lambda bi, h, i: (bi, 0, nh + h)),
            pl.BlockSpec((1, hw), lambda bi, h, i: (0, 0)),
            vec, vec, vec, vec,
        ],
        out_specs=pl.BlockSpec((1, tq, hw), lambda bi, h, i: (bi, i, h)),
        out_shape=jax.ShapeDtypeStruct((b, s, width), BF16),
        scratch_shapes=[
            pltpu.VMEM((2, tq, LANES), F32),
            pltpu.VMEM((2, tq, LANES), F32),
            pltpu.VMEM((2, tq, hw), F32),
            pltpu.VMEM((2, 2, tq, tq), F32),
        ],
        compiler_params=_params("parallel", "parallel", "arbitrary"),
        name=name,
    )(q, kv, kv, subln.reshape(1, hw), lq1.reshape(1, LANES), lk1.reshape(1, LANES),
      lq2.reshape(1, LANES), lk2.reshape(1, LANES))


def kernel(x, p, ln_mix, ln_mlp, ln_ple, a_w_in, a_lb, a_onorm, a_w_out, kv_norm, w_k, w_v, k_norm, b_w_q, q_norm, lam_q1, lam_k1, lam_q2, lam_k2, b_subln, b_w_out, mlp_up, mlp_down, ple_proj, ple_gate):
    b, s, d = x.shape
    depth = ln_mix.shape[0]
    n_a = a_w_in.shape[0]
    t = b * s
    x2 = x.reshape(t, d)
    kv = None
    for i in range(depth):
        if i < n_a:
            j = i
            width = a_w_out.shape[1]
            wq, wf, wi, wg = jnp.split(a_w_in[j], 4, axis=1)
            qig, log_f = _norm_linear(
                x2, ln_mix[i], jnp.concatenate([wq, wi, wg, wf], axis=1), a_lb,
                sections=(("silu", 0), ("plain", 0), ("silu", 0), ("log_forget", 1)), aux_section=3,
                sec_width=width, out_dtypes=(BF16, F32), lb_rows=i + 1, name=f"hgrn_in_{i}")
            o = _hgrn(qig.reshape(b, s, 3 * width), log_f.reshape(b, s, width), a_onorm[j], name=f"hgrn_{i}")
            w_out = a_w_out[j]
        else:
            j = i - n_a
            width = b_w_q.shape[2]
            head_gain = jnp.tile(q_norm[j].reshape(1, -1), (1, width // (2 * LANES)))
            (q,) = _norm_linear(x2, ln_mix[i], b_w_q[j], head_gain, sections=(("head_rms", 0),), aux_section=0,
                                sec_width=width, out_dtypes=(BF16,), scale=1.0 / math.sqrt(LANES),
                                name=f"attn_q_{i}")
            lam_init = 0.8 - 0.6 * math.exp(-0.3 * i)
            o = _diff_attention(q.reshape(b, s, width), kv, b_subln[j], lam_q1[j], lam_k1[j], lam_q2[j],
                                lam_k2[j], lam_init=lam_init, name=f"attn_{i}")
            w_out = b_w_out[j]
        x2 = _mixer_out_mlp(x2, o.reshape(t, width), w_out, ln_mlp[i], mlp_up[i], mlp_down[i], name=f"mlp_{i}")
        x2 = _ple(x2, ln_ple[i], p[i].reshape(t, -1), ple_gate[i], ple_proj[i], name=f"ple_{i}")
        if i == n_a - 1:
            width = w_k.shape[1]
            head_gain = jnp.tile(k_norm.reshape(1, -1), (1, width // (2 * LANES)))
            (kv,) = _norm_linear(x2, kv_norm, jnp.concatenate([w_k, w_v], axis=1), head_gain,
                                 sections=(("head_rms", 0), ("plain", 0)), aux_section=0, sec_width=width,
                                 out_dtypes=(BF16,), name="shared_kv")
            kv = kv.reshape(b, s, 2 * width)
    return x2.reshape(b, s, d)
```

```python
import functools
import math

import numpy as np
import jax
import jax.numpy as jnp
from jax import lax
from jax.experimental import pallas as pl
from jax.experimental.pallas import tpu as pltpu

EPS = 1e-6
LANES = 128
HGRN_CHUNK = 128
ROW_CHUNK = 256
SAFE_LOG2_SHIFT = 60.0
F32 = jnp.float32
BF16 = jnp.bfloat16
VMEM_LIMIT_BYTES = 56 * 1024 * 1024

_NT = (((1,), (1,)), ((), ()))
_TN = (((0,), (0,)), ((), ()))


def _params(*sem):
    return pltpu.CompilerParams(dimension_semantics=sem, vmem_limit_bytes=VMEM_LIMIT_BYTES)


def _rms_bf16(x, gain):
    ms = jnp.mean(x * x, axis=-1, keepdims=True)
    return (x * lax.rsqrt(ms + EPS) * gain).astype(BF16)


def _silu(x):
    return x * jax.nn.sigmoid(x)


def _group_rms(y, gain, scale):
    outs = []
    for c in range(y.shape[1] // LANES):
        sl = slice(c * LANES, (c + 1) * LANES)
        yc = y[:, sl]
        ms = jnp.mean(yc * yc, axis=-1, keepdims=True)
        outs.append(yc * lax.rsqrt(ms + EPS) * (gain[:, sl] * scale))
    return jnp.concatenate(outs, axis=1)


def _row_chunks(tm):
    rc = min(ROW_CHUNK, tm)
    return [slice(r * rc, (r + 1) * rc) for r in range(tm // rc)]


def _norm_linear_kernel(x_ref, g_ref, w_ref, aux_ref, *rest, sections, sec_blocks, lb_rows, scale):
    o_refs, h_scr = rest[:-1], rest[-1]
    j = pl.program_id(1)

    def epilogue(name, acc):
        if name == "silu":
            return _silu(acc)
        if name == "plain":
            return acc
        if name == "head_rms":
            return _group_rms(acc, aux_ref[...], scale)
        if name == "log_forget":
            a = aux_ref[...]
            e = jnp.exp(a - jnp.max(a, axis=0, keepdims=True))
            lb = jnp.sum(e[:lb_rows], axis=0, keepdims=True) / jnp.sum(e, axis=0, keepdims=True)
            return jnp.log2(lb + (1.0 - lb) * jax.nn.sigmoid(acc))
        raise ValueError(name)

    def run(first, name, o_ref):
        for rows in _row_chunks(x_ref.shape[0]):
            if first:
                h = _rms_bf16(x_ref[rows, :], g_ref[...])
                h_scr[rows, :] = h
            else:
                h = h_scr[rows, :]
            acc = jnp.dot(h, w_ref[...], preferred_element_type=F32)
            o_ref[rows, :] = epilogue(name, acc).astype(o_ref.dtype)

    for s, (name, out_idx) in enumerate(sections):
        lo, hi = s * sec_blocks, (s + 1) * sec_blocks
        if s == 0:
            pl.when(j == 0)(functools.partial(run, True, name, o_refs[out_idx]))
            lo = 1
        if lo < hi:
            pl.when(jnp.logical_and(j >= lo, j < hi))(functools.partial(run, False, name, o_refs[out_idx]))


def _norm_linear(x2, gain, w, aux, *, sections, aux_section, sec_width, out_dtypes, name, lb_rows=1, scale=1.0,
                 tm=1024, tn=1024):
    t, k = x2.shape
    n = w.shape[1]
    tm, tn = min(tm, t), min(tn, sec_width)
    sb = sec_width // tn
    first_step = [None] * len(out_dtypes)
    n_blocks = [0] * len(out_dtypes)
    for s, (_, o) in enumerate(sections):
        if first_step[o] is None:
            first_step[o] = s * sb
        assert s * sb == first_step[o] + n_blocks[o], "an output's sections must be consecutive"
        n_blocks[o] += sb

    def out_spec(o):
        return pl.BlockSpec((tm, tn), lambda i, j: (i, jnp.clip(j - first_step[o], 0, n_blocks[o] - 1)))

    kern = functools.partial(_norm_linear_kernel, sections=tuple(sections), sec_blocks=sb, lb_rows=lb_rows, scale=scale)
    return pl.pallas_call(
        kern,
        grid=(t // tm, n // tn),
        in_specs=[
            pl.BlockSpec((tm, k), lambda i, j: (i, 0)),
            pl.BlockSpec((1, k), lambda i, j: (0, 0)),
            pl.BlockSpec((k, tn), lambda i, j: (0, j)),
            pl.BlockSpec((aux.shape[0], tn), lambda i, j: (0, jnp.clip(j - aux_section * sb, 0, sb - 1))),
        ],
        out_specs=[out_spec(o) for o in range(len(out_dtypes))],
        out_shape=[jax.ShapeDtypeStruct((t, n_blocks[o] * tn), out_dtypes[o]) for o in range(len(out_dtypes))],
        scratch_shapes=[pltpu.VMEM((tm, k), BF16)],
        compiler_params=_params("parallel", "arbitrary"),
        name=name,
    )(x2, gain.reshape(1, k), w.astype(BF16), aux)


def _mixer_out_mlp_kernel(x_ref, a_ref, wo_ref, g_ref, up_ref, down_ref, o_ref, h_scr):
    def run(first):
        for rows in _row_chunks(x_ref.shape[0]):
            if first:
                base = x_ref[rows, :] + jnp.dot(a_ref[rows, :], wo_ref[...], preferred_element_type=F32)
                h = _rms_bf16(base, g_ref[...])
                h_scr[rows, :] = h
            else:
                h = h_scr[rows, :]
                base = o_ref[rows, :]
            u = jnp.dot(h, up_ref[...], preferred_element_type=F32)
            a = jnp.square(jnp.maximum(u, 0.0)).astype(BF16)
            o_ref[rows, :] = base + jnp.dot(a, down_ref[...], preferred_element_type=F32)

    j = pl.program_id(1)
    pl.when(j == 0)(lambda: run(True))
    pl.when(j > 0)(lambda: run(False))


def _mixer_out_mlp(x2, a, w_out, gain, up, down, *, name, tm=512, tf=1024):
    t, d = x2.shape
    ka = a.shape[1]
    f = up.shape[1]
    tm, tf = min(tm, t), min(tf, f)
    return pl.pallas_call(
        _mixer_out_mlp_kernel,
        grid=(t // tm, f // tf),
        in_specs=[
            pl.BlockSpec((tm, d), lambda i, j: (i, 0)),
            pl.BlockSpec((tm, ka), lambda i, j: (i, 0)),
            pl.BlockSpec((ka, d), lambda i, j: (0, 0), pipeline_mode=pl.Buffered(1)),
            pl.BlockSpec((1, d), lambda i, j: (0, 0)),
            pl.BlockSpec((d, tf), lambda i, j: (0, j)),
            pl.BlockSpec((tf, d), lambda i, j: (j, 0)),
        ],
        out_specs=pl.BlockSpec((tm, d), lambda i, j: (i, 0)),
        out_shape=jax.ShapeDtypeStruct((t, d), F32),
        scratch_shapes=[pltpu.VMEM((tm, d), BF16)],
        compiler_params=_params("parallel", "arbitrary"),
        name=name,
    )(x2, a, w_out.astype(BF16), gain.reshape(1, d), up.astype(BF16), down.astype(BF16))


def _ple_kernel(x_ref, g_ref, p_ref, gw_ref, pw_ref, o_ref, h_scr, *, tn):
    j = pl.program_id(1)
    cols = pl.ds(pl.multiple_of(j * tn, tn), tn)

    def run(first):
        for rows in _row_chunks(x_ref.shape[0]):
            if first:
                h = _rms_bf16(x_ref[rows, :], g_ref[...])
                h_scr[rows, :] = h
            else:
                h = h_scr[rows, :]
            gate = jax.nn.sigmoid(jnp.dot(h, gw_ref[...], preferred_element_type=F32))
            emb = jnp.dot(p_ref[rows, :].astype(BF16), pw_ref[...], preferred_element_type=F32)
            o_ref[rows, :] = x_ref[rows, cols] + gate * emb

    pl.when(j == 0)(lambda: run(True))
    pl.when(j > 0)(lambda: run(False))


def _ple(x2, gain, p2, gate_w, proj_w, *, name, tm=1024, tn=1024):
    t, d = x2.shape
    pd = p2.shape[1]
    tm, tn = min(tm, t), min(tn, d)
    return pl.pallas_call(
        functools.partial(_ple_kernel, tn=tn),
        grid=(t // tm, d // tn),
        in_specs=[
            pl.BlockSpec((tm, d), lambda i, j: (i, 0)),
            pl.BlockSpec((1, d), lambda i, j: (0, 0)),
            pl.BlockSpec((tm, pd), lambda i, j: (i, 0)),
            pl.BlockSpec((d, tn), lambda i, j: (0, j)),
            pl.BlockSpec((pd, tn), lambda i, j: (0, j)),
        ],
        out_specs=pl.BlockSpec((tm, tn), lambda i, j: (i, j)),
        out_shape=jax.ShapeDtypeStruct((t, d), F32),
        scratch_shapes=[pltpu.VMEM((tm, d), BF16)],
        compiler_params=_params("parallel", "arbitrary"),
        name=name,
    )(x2, gain.reshape(1, d), p2, gate_w.astype(BF16), proj_w.astype(BF16))


def _hgrn_levels(c):
    return [c >> (l + 1) for l in range(int(math.log2(c)))]


def _hgrn_constants(c):
    idx = np.arange(c)
    x = idx[:, None] ^ idx[None, :]
    masks = [((x >= m) & (x < 2 * m) & ((idx[:, None] & m) != 0)).astype(np.float32) for m in _hgrn_levels(c)]
    return np.tril(np.ones((c, c), np.float32)), np.stack(masks, 0)


def _level_log_decay(g_cum, g, m):
    c, w = g_cum.shape
    if m == 1:
        odd = (lax.broadcasted_iota(jnp.int32, (c, w), 0) & 1) != 0
        return jnp.where(odd, g, 0.0)
    if m < 8:
        g3 = g_cum.reshape(c // 8, 8, w)
        sub = lax.broadcasted_iota(jnp.int32, g3.shape, 1)
        if m == 4:
            mid = g3[:, 3:4, :]
        else:
            mid = jnp.where(sub < 4, g3[:, 1:2, :], g3[:, 5:6, :])
        return (-jnp.abs(g3 - mid)).reshape(c, w)
    g3 = g_cum.reshape(c // (2 * m), 2 * m, w)
    mid = g3[:, m - 1:m, :]
    return jnp.concatenate([mid - g3[:, :m, :], g3[:, m:, :] - mid], axis=1).reshape(c, w)


def _pair_rows(lower, upper, m):
    c, w = lower.shape
    if m < 8:
        return jnp.where((lax.broadcasted_iota(jnp.int32, (c, w), 0) & m) != 0, lower, upper)
    lo3 = lower.reshape(c // (2 * m), 2 * m, w)
    up3 = upper.reshape(c // (2 * m), 2 * m, w)
    return jnp.concatenate([up3[:, :m, :], lo3[:, m:, :]], axis=1).reshape(c, w)


def _hgrn_kernel(q_ref, f_ref, v_ref, gate_ref, tri_ref, msk_ref, on_ref, o_ref, st_scr, *, c, hb, nchunks):
    @pl.when(pl.program_id(2) == 0)
    def _():
        st_scr[...] = jnp.zeros_like(st_scr)

    levels = _hgrn_levels(c)

    def chunk(ci, carry):
        rows = pl.ds(pl.multiple_of(ci * c, c), c)
        g_all = f_ref[0, rows, :]
        g_hi = g_all.astype(BF16)
        g_lo = (g_all - g_hi.astype(F32)).astype(BF16)
        tri = tri_ref[...]
        g_cum_all = (jnp.dot(tri, g_hi, preferred_element_type=F32)
                     + jnp.dot(tri, g_lo, preferred_element_type=F32))
        k_all = 1.0 - jnp.exp2(g_all)
        g_last = g_cum_all[c - 1:c, :]
        q_dec_all = jnp.exp2(g_cum_all)
        k_dec_all = jnp.exp2(g_last - g_cum_all)
        st_dec_all = jnp.exp2(g_last)
        lvl_dec_all = [jnp.exp2(_level_log_decay(g_cum_all, g_all, m)) for m in levels]
        for h in range(hb):
            hl = slice(h * LANES, (h + 1) * LANES)
            qs = q_ref[0, rows, hl].astype(F32)
            vb = v_ref[0, rows, hl]
            v = vb.astype(F32)
            k = k_all[:, hl]
            a = jnp.zeros((c, c), F32)
            for l, m in enumerate(levels):
                x = (_pair_rows(qs, k, m) * lvl_dec_all[l][:, hl]).astype(BF16)
                a = a + msk_ref[l] * lax.dot_general(x, x, _NT, preferred_element_type=F32)
            o = jnp.dot(a.astype(BF16), vb, preferred_element_type=F32)
            o = o + jnp.sum(qs * k, axis=-1, keepdims=True) * v
            st = st_scr[h]
            qg = (qs * q_dec_all[:, hl]).astype(BF16)
            o = o + lax.dot_general(qg, st.astype(BF16), _NT, preferred_element_type=F32)
            kt = (k * k_dec_all[:, hl]).astype(BF16)
            st_scr[h] = st * st_dec_all[:, hl] + lax.dot_general(vb, kt, _TN, preferred_element_type=F32)
            ms = jnp.mean(o * o, axis=-1, keepdims=True)
            y = o * lax.rsqrt(ms + EPS) * on_ref[...] * gate_ref[0, rows, hl].astype(F32)
            o_ref[0, rows, hl] = y.astype(o_ref.dtype)
        return carry

    lax.fori_loop(0, nchunks, chunk, 0)


def _hgrn(qig, log_f, onorm, *, name, rows=512, hb=4):
    b, s, width = log_f.shape
    c = HGRN_CHUNK
    rows = min(rows, s)
    hw = hb * LANES
    nsb = width // hw
    tri, masks = _hgrn_constants(c)
    nl = masks.shape[0]

    def sec(k):
        return pl.BlockSpec((1, rows, hw), lambda bi, hg, r: (bi, r, k * nsb + hg))

    kern = functools.partial(_hgrn_kernel, c=c, hb=hb, nchunks=rows // c)
    return pl.pallas_call(
        kern,
        grid=(b, nsb, s // rows),
        in_specs=[
            sec(0), sec(0), sec(1), sec(2),
            pl.BlockSpec((c, c), lambda bi, hg, r: (0, 0)),
            pl.BlockSpec((nl, c, c), lambda bi, hg, r: (0, 0, 0)),
            pl.BlockSpec((1, LANES), lambda bi, hg, r: (0, 0)),
        ],
        out_specs=pl.BlockSpec((1, rows, hw), lambda bi, hg, r: (bi, r, hg)),
        out_shape=jax.ShapeDtypeStruct((b, s, width), BF16),
        scratch_shapes=[pltpu.VMEM((hb, LANES, LANES), F32)],
        compiler_params=_params("parallel", "parallel", "arbitrary"),
        name=name,
    )(qig, log_f, qig, qig, jnp.asarray(tri, BF16), jnp.asarray(masks, F32), onorm.reshape(1, LANES))


def _attn_kernel(q_ref, k_ref, v_ref, sub_ref, lq1_ref, lk1_ref, lq2_ref, lk2_ref, o_ref,
                 m_scr, l_scr, acc_scr, kmax_scr, *, tq, lam_init):
    qi = pl.program_id(2)
    hw = acc_scr.shape[-1]
    maps = [slice(c * LANES, (c + 1) * LANES) for c in range(2)]

    @pl.when(qi == 0)
    def _():
        def key_block(i, carry):
            kb = k_ref[0, pl.ds(pl.multiple_of(i * tq, tq), tq), :].astype(F32)
            sq = [jnp.sum(kb[:, sl] * kb[:, sl], axis=-1, keepdims=True) for sl in maps]
            return tuple(jnp.maximum(carry[c], jnp.max(sq[c], axis=0, keepdims=True)) for c in range(2))

        zero = jnp.zeros((1, 1), F32)
        k_sq = lax.fori_loop(0, k_ref.shape[1] // tq, key_block, (zero, zero))
        for c in range(2):
            kmax_scr[c] = jnp.broadcast_to(jnp.sqrt(k_sq[c]), (1, LANES))

    qf = q_ref[0].astype(F32)
    for c, sl in enumerate(maps):
        q_norm = jnp.sqrt(jnp.sum(qf[:, sl] * qf[:, sl], axis=-1, keepdims=True))
        m_scr[c] = q_norm * kmax_scr[c]
    bounded = jnp.max(m_scr[...]) < SAFE_LOG2_SHIFT
    l_scr[...] = jnp.zeros_like(l_scr)
    acc_scr[...] = jnp.zeros_like(acc_scr)

    def causal(shape, row0):
        row = lax.broadcasted_iota(jnp.int32, shape, 0) + row0
        return lax.broadcasted_iota(jnp.int32, shape, 1) <= row

    def block(j, masked, running_max):
        base = pl.multiple_of(j * tq, tq)
        for qrows in _row_chunks(tq):
            nk = qrows.stop if masked else tq
            keys = pl.ds(base, nk)
            vb = v_ref[0, keys, :]
            for c, sl in enumerate(maps):
                s = lax.dot_general(q_ref[0, qrows, sl], k_ref[0, keys, sl], _NT,
                                    preferred_element_type=F32)
                if running_max:
                    if masked:
                        s = jnp.where(causal(s.shape, qrows.start), s, -jnp.inf)
                    m_prev = m_scr[c, qrows, :]
                    m_new = jnp.maximum(m_prev, jnp.max(s, axis=-1, keepdims=True))
                    alpha = jnp.exp2(m_prev - m_new)
                    p = jnp.exp2(s - jnp.tile(m_new, (1, nk // LANES)))
                    l_scr[c, qrows, :] = alpha * l_scr[c, qrows, :] + jnp.sum(p, axis=-1, keepdims=True)
                    acc_scr[c, qrows, :] = (jnp.tile(alpha, (1, hw // LANES)) * acc_scr[c, qrows, :]
                                            + jnp.dot(p.astype(BF16), vb, preferred_element_type=F32))
                    m_scr[c, qrows, :] = m_new
                else:
                    p = jnp.exp2(s - jnp.tile(m_scr[c, qrows, :], (1, nk // LANES)))
                    if masked:
                        p = jnp.where(causal(p.shape, qrows.start), p, 0.0)
                    lane_sums = p[:, 0:LANES]
                    for g in range(1, nk // LANES):
                        lane_sums = lane_sums + p[:, g * LANES:(g + 1) * LANES]
                    l_scr[c, qrows, :] += lane_sums
                    acc_scr[c, qrows, :] += jnp.dot(p.astype(BF16), vb, preferred_element_type=F32)

    def all_blocks(running_max):
        def body(j, carry):
            block(j, False, running_max)
            return carry

        lax.fori_loop(0, qi, body, 0)
        block(qi, True, running_max)

    @pl.when(bounded)
    def _():
        all_blocks(False)
        for c in range(2):
            l_scr[c] = jnp.broadcast_to(jnp.sum(l_scr[c], axis=-1, keepdims=True), l_scr.shape[1:])

    @pl.when(jnp.logical_not(bounded))
    def _():
        m_scr[...] = jnp.full_like(m_scr, -1e30)
        all_blocks(True)

    lam = (jnp.exp(jnp.sum(lq1_ref[...] * lk1_ref[...], axis=-1, keepdims=True))
           - jnp.exp(jnp.sum(lq2_ref[...] * lk2_ref[...], axis=-1, keepdims=True))
           + lam_init)
    inv1 = jnp.tile(1.0 / l_scr[0], (1, hw // LANES))
    inv2 = jnp.tile(1.0 / l_scr[1], (1, hw // LANES))
    o = acc_scr[0] * inv1 - lam * (acc_scr[1] * inv2)
    ms = jnp.mean(o * o, axis=-1, keepdims=True)
    o_ref[0] = (o * lax.rsqrt(ms + EPS) * (sub_ref[...] * (1.0 - lam_init))).astype(o_ref.dtype)


def _diff_attention(q, kv, subln, lq1, lk1, lq2, lk2, *, lam_init, name, tq=1024):
    b, s, width = q.shape
    hw = 2 * LANES
    nh = width // hw
    tq = min(tq, s)
    vec = pl.BlockSpec((1, LANES), lambda bi, h, i: (0, 0))
    return pl.pallas_call(
        functools.partial(_attn_kernel, tq=tq, lam_init=lam_init),
        grid=(b, nh, s // tq),
        in_specs=[
            pl.BlockSpec((1, tq, hw), lambda bi, h, i: (bi, i, h)),
            pl.BlockSpec((1, s, hw), lambda bi, h, i: (bi, 0, h)),
            pl.BlockSpec((1, s, hw), lambda bi, h, i: (bi, 0, nh + h)),
            pl.BlockSpec((1, hw), lambda bi, h, i: (0, 0)),
            vec, vec, vec, vec,
        ],
        out_specs=pl.BlockSpec((1, tq, hw), lambda bi, h, i: (bi, i, h)),
        out_shape=jax.ShapeDtypeStruct((b, s, width), BF16),
        scratch_shapes=[
            pltpu.VMEM((2, tq, LANES), F32),
            pltpu.VMEM((2, tq, LANES), F32),
            pltpu.VMEM((2, tq, hw), F32),
            pltpu.VMEM((2, 1, LANES), F32),
        ],
        compiler_params=_params("parallel", "parallel", "arbitrary"),
        name=name,
    )(q, kv, kv, subln.reshape(1, hw), lq1.reshape(1, LANES), lk1.reshape(1, LANES),
      lq2.reshape(1, LANES), lk2.reshape(1, LANES))


def kernel(x, p, ln_mix, ln_mlp, ln_ple, a_w_in, a_lb, a_onorm, a_w_out, kv_norm, w_k, w_v, k_norm, b_w_q, q_norm, lam_q1, lam_k1, lam_q2, lam_k2, b_subln, b_w_out, mlp_up, mlp_down, ple_proj, ple_gate):
    b, s, d = x.shape
    depth = ln_mix.shape[0]
    n_a = a_w_in.shape[0]
    t = b * s
    x2 = x.reshape(t, d)
    kv = None
    for i in range(depth):
        if i < n_a:
            j = i
            width = a_w_out.shape[1]
            wq, wf, wi, wg = jnp.split(a_w_in[j], 4, axis=1)
            qig, log_f = _norm_linear(
                x2, ln_mix[i], jnp.concatenate([wq, wi, wg, wf], axis=1), a_lb,
                sections=(("silu", 0), ("plain", 0), ("silu", 0), ("log_forget", 1)), aux_section=3,
                sec_width=width, out_dtypes=(BF16, F32), lb_rows=i + 1, name=f"hgrn_in_{i}")
            o = _hgrn(qig.reshape(b, s, 3 * width), log_f.reshape(b, s, width), a_onorm[j], name=f"hgrn_{i}")
            w_out = a_w_out[j]
        else:
            j = i - n_a
            width = b_w_q.shape[2]
            head_gain = jnp.tile(q_norm[j].reshape(1, -1), (1, width // (2 * LANES)))
            (q,) = _norm_linear(x2, ln_mix[i], b_w_q[j], head_gain, sections=(("head_rms", 0),), aux_section=0,
                                sec_width=width, out_dtypes=(BF16,), scale=math.log2(math.e) / math.sqrt(LANES),
                                name=f"attn_q_{i}")
            lam_init = 0.8 - 0.6 * math.exp(-0.3 * i)
            o = _diff_attention(q.reshape(b, s, width), kv, b_subln[j], lam_q1[j], lam_k1[j], lam_q2[j],
                                lam_k2[j], lam_init=lam_init, name=f"attn_{i}")
            w_out = b_w_out[j]
        x2 = _mixer_out_mlp(x2, o.reshape(t, width), w_out, ln_mlp[i], mlp_up[i], mlp_down[i], name=f"mlp_{i}")
        x2 = _ple(x2, ln_ple[i], p[i].reshape(t, -1), ple_gate[i], ple_proj[i], name=f"ple_{i}")
        if i == n_a - 1:
            width = w_k.shape[1]
            head_gain = jnp.tile(k_norm.reshape(1, -1), (1, width // (2 * LANES)))
            (kv,) = _norm_linear(x2, kv_norm, jnp.concatenate([w_k, w_v], axis=1), head_gain,
                                 sections=(("head_rms", 0), ("plain", 0)), aux_section=0, sec_width=width,
                                 out_dtypes=(BF16,), name="shared_kv")
            kv = kv.reshape(b, s, 2 * width)
    return x2.reshape(b, s, d)
```

```python
import functools
import math

import numpy as np
import jax
import jax.numpy as jnp
from jax import lax
from jax.experimental import pallas as pl
from jax.experimental.pallas import tpu as pltpu

EPS = 1e-6
LANES = 128
HGRN_CHUNK = 128
HGRN_UNROLL = 4
ROW_CHUNK = 256
SAFE_LOG2_SHIFT = 60.0
F32 = jnp.float32
BF16 = jnp.bfloat16
VMEM_LIMIT_BYTES = 56 * 1024 * 1024

_NT = (((1,), (1,)), ((), ()))
_TN = (((0,), (0,)), ((), ()))


def _params(*sem):
    return pltpu.CompilerParams(dimension_semantics=sem, vmem_limit_bytes=VMEM_LIMIT_BYTES)


def _rms_bf16(x, gain):
    ms = jnp.mean(x * x, axis=-1, keepdims=True)
    return (x * lax.rsqrt(ms + EPS) * gain).astype(BF16)


def _silu(x):
    return x * jax.nn.sigmoid(x)


def _group_rms(y, gain, scale):
    outs = []
    for c in range(y.shape[1] // LANES):
        sl = slice(c * LANES, (c + 1) * LANES)
        yc = y[:, sl]
        ms = jnp.mean(yc * yc, axis=-1, keepdims=True)
        outs.append(yc * lax.rsqrt(ms + EPS) * (gain[:, sl] * scale))
    return jnp.concatenate(outs, axis=1)


def _row_chunks(tm):
    rc = min(ROW_CHUNK, tm)
    return [slice(r * rc, (r + 1) * rc) for r in range(tm // rc)]


def _norm_linear_kernel(x_ref, g_ref, w_ref, aux_ref, *rest, sections, sec_blocks, lb_rows, scale):
    o_refs, h_scr = rest[:-1], rest[-1]
    j = pl.program_id(1)

    def epilogue(name, acc):
        if name == "silu":
            return _silu(acc)
        if name == "plain":
            return acc
        if name == "head_rms":
            return _group_rms(acc, aux_ref[...], scale)
        if name == "log_forget":
            a = aux_ref[...]
            e = jnp.exp(a - jnp.max(a, axis=0, keepdims=True))
            lb = jnp.sum(e[:lb_rows], axis=0, keepdims=True) / jnp.sum(e, axis=0, keepdims=True)
            return jnp.log2(lb + (1.0 - lb) * jax.nn.sigmoid(acc))
        raise ValueError(name)

    def run(first, name, o_ref):
        for rows in _row_chunks(x_ref.shape[0]):
            if first:
                h = _rms_bf16(x_ref[rows, :], g_ref[...])
                h_scr[rows, :] = h
            else:
                h = h_scr[rows, :]
            acc = jnp.dot(h, w_ref[...], preferred_element_type=F32)
            o_ref[rows, :] = epilogue(name, acc).astype(o_ref.dtype)

    for s, (name, out_idx) in enumerate(sections):
        lo, hi = s * sec_blocks, (s + 1) * sec_blocks
        if s == 0:
            pl.when(j == 0)(functools.partial(run, True, name, o_refs[out_idx]))
            lo = 1
        if lo < hi:
            pl.when(jnp.logical_and(j >= lo, j < hi))(functools.partial(run, False, name, o_refs[out_idx]))


def _norm_linear(x2, gain, w, aux, *, sections, aux_section, sec_width, out_dtypes, name, lb_rows=1, scale=1.0,
                 tm=1024, tn=1024):
    t, k = x2.shape
    n = w.shape[1]
    tm, tn = min(tm, t), min(tn, sec_width)
    sb = sec_width // tn
    first_step = [None] * len(out_dtypes)
    n_blocks = [0] * len(out_dtypes)
    for s, (_, o) in enumerate(sections):
        if first_step[o] is None:
            first_step[o] = s * sb
        assert s * sb == first_step[o] + n_blocks[o], "an output's sections must be consecutive"
        n_blocks[o] += sb

    def out_spec(o):
        return pl.BlockSpec((tm, tn), lambda i, j: (i, jnp.clip(j - first_step[o], 0, n_blocks[o] - 1)))

    kern = functools.partial(_norm_linear_kernel, sections=tuple(sections), sec_blocks=sb, lb_rows=lb_rows, scale=scale)
    return pl.pallas_call(
        kern,
        grid=(t // tm, n // tn),
        in_specs=[
            pl.BlockSpec((tm, k), lambda i, j: (i, 0)),
            pl.BlockSpec((1, k), lambda i, j: (0, 0)),
            pl.BlockSpec((k, tn), lambda i, j: (0, j)),
            pl.BlockSpec((aux.shape[0], tn), lambda i, j: (0, jnp.clip(j - aux_section * sb, 0, sb - 1))),
        ],
        out_specs=[out_spec(o) for o in range(len(out_dtypes))],
        out_shape=[jax.ShapeDtypeStruct((t, n_blocks[o] * tn), out_dtypes[o]) for o in range(len(out_dtypes))],
        scratch_shapes=[pltpu.VMEM((tm, k), BF16)],
        compiler_params=_params("parallel", "arbitrary"),
        name=name,
    )(x2, gain.reshape(1, k), w.astype(BF16), aux)


def _mixer_out_mlp_kernel(x_ref, a_ref, wo_ref, g_ref, up_ref, down_ref, o_ref, h_scr):
    def mixer_out():
        for rows in _row_chunks(x_ref.shape[0]):
            x1 = x_ref[rows, :] + jnp.dot(a_ref[rows, :], wo_ref[...], preferred_element_type=F32)
            o_ref[rows, :] = x1
            h_scr[rows, :] = _rms_bf16(x1, g_ref[...])

    def mlp_tile():
        for rows in _row_chunks(x_ref.shape[0]):
            u = jnp.dot(h_scr[rows, :], up_ref[...], preferred_element_type=F32)
            a = jnp.square(jnp.maximum(u, 0.0)).astype(BF16)
            o_ref[rows, :] += jnp.dot(a, down_ref[...], preferred_element_type=F32)

    j = pl.program_id(1)
    pl.when(j == 0)(mixer_out)
    pl.when(j > 0)(mlp_tile)


def _mixer_out_mlp(x2, a, w_out, gain, up, down, *, name, tm=512, tf=1024):
    t, d = x2.shape
    ka = a.shape[1]
    f = up.shape[1]
    tm, tf = min(tm, t), min(tf, f)
    return pl.pallas_call(
        _mixer_out_mlp_kernel,
        grid=(t // tm, 1 + f // tf),
        in_specs=[
            pl.BlockSpec((tm, d), lambda i, j: (i, 0)),
            pl.BlockSpec((tm, ka), lambda i, j: (i, 0)),
            pl.BlockSpec((ka, d), lambda i, j: (0, 0), pipeline_mode=pl.Buffered(1)),
            pl.BlockSpec((1, d), lambda i, j: (0, 0)),
            pl.BlockSpec((d, tf), lambda i, j: (0, jnp.maximum(j - 1, 0))),
            pl.BlockSpec((tf, d), lambda i, j: (jnp.maximum(j - 1, 0), 0)),
        ],
        out_specs=pl.BlockSpec((tm, d), lambda i, j: (i, 0)),
        out_shape=jax.ShapeDtypeStruct((t, d), F32),
        scratch_shapes=[pltpu.VMEM((tm, d), BF16)],
        compiler_params=_params("parallel", "arbitrary"),
        name=name,
    )(x2, a, w_out.astype(BF16), gain.reshape(1, d), up.astype(BF16), down.astype(BF16))


def _ple_kernel(x_ref, g_ref, p_ref, gw_ref, pw_ref, o_ref, h_scr, *, tn):
    j = pl.program_id(1)
    cols = pl.ds(pl.multiple_of(j * tn, tn), tn)

    def run(first):
        for rows in _row_chunks(x_ref.shape[0]):
            if first:
                h = _rms_bf16(x_ref[rows, :], g_ref[...])
                h_scr[rows, :] = h
            else:
                h = h_scr[rows, :]
            gate = jax.nn.sigmoid(jnp.dot(h, gw_ref[...], preferred_element_type=F32))
            emb = jnp.dot(p_ref[rows, :].astype(BF16), pw_ref[...], preferred_element_type=F32)
            o_ref[rows, :] = x_ref[rows, cols] + gate * emb

    pl.when(j == 0)(lambda: run(True))
    pl.when(j > 0)(lambda: run(False))


def _ple(x2, gain, p2, gate_w, proj_w, *, name, tm=1024, tn=1024):
    t, d = x2.shape
    pd = p2.shape[1]
    tm, tn = min(tm, t), min(tn, d)
    return pl.pallas_call(
        functools.partial(_ple_kernel, tn=tn),
        grid=(t // tm, d // tn),
        in_specs=[
            pl.BlockSpec((tm, d), lambda i, j: (i, 0)),
            pl.BlockSpec((1, d), lambda i, j: (0, 0)),
            pl.BlockSpec((tm, pd), lambda i, j: (i, 0)),
            pl.BlockSpec((d, tn), lambda i, j: (0, j)),
            pl.BlockSpec((pd, tn), lambda i, j: (0, j)),
        ],
        out_specs=pl.BlockSpec((tm, tn), lambda i, j: (i, j)),
        out_shape=jax.ShapeDtypeStruct((t, d), F32),
        scratch_shapes=[pltpu.VMEM((tm, d), BF16)],
        compiler_params=_params("parallel", "arbitrary"),
        name=name,
    )(x2, gain.reshape(1, d), p2, gate_w.astype(BF16), proj_w.astype(BF16))


def _hgrn_levels(c):
    return [c >> (l + 1) for l in range(int(math.log2(c)))]


def _hgrn_constants(c):
    idx = np.arange(c)
    x = idx[:, None] ^ idx[None, :]
    masks = [((x >= m) & (x < 2 * m) & ((idx[:, None] & m) != 0)).astype(np.float32) for m in _hgrn_levels(c)]
    return np.tril(np.ones((c, c), np.float32)), np.stack(masks, 0)


def _level_log_decay(g_cum, g, m):
    c, w = g_cum.shape
    if m == 1:
        odd = (lax.broadcasted_iota(jnp.int32, (c, w), 0) & 1) != 0
        return jnp.where(odd, g, 0.0)
    if m < 8:
        g3 = g_cum.reshape(c // 8, 8, w)
        sub = lax.broadcasted_iota(jnp.int32, g3.shape, 1)
        if m == 4:
            mid = g3[:, 3:4, :]
        else:
            mid = jnp.where(sub < 4, g3[:, 1:2, :], g3[:, 5:6, :])
        return (-jnp.abs(g3 - mid)).reshape(c, w)
    g3 = g_cum.reshape(c // (2 * m), 2 * m, w)
    mid = g3[:, m - 1:m, :]
    return jnp.concatenate([mid - g3[:, :m, :], g3[:, m:, :] - mid], axis=1).reshape(c, w)


def _pair_rows(lower, upper, m):
    c, w = lower.shape
    if m < 8:
        return jnp.where((lax.broadcasted_iota(jnp.int32, (c, w), 0) & m) != 0, lower, upper)
    lo3 = lower.reshape(c // (2 * m), 2 * m, w)
    up3 = upper.reshape(c // (2 * m), 2 * m, w)
    return jnp.concatenate([up3[:, :m, :], lo3[:, m:, :]], axis=1).reshape(c, w)


def _hgrn_kernel(q_ref, f_ref, v_ref, gate_ref, tri_ref, msk_ref, on_ref, o_ref, st_scr, *, c, hb, nchunks):
    @pl.when(pl.program_id(2) == 0)
    def _():
        st_scr[...] = jnp.zeros_like(st_scr)

    levels = _hgrn_levels(c)

    def chunk(ci, carry):
        rows = pl.ds(pl.multiple_of(ci * c, c), c)
        g_all = f_ref[0, rows, :]
        g_hi = g_all.astype(BF16)
        g_lo = (g_all - g_hi.astype(F32)).astype(BF16)
        tri = tri_ref[...]
        g_cum_all = (jnp.dot(tri, g_hi, preferred_element_type=F32)
                     + jnp.dot(tri, g_lo, preferred_element_type=F32))
        k_all = 1.0 - jnp.exp2(g_all)
        g_last = g_cum_all[c - 1:c, :]
        q_dec_all = jnp.exp2(g_cum_all)
        k_dec_all = jnp.exp2(g_last - g_cum_all)
        st_dec_all = jnp.exp2(g_last)
        lvl_dec_all = [jnp.exp2(_level_log_decay(g_cum_all, g_all, m)) for m in levels]
        for h in range(hb):
            hl = slice(h * LANES, (h + 1) * LANES)
            qs = q_ref[0, rows, hl].astype(F32)
            vb = v_ref[0, rows, hl]
            v = vb.astype(F32)
            k = k_all[:, hl]
            a = jnp.zeros((c, c), F32)
            for l, m in enumerate(levels):
                x = (_pair_rows(qs, k, m) * lvl_dec_all[l][:, hl]).astype(BF16)
                a = a + msk_ref[l] * lax.dot_general(x, x, _NT, preferred_element_type=F32)
            o = jnp.dot(a.astype(BF16), vb, preferred_element_type=F32)
            o = o + jnp.sum(qs * k, axis=-1, keepdims=True) * v
            st = st_scr[h]
            qg = (qs * q_dec_all[:, hl]).astype(BF16)
            o = o + lax.dot_general(qg, st.astype(BF16), _NT, preferred_element_type=F32)
            kt = (k * k_dec_all[:, hl]).astype(BF16)
            st_scr[h] = st * st_dec_all[:, hl] + lax.dot_general(vb, kt, _TN, preferred_element_type=F32)
            ms = jnp.mean(o * o, axis=-1, keepdims=True)
            y = o * lax.rsqrt(ms + EPS) * on_ref[...] * gate_ref[0, rows, hl].astype(F32)
            o_ref[0, rows, hl] = y.astype(o_ref.dtype)
        return carry

    unroll = HGRN_UNROLL if nchunks % HGRN_UNROLL == 0 else 1

    def trip(u, carry):
        for r in range(unroll):
            chunk(u * unroll + r, carry)
        return carry

    lax.fori_loop(0, nchunks // unroll, trip, 0)


def _hgrn(qig, log_f, onorm, *, name, rows=512, hb=4):
    b, s, width = log_f.shape
    c = HGRN_CHUNK
    rows = min(rows, s)
    hw = hb * LANES
    nsb = width // hw
    tri, masks = _hgrn_constants(c)
    nl = masks.shape[0]

    def sec(k):
        return pl.BlockSpec((1, rows, hw), lambda bi, hg, r: (bi, r, k * nsb + hg))

    kern = functools.partial(_hgrn_kernel, c=c, hb=hb, nchunks=rows // c)
    return pl.pallas_call(
        kern,
        grid=(b, nsb, s // rows),
        in_specs=[
            sec(0), sec(0), sec(1), sec(2),
            pl.BlockSpec((c, c), lambda bi, hg, r: (0, 0)),
            pl.BlockSpec((nl, c, c), lambda bi, hg, r: (0, 0, 0)),
            pl.BlockSpec((1, LANES), lambda bi, hg, r: (0, 0)),
        ],
        out_specs=pl.BlockSpec((1, rows, hw), lambda bi, hg, r: (bi, r, hg)),
        out_shape=jax.ShapeDtypeStruct((b, s, width), BF16),
        scratch_shapes=[pltpu.VMEM((hb, LANES, LANES), F32)],
        compiler_params=_params("parallel", "parallel", "arbitrary"),
        name=name,
    )(qig, log_f, qig, qig, jnp.asarray(tri, BF16), jnp.asarray(masks, F32), onorm.reshape(1, LANES))


def _attn_kernel(q_ref, k_ref, v_ref, qg_ref, sub_ref, lq1_ref, lk1_ref, lq2_ref, lk2_ref, o_ref,
                 m_scr, l_scr, acc_scr, shift_scr, flag_scr, *, tq, lam_init):
    qi = pl.program_id(2)
    hw = acc_scr.shape[-1]
    maps = [slice(c * LANES, (c + 1) * LANES) for c in range(2)]

    @pl.when(qi == 0)
    def _():
        def key_block(i, carry):
            kb = k_ref[0, pl.ds(pl.multiple_of(i * tq, tq), tq), :].astype(F32)
            sq = [jnp.sum(kb[:, sl] * kb[:, sl], axis=-1, keepdims=True) for sl in maps]
            return tuple(jnp.maximum(carry[c], jnp.max(sq[c], axis=0, keepdims=True)) for c in range(2))

        zero = jnp.zeros((1, 1), F32)
        k_sq = lax.fori_loop(0, k_ref.shape[1] // tq, key_block, (zero, zero))
        bounds = []
        for c, sl in enumerate(maps):
            q_max = math.sqrt(LANES) * jnp.max(jnp.abs(qg_ref[:, sl]), axis=-1, keepdims=True)
            bounds.append(q_max * jnp.sqrt(k_sq[c]))
            shift_scr[c] = jnp.broadcast_to(bounds[c], (1, LANES))
        flag_scr[0] = (jnp.max(jnp.maximum(bounds[0], bounds[1])) < SAFE_LOG2_SHIFT).astype(jnp.int32)

    bounded = flag_scr[0] == 1

    def causal(shape, row0):
        row = lax.broadcasted_iota(jnp.int32, shape, 0) + row0
        return lax.broadcasted_iota(jnp.int32, shape, 1) <= row

    def block(j, masked, running_max, first=False):
        base = pl.multiple_of(j * tq, tq)
        for qrows in _row_chunks(tq):
            nk = qrows.stop if masked else tq
            keys = pl.ds(base, nk)
            vb = v_ref[0, keys, :]
            for c, sl in enumerate(maps):
                s = lax.dot_general(q_ref[0, qrows, sl], k_ref[0, keys, sl], _NT,
                                    preferred_element_type=F32)
                if running_max:
                    if masked:
                        s = jnp.where(causal(s.shape, qrows.start), s, -jnp.inf)
                    m_prev = m_scr[c, qrows, :]
                    m_new = jnp.maximum(m_prev, jnp.max(s, axis=-1, keepdims=True))
                    alpha = jnp.exp2(m_prev - m_new)
                    p = jnp.exp2(s - jnp.tile(m_new, (1, nk // LANES)))
                    l_scr[c, qrows, :] = alpha * l_scr[c, qrows, :] + jnp.sum(p, axis=-1, keepdims=True)
                    acc_scr[c, qrows, :] = (jnp.tile(alpha, (1, hw // LANES)) * acc_scr[c, qrows, :]
                                            + jnp.dot(p.astype(BF16), vb, preferred_element_type=F32))
                    m_scr[c, qrows, :] = m_new
                else:
                    p = jnp.exp2(s - jnp.tile(shift_scr[c], (1, nk // LANES)))
                    if masked:
                        p = jnp.where(causal(p.shape, qrows.start), p, 0.0)
                    lane_sums = p[:, 0:LANES]
                    for g in range(1, nk // LANES):
                        lane_sums = lane_sums + p[:, g * LANES:(g + 1) * LANES]
                    pv = jnp.dot(p.astype(BF16), vb, preferred_element_type=F32)
                    if first:
                        l_scr[c, qrows, :] = lane_sums
                        acc_scr[c, qrows, :] = pv
                    else:
                        l_scr[c, qrows, :] += lane_sums
                        acc_scr[c, qrows, :] += pv

    def full_blocks(running_max):
        def body(j, carry):
            block(j, False, running_max)
            return carry

        lax.fori_loop(0, qi, body, 0)

    @pl.when(bounded)
    def _():
        block(qi, True, False, first=True)
        full_blocks(False)
        for c in range(2):
            l_scr[c] = jnp.broadcast_to(jnp.sum(l_scr[c], axis=-1, keepdims=True), l_scr.shape[1:])

    @pl.when(jnp.logical_not(bounded))
    def _():
        m_scr[...] = jnp.full_like(m_scr, -1e30)
        l_scr[...] = jnp.zeros_like(l_scr)
        acc_scr[...] = jnp.zeros_like(acc_scr)
        full_blocks(True)
        block(qi, True, True)

    lam = (jnp.exp(jnp.sum(lq1_ref[...] * lk1_ref[...], axis=-1, keepdims=True))
           - jnp.exp(jnp.sum(lq2_ref[...] * lk2_ref[...], axis=-1, keepdims=True))
           + lam_init)
    inv1 = jnp.tile(1.0 / l_scr[0], (1, hw // LANES))
    inv2 = jnp.tile(1.0 / l_scr[1], (1, hw // LANES))
    o = acc_scr[0] * inv1 - lam * (acc_scr[1] * inv2)
    ms = jnp.mean(o * o, axis=-1, keepdims=True)
    o_ref[0] = (o * lax.rsqrt(ms + EPS) * (sub_ref[...] * (1.0 - lam_init))).astype(o_ref.dtype)


def _diff_attention(q, kv, q_gain, subln, lq1, lk1, lq2, lk2, *, lam_init, name, tq=1024):
    b, s, width = q.shape
    hw = 2 * LANES
    nh = width // hw
    tq = min(tq, s)
    vec = pl.BlockSpec((1, LANES), lambda bi, h, i: (0, 0))
    head_vec = pl.BlockSpec((1, hw), lambda bi, h, i: (0, 0))
    return pl.pallas_call(
        functools.partial(_attn_kernel, tq=tq, lam_init=lam_init),
        grid=(b, nh, s // tq),
        in_specs=[
            pl.BlockSpec((1, tq, hw), lambda bi, h, i: (bi, i, h)),
            pl.BlockSpec((1, s, hw), lambda bi, h, i: (bi, 0, h)),
            pl.BlockSpec((1, s, hw), lambda bi, h, i: (bi, 0, nh + h)),
            head_vec, head_vec,
            vec, vec, vec, vec,
        ],
        out_specs=pl.BlockSpec((1, tq, hw), lambda bi, h, i: (bi, i, h)),
        out_shape=jax.ShapeDtypeStruct((b, s, width), BF16),
        scratch_shapes=[
            pltpu.VMEM((2, tq, LANES), F32),
            pltpu.VMEM((2, tq, LANES), F32),
            pltpu.VMEM((2, tq, hw), F32),
            pltpu.VMEM((2, 1, LANES), F32),
            pltpu.SMEM((1,), jnp.int32),
        ],
        compiler_params=_params("parallel", "parallel", "arbitrary"),
        name=name,
    )(q, kv, kv, q_gain.reshape(1, hw), subln.reshape(1, hw), lq1.reshape(1, LANES), lk1.reshape(1, LANES),
      lq2.reshape(1, LANES), lk2.reshape(1, LANES))


def kernel(x, p, ln_mix, ln_mlp, ln_ple, a_w_in, a_lb, a_onorm, a_w_out, kv_norm, w_k, w_v, k_norm, b_w_q, q_norm, lam_q1, lam_k1, lam_q2, lam_k2, b_subln, b_w_out, mlp_up, mlp_down, ple_proj, ple_gate):
    b, s, d = x.shape
    depth = ln_mix.shape[0]
    n_a = a_w_in.shape[0]
    t = b * s
    x2 = x.reshape(t, d)
    kv = None
    for i in range(depth):
        if i < n_a:
            j = i
            width = a_w_out.shape[1]
            wq, wf, wi, wg = jnp.split(a_w_in[j], 4, axis=1)
            qig, log_f = _norm_linear(
                x2, ln_mix[i], jnp.concatenate([wq, wi, wg, wf], axis=1), a_lb,
                sections=(("silu", 0), ("plain", 0), ("silu", 0), ("log_forget", 1)), aux_section=3,
                sec_width=width, out_dtypes=(BF16, F32), lb_rows=i + 1, name=f"hgrn_in_{i}")
            o = _hgrn(qig.reshape(b, s, 3 * width), log_f.reshape(b, s, width), a_onorm[j], name=f"hgrn_{i}")
            w_out = a_w_out[j]
        else:
            j = i - n_a
            width = b_w_q.shape[2]
            q_scale = math.log2(math.e) / math.sqrt(LANES)
            head_gain = jnp.tile(q_norm[j].reshape(1, -1), (1, width // (2 * LANES)))
            (q,) = _norm_linear(x2, ln_mix[i], b_w_q[j], head_gain, sections=(("head_rms", 0),), aux_section=0,
                                sec_width=width, out_dtypes=(BF16,), scale=q_scale, name=f"attn_q_{i}")
            lam_init = 0.8 - 0.6 * math.exp(-0.3 * i)
            o = _diff_attention(q.reshape(b, s, width), kv, q_norm[j] * q_scale, b_subln[j], lam_q1[j], lam_k1[j],
                                lam_q2[j], lam_k2[j], lam_init=lam_init, name=f"attn_{i}")
            w_out = b_w_out[j]
        x2 = _mixer_out_mlp(x2, o.reshape(t, width), w_out, ln_mlp[i], mlp_up[i], mlp_down[i], name=f"mlp_{i}")
        x2 = _ple(x2, ln_ple[i], p[i].reshape(t, -1), ple_gate[i], ple_proj[i], name=f"ple_{i}")
        if i == n_a - 1:
            width = w_k.shape[1]
            head_gain = jnp.tile(k_norm.reshape(1, -1), (1, width // (2 * LANES)))
            (kv,) = _norm_linear(x2, kv_norm, jnp.concatenate([w_k, w_v], axis=1), head_gain,
                                 sections=(("head_rms", 0), ("plain", 0)), aux_section=0, sec_width=width,
                                 out_dtypes=(BF16,), name="shared_kv")
            kv = kv.reshape(b, s, 2 * width)
    return x2.reshape(b, s, d)
```

```python
import functools
import math

import numpy as np
import jax
import jax.numpy as jnp
from jax import lax
from jax.experimental import pallas as pl
from jax.experimental.pallas import tpu as pltpu

EPS = 1e-6
LANES = 128
HGRN_CHUNK = 128
HGRN_UNROLL = 4
CAST_BLOCK_ELEMS = 2 * 1024 * 1024
ROW_CHUNK = 256
SAFE_LOG2_SHIFT = 60.0
F32 = jnp.float32
BF16 = jnp.bfloat16
VMEM_LIMIT_BYTES = 56 * 1024 * 1024

_NT = (((1,), (1,)), ((), ()))
_TN = (((0,), (0,)), ((), ()))


def _params(*sem):
    return pltpu.CompilerParams(dimension_semantics=sem, vmem_limit_bytes=VMEM_LIMIT_BYTES)


def _rms_bf16(x, gain):
    ms = jnp.mean(x * x, axis=-1, keepdims=True)
    return (x * lax.rsqrt(ms + EPS) * gain).astype(BF16)


def _silu(x):
    return x * jax.nn.sigmoid(x)


def _group_rms(y, gain, scale):
    outs = []
    for c in range(y.shape[1] // LANES):
        sl = slice(c * LANES, (c + 1) * LANES)
        yc = y[:, sl]
        ms = jnp.mean(yc * yc, axis=-1, keepdims=True)
        outs.append(yc * lax.rsqrt(ms + EPS) * (gain[:, sl] * scale))
    return jnp.concatenate(outs, axis=1)


def _row_chunks(tm):
    rc = min(ROW_CHUNK, tm)
    return [slice(r * rc, (r + 1) * rc) for r in range(tm // rc)]


def _norm_linear_kernel(x_ref, g_ref, aux_ref, *rest, sections, n_weights, sec_blocks, lb_rows, scale):
    w_refs, o_refs, h_scr = rest[:n_weights], rest[n_weights:-1], rest[-1]
    j = pl.program_id(1)

    def epilogue(name, acc):
        if name == "silu":
            return _silu(acc)
        if name == "plain":
            return acc
        if name == "head_rms":
            return _group_rms(acc, aux_ref[...], scale)
        if name == "log_forget":
            a = aux_ref[...]
            e = jnp.exp(a - jnp.max(a, axis=0, keepdims=True))
            lb = jnp.sum(e[:lb_rows], axis=0, keepdims=True) / jnp.sum(e, axis=0, keepdims=True)
            return jnp.log2(lb + (1.0 - lb) * jax.nn.sigmoid(acc))
        raise ValueError(name)

    def run(first, name, o_ref, w_ref):
        for rows in _row_chunks(x_ref.shape[0]):
            if first:
                h = _rms_bf16(x_ref[rows, :], g_ref[...])
                h_scr[rows, :] = h
            else:
                h = h_scr[rows, :]
            acc = jnp.dot(h, w_ref[...], preferred_element_type=F32)
            o_ref[rows, :] = epilogue(name, acc).astype(o_ref.dtype)

    for s, (name, out_idx, w_idx, _) in enumerate(sections):
        lo, hi = s * sec_blocks, (s + 1) * sec_blocks
        if s == 0:
            pl.when(j == 0)(functools.partial(run, True, name, o_refs[out_idx], w_refs[w_idx]))
            lo = 1
        if lo < hi:
            pl.when(jnp.logical_and(j >= lo, j < hi))(
                functools.partial(run, False, name, o_refs[out_idx], w_refs[w_idx]))


def _step_table(j, table):
    out = table[0]
    for jj in range(1, len(table)):
        if table[jj] != table[jj - 1]:
            out = jnp.where(j >= jj, table[jj], out)
    return out


def _norm_linear(x2, gain, ws, aux, *, sections, aux_section, sec_width, out_dtypes, name, lb_rows=1, scale=1.0,
                 tm=1024, tn=1024):
    t, k = x2.shape
    tm, tn = min(tm, t), min(tn, sec_width)
    sb = sec_width // tn
    n_steps = len(sections) * sb
    first_step = [None] * len(out_dtypes)
    n_blocks = [0] * len(out_dtypes)
    for s, (_, o, _, _) in enumerate(sections):
        if first_step[o] is None:
            first_step[o] = s * sb
        assert s * sb == first_step[o] + n_blocks[o], "an output's sections must be consecutive"
        n_blocks[o] += sb

    def out_spec(o):
        return pl.BlockSpec((tm, tn), lambda i, j: (i, jnp.clip(j - first_step[o], 0, n_blocks[o] - 1)))

    def weight_spec(w_idx):
        cols = [None] * n_steps
        for s, (_, _, wi, w_sec) in enumerate(sections):
            if wi == w_idx:
                for r in range(sb):
                    cols[s * sb + r] = w_sec * sb + r
        nxt = next(c for c in cols if c is not None)
        for jj in range(n_steps):
            if cols[jj] is None:
                cols[jj] = nxt
            nxt = cols[jj]
        return pl.BlockSpec((k, tn), lambda i, j: (0, _step_table(j, cols)))

    kern = functools.partial(_norm_linear_kernel, sections=tuple(sections), n_weights=len(ws), sec_blocks=sb,
                             lb_rows=lb_rows, scale=scale)
    return pl.pallas_call(
        kern,
        grid=(t // tm, n_steps),
        in_specs=[
            pl.BlockSpec((tm, k), lambda i, j: (i, 0)),
            pl.BlockSpec((1, k), lambda i, j: (0, 0)),
            pl.BlockSpec((aux.shape[0], tn), lambda i, j: (0, jnp.clip(j - aux_section * sb, 0, sb - 1))),
        ] + [weight_spec(wi) for wi in range(len(ws))],
        out_specs=[out_spec(o) for o in range(len(out_dtypes))],
        out_shape=[jax.ShapeDtypeStruct((t, n_blocks[o] * tn), out_dtypes[o]) for o in range(len(out_dtypes))],
        scratch_shapes=[pltpu.VMEM((tm, k), BF16)],
        compiler_params=_params("parallel", "arbitrary"),
        name=name,
    )(x2, gain.reshape(1, k), aux, *ws)


def _cast_kernel(w_ref, o_ref):
    o_ref[...] = w_ref[...].astype(o_ref.dtype)


def _cast_bf16(w, layer, *, name):
    _, r, c = w.shape
    tc = min(c, 2048)
    tr = min(r, max(8, CAST_BLOCK_ELEMS // tc))
    return pl.pallas_call(
        _cast_kernel,
        grid=(r // tr, c // tc),
        in_specs=[pl.BlockSpec((None, tr, tc), lambda i, j: (layer, i, j))],
        out_specs=pl.BlockSpec((tr, tc), lambda i, j: (i, j)),
        out_shape=jax.ShapeDtypeStruct((r, c), BF16),
        compiler_params=_params("parallel", "parallel"),
        name=name,
    )(w)


def _mixer_out_mlp_kernel(x_ref, a_ref, wo_ref, g_ref, up_ref, down_ref, o_ref, h_scr):
    def mixer_out():
        for rows in _row_chunks(x_ref.shape[0]):
            x1 = x_ref[rows, :] + jnp.dot(a_ref[rows, :], wo_ref[...], preferred_element_type=F32)
            o_ref[rows, :] = x1
            h_scr[rows, :] = _rms_bf16(x1, g_ref[...])

    def mlp_tile():
        for rows in _row_chunks(x_ref.shape[0]):
            u = jnp.dot(h_scr[rows, :], up_ref[...], preferred_element_type=F32)
            a = jnp.square(jnp.maximum(u, 0.0)).astype(BF16)
            o_ref[rows, :] += jnp.dot(a, down_ref[...], preferred_element_type=F32)

    j = pl.program_id(1)
    pl.when(j == 0)(mixer_out)
    pl.when(j > 0)(mlp_tile)


def _mixer_out_mlp(x2, a, w_out, gain, up, down, *, name, tm=512, tf=1024):
    t, d = x2.shape
    ka = a.shape[1]
    f = up.shape[1]
    tm, tf = min(tm, t), min(tf, f)
    return pl.pallas_call(
        _mixer_out_mlp_kernel,
        grid=(t // tm, 1 + f // tf),
        in_specs=[
            pl.BlockSpec((tm, d), lambda i, j: (i, 0)),
            pl.BlockSpec((tm, ka), lambda i, j: (i, 0)),
            pl.BlockSpec((ka, d), lambda i, j: (0, 0), pipeline_mode=pl.Buffered(1)),
            pl.BlockSpec((1, d), lambda i, j: (0, 0)),
            pl.BlockSpec((d, tf), lambda i, j: (0, jnp.maximum(j - 1, 0))),
            pl.BlockSpec((tf, d), lambda i, j: (jnp.maximum(j - 1, 0), 0)),
        ],
        out_specs=pl.BlockSpec((tm, d), lambda i, j: (i, 0)),
        out_shape=jax.ShapeDtypeStruct((t, d), F32),
        scratch_shapes=[pltpu.VMEM((tm, d), BF16)],
        compiler_params=_params("parallel", "arbitrary"),
        name=name,
    )(x2, a, w_out, gain.reshape(1, d), up, down)


def _ple_kernel(x_ref, g_ref, p_ref, gw_ref, pw_ref, o_ref, h_scr, *, tn):
    j = pl.program_id(1)
    cols = pl.ds(pl.multiple_of(j * tn, tn), tn)

    def run(first):
        for rows in _row_chunks(x_ref.shape[0]):
            if first:
                h = _rms_bf16(x_ref[rows, :], g_ref[...])
                h_scr[rows, :] = h
            else:
                h = h_scr[rows, :]
            gate = jax.nn.sigmoid(jnp.dot(h, gw_ref[...], preferred_element_type=F32))
            emb = jnp.dot(p_ref[rows, :].astype(BF16), pw_ref[...], preferred_element_type=F32)
            o_ref[rows, :] = x_ref[rows, cols] + gate * emb

    pl.when(j == 0)(lambda: run(True))
    pl.when(j > 0)(lambda: run(False))


def _ple(x2, gain, p3, layer, gate_w, proj_w, *, name, tm=1024, tn=1024):
    t, d = x2.shape
    pd = p3.shape[2]
    tm, tn = min(tm, t), min(tn, d)
    return pl.pallas_call(
        functools.partial(_ple_kernel, tn=tn),
        grid=(t // tm, d // tn),
        in_specs=[
            pl.BlockSpec((tm, d), lambda i, j: (i, 0)),
            pl.BlockSpec((1, d), lambda i, j: (0, 0)),
            pl.BlockSpec((None, tm, pd), lambda i, j: (layer, i, 0)),
            pl.BlockSpec((d, tn), lambda i, j: (0, j)),
            pl.BlockSpec((pd, tn), lambda i, j: (0, j)),
        ],
        out_specs=pl.BlockSpec((tm, tn), lambda i, j: (i, j)),
        out_shape=jax.ShapeDtypeStruct((t, d), F32),
        scratch_shapes=[pltpu.VMEM((tm, d), BF16)],
        compiler_params=_params("parallel", "arbitrary"),
        name=name,
    )(x2, gain.reshape(1, d), p3, gate_w, proj_w)


def _hgrn_levels(c):
    return [c >> (l + 1) for l in range(int(math.log2(c)))]


def _hgrn_constants(c):
    idx = np.arange(c)
    x = idx[:, None] ^ idx[None, :]
    masks = [((x >= m) & (x < 2 * m) & ((idx[:, None] & m) != 0)).astype(np.float32) for m in _hgrn_levels(c)]
    return np.tril(np.ones((c, c), np.float32)), np.stack(masks, 0)


def _level_log_decay(g_cum, g, m):
    c, w = g_cum.shape
    if m == 1:
        odd = (lax.broadcasted_iota(jnp.int32, (c, w), 0) & 1) != 0
        return jnp.where(odd, g, 0.0)
    if m < 8:
        g3 = g_cum.reshape(c // 8, 8, w)
        sub = lax.broadcasted_iota(jnp.int32, g3.shape, 1)
        if m == 4:
            mid = g3[:, 3:4, :]
        else:
            mid = jnp.where(sub < 4, g3[:, 1:2, :], g3[:, 5:6, :])
        return (-jnp.abs(g3 - mid)).reshape(c, w)
    g3 = g_cum.reshape(c // (2 * m), 2 * m, w)
    mid = g3[:, m - 1:m, :]
    return jnp.concatenate([mid - g3[:, :m, :], g3[:, m:, :] - mid], axis=1).reshape(c, w)


def _pair_rows(lower, upper, m):
    c, w = lower.shape
    if m < 8:
        return jnp.where((lax.broadcasted_iota(jnp.int32, (c, w), 0) & m) != 0, lower, upper)
    lo3 = lower.reshape(c // (2 * m), 2 * m, w)
    up3 = upper.reshape(c // (2 * m), 2 * m, w)
    return jnp.concatenate([up3[:, :m, :], lo3[:, m:, :]], axis=1).reshape(c, w)


def _hgrn_kernel(q_ref, f_ref, v_ref, gate_ref, tri_ref, msk_ref, on_ref, o_ref, st_scr, *, c, hb, nchunks):
    @pl.when(pl.program_id(2) == 0)
    def _():
        st_scr[...] = jnp.zeros_like(st_scr)

    levels = _hgrn_levels(c)

    def chunk(ci, carry):
        rows = pl.ds(pl.multiple_of(ci * c, c), c)
        g_all = f_ref[0, rows, :]
        g_hi = g_all.astype(BF16)
        g_lo = (g_all - g_hi.astype(F32)).astype(BF16)
        tri = tri_ref[...]
        g_cum_all = (jnp.dot(tri, g_hi, preferred_element_type=F32)
                     + jnp.dot(tri, g_lo, preferred_element_type=F32))
        k_all = 1.0 - jnp.exp2(g_all)
        g_last = g_cum_all[c - 1:c, :]
        q_dec_all = jnp.exp2(g_cum_all)
        k_dec_all = jnp.exp2(g_last - g_cum_all)
        st_dec_all = jnp.exp2(g_last)
        lvl_dec_all = [jnp.exp2(_level_log_decay(g_cum_all, g_all, m)) for m in levels]
        for h in range(hb):
            hl = slice(h * LANES, (h + 1) * LANES)
            qs = q_ref[0, rows, hl].astype(F32)
            vb = v_ref[0, rows, hl]
            v = vb.astype(F32)
            k = k_all[:, hl]
            a = jnp.zeros((c, c), F32)
            for l, m in enumerate(levels):
                x = (_pair_rows(qs, k, m) * lvl_dec_all[l][:, hl]).astype(BF16)
                a = a + msk_ref[l] * lax.dot_general(x, x, _NT, preferred_element_type=F32)
            o = jnp.dot(a.astype(BF16), vb, preferred_element_type=F32)
            o = o + jnp.sum(qs * k, axis=-1, keepdims=True) * v
            st = st_scr[h]
            qg = (qs * q_dec_all[:, hl]).astype(BF16)
            o = o + lax.dot_general(qg, st.astype(BF16), _NT, preferred_element_type=F32)
            kt = (k * k_dec_all[:, hl]).astype(BF16)
            st_scr[h] = st * st_dec_all[:, hl] + lax.dot_general(vb, kt, _TN, preferred_element_type=F32)
            ms = jnp.mean(o * o, axis=-1, keepdims=True)
            y = o * lax.rsqrt(ms + EPS) * on_ref[...] * gate_ref[0, rows, hl].astype(F32)
            o_ref[0, rows, hl] = y.astype(o_ref.dtype)
        return carry

    unroll = HGRN_UNROLL if nchunks % HGRN_UNROLL == 0 else 1

    def trip(u, carry):
        for r in range(unroll):
            chunk(u * unroll + r, carry)
        return carry

    lax.fori_loop(0, nchunks // unroll, trip, 0)


def _hgrn(qig, log_f, onorm, *, name, rows=512, hb=4):
    b, s, width = log_f.shape
    c = HGRN_CHUNK
    rows = min(rows, s)
    hw = hb * LANES
    nsb = width // hw
    tri, masks = _hgrn_constants(c)
    nl = masks.shape[0]

    def sec(k):
        return pl.BlockSpec((1, rows, hw), lambda bi, hg, r: (bi, r, k * nsb + hg))

    kern = functools.partial(_hgrn_kernel, c=c, hb=hb, nchunks=rows // c)
    return pl.pallas_call(
        kern,
        grid=(b, nsb, s // rows),
        in_specs=[
            sec(0), sec(0), sec(1), sec(2),
            pl.BlockSpec((c, c), lambda bi, hg, r: (0, 0)),
            pl.BlockSpec((nl, c, c), lambda bi, hg, r: (0, 0, 0)),
            pl.BlockSpec((1, LANES), lambda bi, hg, r: (0, 0)),
        ],
        out_specs=pl.BlockSpec((1, rows, hw), lambda bi, hg, r: (bi, r, hg)),
        out_shape=jax.ShapeDtypeStruct((b, s, width), BF16),
        scratch_shapes=[pltpu.VMEM((hb, LANES, LANES), F32)],
        compiler_params=_params("parallel", "parallel", "arbitrary"),
        name=name,
    )(qig, log_f, qig, qig, jnp.asarray(tri, BF16), jnp.asarray(masks, F32), onorm.reshape(1, LANES))


def _attn_kernel(q_ref, k_ref, v_ref, qg_ref, sub_ref, lq1_ref, lk1_ref, lq2_ref, lk2_ref, o_ref,
                 m_scr, l_scr, acc_scr, shift_scr, flag_scr, *, tq, lam_init):
    qi = pl.program_id(2)
    hw = acc_scr.shape[-1]
    maps = [slice(c * LANES, (c + 1) * LANES) for c in range(2)]

    @pl.when(qi == 0)
    def _():
        def key_block(i, carry):
            kb = k_ref[0, pl.ds(pl.multiple_of(i * tq, tq), tq), :].astype(F32)
            sq = [jnp.sum(kb[:, sl] * kb[:, sl], axis=-1, keepdims=True) for sl in maps]
            return tuple(jnp.maximum(carry[c], jnp.max(sq[c], axis=0, keepdims=True)) for c in range(2))

        zero = jnp.zeros((1, 1), F32)
        k_sq = lax.fori_loop(0, k_ref.shape[1] // tq, key_block, (zero, zero))
        bounds = []
        for c, sl in enumerate(maps):
            q_max = math.sqrt(LANES) * jnp.max(jnp.abs(qg_ref[:, sl]), axis=-1, keepdims=True)
            bounds.append(q_max * jnp.sqrt(k_sq[c]))
            shift_scr[c] = jnp.broadcast_to(bounds[c], (1, LANES))
        flag_scr[0] = (jnp.max(jnp.maximum(bounds[0], bounds[1])) < SAFE_LOG2_SHIFT).astype(jnp.int32)

    bounded = flag_scr[0] == 1

    def causal(shape, row0):
        row = lax.broadcasted_iota(jnp.int32, shape, 0) + row0
        return lax.broadcasted_iota(jnp.int32, shape, 1) <= row

    def block(j, masked, running_max, first=False):
        base = pl.multiple_of(j * tq, tq)
        for qrows in _row_chunks(tq):
            nk = qrows.stop if masked else tq
            keys = pl.ds(base, nk)
            vb = v_ref[0, keys, :]
            for c, sl in enumerate(maps):
                s = lax.dot_general(q_ref[0, qrows, sl], k_ref[0, keys, sl], _NT,
                                    preferred_element_type=F32)
                if running_max:
                    if masked:
                        s = jnp.where(causal(s.shape, qrows.start), s, -jnp.inf)
                    m_prev = m_scr[c, qrows, :]
                    m_new = jnp.maximum(m_prev, jnp.max(s, axis=-1, keepdims=True))
                    alpha = jnp.exp2(m_prev - m_new)
                    p = jnp.exp2(s - jnp.tile(m_new, (1, nk // LANES)))
                    l_scr[c, qrows, :] = alpha * l_scr[c, qrows, :] + jnp.sum(p, axis=-1, keepdims=True)
                    acc_scr[c, qrows, :] = (jnp.tile(alpha, (1, hw // LANES)) * acc_scr[c, qrows, :]
                                            + jnp.dot(p.astype(BF16), vb, preferred_element_type=F32))
                    m_scr[c, qrows, :] = m_new
                else:
                    p = jnp.exp2(s - jnp.tile(shift_scr[c], (1, nk // LANES)))
                    if masked:
                        p = jnp.where(causal(p.shape, qrows.start), p, 0.0)
                    lane_sums = p[:, 0:LANES]
                    for g in range(1, nk // LANES):
                        lane_sums = lane_sums + p[:, g * LANES:(g + 1) * LANES]
                    pv = jnp.dot(p.astype(BF16), vb, preferred_element_type=F32)
                    if first:
                        l_scr[c, qrows, :] = lane_sums
                        acc_scr[c, qrows, :] = pv
                    else:
                        l_scr[c, qrows, :] += lane_sums
                        acc_scr[c, qrows, :] += pv

    def full_blocks(running_max):
        def body(j, carry):
            block(j, False, running_max)
            return carry

        lax.fori_loop(0, qi, body, 0)

    @pl.when(bounded)
    def _():
        block(qi, True, False, first=True)
        full_blocks(False)
        for c in range(2):
            l_scr[c] = jnp.broadcast_to(jnp.sum(l_scr[c], axis=-1, keepdims=True), l_scr.shape[1:])

    @pl.when(jnp.logical_not(bounded))
    def _():
        m_scr[...] = jnp.full_like(m_scr, -1e30)
        l_scr[...] = jnp.zeros_like(l_scr)
        acc_scr[...] = jnp.zeros_like(acc_scr)
        full_blocks(True)
        block(qi, True, True)

    lam = (jnp.exp(jnp.sum(lq1_ref[...] * lk1_ref[...], axis=-1, keepdims=True))
           - jnp.exp(jnp.sum(lq2_ref[...] * lk2_ref[...], axis=-1, keepdims=True))
           + lam_init)
    inv1 = jnp.tile(1.0 / l_scr[0], (1, hw // LANES))
    inv2 = jnp.tile(1.0 / l_scr[1], (1, hw // LANES))
    o = acc_scr[0] * inv1 - lam * (acc_scr[1] * inv2)
    ms = jnp.mean(o * o, axis=-1, keepdims=True)
    o_ref[0] = (o * lax.rsqrt(ms + EPS) * (sub_ref[...] * (1.0 - lam_init))).astype(o_ref.dtype)


def _diff_attention(q, kv, q_gain, subln, lq1, lk1, lq2, lk2, *, lam_init, name, tq=1024):
    b, s, width = q.shape
    hw = 2 * LANES
    nh = width // hw
    tq = min(tq, s)
    vec = pl.BlockSpec((1, LANES), lambda bi, h, i: (0, 0))
    head_vec = pl.BlockSpec((1, hw), lambda bi, h, i: (0, 0))
    return pl.pallas_call(
        functools.partial(_attn_kernel, tq=tq, lam_init=lam_init),
        grid=(b, nh, s // tq),
        in_specs=[
            pl.BlockSpec((1, tq, hw), lambda bi, h, i: (bi, i, h)),
            pl.BlockSpec((1, s, hw), lambda bi, h, i: (bi, 0, h)),
            pl.BlockSpec((1, s, hw), lambda bi, h, i: (bi, 0, nh + h)),
            head_vec, head_vec,
            vec, vec, vec, vec,
        ],
        out_specs=pl.BlockSpec((1, tq, hw), lambda bi, h, i: (bi, i, h)),
        out_shape=jax.ShapeDtypeStruct((b, s, width), BF16),
        scratch_shapes=[
            pltpu.VMEM((2, tq, LANES), F32),
            pltpu.VMEM((2, tq, LANES), F32),
            pltpu.VMEM((2, tq, hw), F32),
            pltpu.VMEM((2, 1, LANES), F32),
            pltpu.SMEM((1,), jnp.int32),
        ],
        compiler_params=_params("parallel", "parallel", "arbitrary"),
        name=name,
    )(q, kv, kv, q_gain.reshape(1, hw), subln.reshape(1, hw), lq1.reshape(1, LANES), lk1.reshape(1, LANES),
      lq2.reshape(1, LANES), lk2.reshape(1, LANES))


def kernel(x, p, ln_mix, ln_mlp, ln_ple, a_w_in, a_lb, a_onorm, a_w_out, kv_norm, w_k, w_v, k_norm, b_w_q, q_norm, lam_q1, lam_k1, lam_q2, lam_k2, b_subln, b_w_out, mlp_up, mlp_down, ple_proj, ple_gate):
    b, s, d = x.shape
    depth = ln_mix.shape[0]
    n_a = a_w_in.shape[0]
    t = b * s
    x2 = x.reshape(t, d)
    kv = None
    for i in range(depth):
        if i < n_a:
            j = i
            width = a_w_out.shape[1]
            qig, log_f = _norm_linear(
                x2, ln_mix[i], (_cast_bf16(a_w_in, j, name=f"cast_hgrn_in_{i}"),), a_lb,
                sections=(("silu", 0, 0, 0), ("plain", 0, 0, 2), ("silu", 0, 0, 3), ("log_forget", 1, 0, 1)),
                aux_section=3, sec_width=width, out_dtypes=(BF16, F32), lb_rows=i + 1, name=f"hgrn_in_{i}")
            o = _hgrn(qig.reshape(b, s, 3 * width), log_f.reshape(b, s, width), a_onorm[j], name=f"hgrn_{i}")
            w_out = _cast_bf16(a_w_out, j, name=f"cast_hgrn_out_{i}")
        else:
            j = i - n_a
            width = b_w_q.shape[2]
            q_scale = math.log2(math.e) / math.sqrt(LANES)
            head_gain = jnp.tile(q_norm[j].reshape(1, -1), (1, width // (2 * LANES)))
            (q,) = _norm_linear(x2, ln_mix[i], (_cast_bf16(b_w_q, j, name=f"cast_attn_q_{i}"),), head_gain,
                                sections=(("head_rms", 0, 0, 0),), aux_section=0, sec_width=width,
                                out_dtypes=(BF16,), scale=q_scale, name=f"attn_q_{i}")
            lam_init = 0.8 - 0.6 * math.exp(-0.3 * i)
            o = _diff_attention(q.reshape(b, s, width), kv, q_norm[j] * q_scale, b_subln[j], lam_q1[j], lam_k1[j],
                                lam_q2[j], lam_k2[j], lam_init=lam_init, name=f"attn_{i}")
            w_out = _cast_bf16(b_w_out, j, name=f"cast_attn_out_{i}")
        x2 = _mixer_out_mlp(x2, o.reshape(t, width), w_out, ln_mlp[i],
                            _cast_bf16(mlp_up, i, name=f"cast_mlp_up_{i}"),
                            _cast_bf16(mlp_down, i, name=f"cast_mlp_down_{i}"), name=f"mlp_{i}")
        x2 = _ple(x2, ln_ple[i], p.reshape(depth, t, -1), i, _cast_bf16(ple_gate, i, name=f"cast_ple_gate_{i}"),
                  _cast_bf16(ple_proj, i, name=f"cast_ple_proj_{i}"), name=f"ple_{i}")
        if i == n_a - 1:
            width = w_k.shape[1]
            head_gain = jnp.tile(k_norm.reshape(1, -1), (1, width // (2 * LANES)))
            (kv,) = _norm_linear(x2, kv_norm, (_cast_bf16(w_k[None], 0, name="cast_w_k"),
                                               _cast_bf16(w_v[None], 0, name="cast_w_v")), head_gain,
                                 sections=(("head_rms", 0, 0, 0), ("plain", 0, 1, 0)), aux_section=0,
                                 sec_width=width, out_dtypes=(BF16,), name="shared_kv")
            kv = kv.reshape(b, s, 2 * width)
    return x2.reshape(b, s, d)
```

```python
import functools
import math

import numpy as np
import jax
import jax.numpy as jnp
from jax import lax
from jax.experimental import pallas as pl
from jax.experimental.pallas import tpu as pltpu

EPS = 1e-6
LANES = 128
HGRN_CHUNK = 128
HGRN_UNROLL = 4
CAST_BLOCK_ELEMS = 2 * 1024 * 1024
ROW_CHUNK = 256
SAFE_LOG2_SHIFT = 60.0
F32 = jnp.float32
BF16 = jnp.bfloat16
VMEM_LIMIT_BYTES = 56 * 1024 * 1024

_NT = (((1,), (1,)), ((), ()))
_TN = (((0,), (0,)), ((), ()))


def _params(*sem, **extra):
    return pltpu.CompilerParams(dimension_semantics=sem, vmem_limit_bytes=VMEM_LIMIT_BYTES, **extra)


def _rms_bf16(x, gain):
    ms = jnp.mean(x * x, axis=-1, keepdims=True)
    return (x * lax.rsqrt(ms + EPS) * gain).astype(BF16)


def _silu(x):
    return x * jax.nn.sigmoid(x)


def _group_rms(y, gain, scale):
    outs = []
    for c in range(y.shape[1] // LANES):
        sl = slice(c * LANES, (c + 1) * LANES)
        yc = y[:, sl]
        ms = jnp.mean(yc * yc, axis=-1, keepdims=True)
        outs.append(yc * lax.rsqrt(ms + EPS) * (gain[:, sl] * scale))
    return jnp.concatenate(outs, axis=1)


def _row_chunks(tm):
    rc = min(ROW_CHUNK, tm)
    return [slice(r * rc, (r + 1) * rc) for r in range(tm // rc)]


def _norm_linear_kernel(x_ref, g_ref, aux_ref, *rest, sections, n_weights, sec_blocks, lb_rows, scale):
    w_refs, o_refs, h_scr = rest[:n_weights], rest[n_weights:-1], rest[-1]
    j = pl.program_id(1)

    def epilogue(name, acc):
        if name == "silu":
            return _silu(acc)
        if name == "plain":
            return acc
        if name == "head_rms":
            return _group_rms(acc, aux_ref[...], scale)
        if name == "log_forget":
            a = aux_ref[...]
            e = jnp.exp(a - jnp.max(a, axis=0, keepdims=True))
            lb = jnp.sum(e[:lb_rows], axis=0, keepdims=True) / jnp.sum(e, axis=0, keepdims=True)
            return jnp.log2(lb + (1.0 - lb) * jax.nn.sigmoid(acc))
        raise ValueError(name)

    def run(first, name, o_ref, w_ref):
        for rows in _row_chunks(x_ref.shape[0]):
            if first:
                h = _rms_bf16(x_ref[rows, :], g_ref[...])
                h_scr[rows, :] = h
            else:
                h = h_scr[rows, :]
            acc = jnp.dot(h, w_ref[...], preferred_element_type=F32)
            o_ref[rows, :] = epilogue(name, acc).astype(o_ref.dtype)

    for s, (name, out_idx, w_idx, _) in enumerate(sections):
        lo, hi = s * sec_blocks, (s + 1) * sec_blocks
        if s == 0:
            pl.when(j == 0)(functools.partial(run, True, name, o_refs[out_idx], w_refs[w_idx]))
            lo = 1
        if lo < hi:
            pl.when(jnp.logical_and(j >= lo, j < hi))(
                functools.partial(run, False, name, o_refs[out_idx], w_refs[w_idx]))


def _step_table(j, table):
    out = table[0]
    for jj in range(1, len(table)):
        if table[jj] != table[jj - 1]:
            out = jnp.where(j >= jj, table[jj], out)
    return out


def _norm_linear(x2, gain, ws, aux, *, sections, aux_section, sec_width, out_dtypes, name, lb_rows=1, scale=1.0,
                 tm=1024, tn=1024):
    t, k = x2.shape
    tm, tn = min(tm, t), min(tn, sec_width)
    sb = sec_width // tn
    n_steps = len(sections) * sb
    first_step = [None] * len(out_dtypes)
    n_blocks = [0] * len(out_dtypes)
    for s, (_, o, _, _) in enumerate(sections):
        if first_step[o] is None:
            first_step[o] = s * sb
        assert s * sb == first_step[o] + n_blocks[o], "an output's sections must be consecutive"
        n_blocks[o] += sb

    def out_spec(o):
        return pl.BlockSpec((tm, tn), lambda i, j: (i, jnp.clip(j - first_step[o], 0, n_blocks[o] - 1)))

    def weight_spec(w_idx):
        cols = [None] * n_steps
        for s, (_, _, wi, w_sec) in enumerate(sections):
            if wi == w_idx:
                for r in range(sb):
                    cols[s * sb + r] = w_sec * sb + r
        held = next(c for c in reversed(cols) if c is not None)
        for jj in range(n_steps):
            if cols[jj] is None:
                cols[jj] = held
            held = cols[jj]
        return pl.BlockSpec((k, tn), lambda i, j: (0, _step_table(j, cols)))

    kern = functools.partial(_norm_linear_kernel, sections=tuple(sections), n_weights=len(ws), sec_blocks=sb,
                             lb_rows=lb_rows, scale=scale)
    return pl.pallas_call(
        kern,
        grid=(t // tm, n_steps),
        in_specs=[
            pl.BlockSpec((tm, k), lambda i, j: (i, 0)),
            pl.BlockSpec((1, k), lambda i, j: (0, 0)),
            pl.BlockSpec((aux.shape[0], tn), lambda i, j: (0, jnp.clip(j - aux_section * sb, 0, sb - 1))),
        ] + [weight_spec(wi) for wi in range(len(ws))],
        out_specs=[out_spec(o) for o in range(len(out_dtypes))],
        out_shape=[jax.ShapeDtypeStruct((t, n_blocks[o] * tn), out_dtypes[o]) for o in range(len(out_dtypes))],
        scratch_shapes=[pltpu.VMEM((tm, k), BF16)],
        compiler_params=_params("parallel", "arbitrary"),
        name=name,
    )(x2, gain.reshape(1, k), aux, *ws)


def _cast_kernel(w_ref, o_ref):
    o_ref[...] = w_ref[...].astype(o_ref.dtype)


def _cast_bf16(w, layer, *, name):
    _, r, c = w.shape
    tc = min(c, 2048)
    tr = min(r, max(8, CAST_BLOCK_ELEMS // tc))
    return pl.pallas_call(
        _cast_kernel,
        grid=(r // tr, c // tc),
        in_specs=[pl.BlockSpec((None, tr, tc), lambda i, j: (layer, i, j))],
        out_specs=pl.BlockSpec((tr, tc), lambda i, j: (i, j)),
        out_shape=jax.ShapeDtypeStruct((r, c), BF16),
        compiler_params=_params("parallel", "parallel"),
        name=name,
    )(w)


def _mixer_out_mlp_kernel(x_ref, a_ref, wo_ref, g_ref, up_ref, down_ref, o_ref, h_scr):
    def mixer_out():
        for rows in _row_chunks(x_ref.shape[0]):
            x1 = x_ref[rows, :] + jnp.dot(a_ref[rows, :], wo_ref[...], preferred_element_type=F32)
            o_ref[rows, :] = x1
            h_scr[rows, :] = _rms_bf16(x1, g_ref[...])

    def mlp_tile():
        for rows in _row_chunks(x_ref.shape[0]):
            u = jnp.dot(h_scr[rows, :], up_ref[...], preferred_element_type=F32)
            a = jnp.square(jnp.maximum(u, 0.0)).astype(BF16)
            o_ref[rows, :] += jnp.dot(a, down_ref[...], preferred_element_type=F32)

    j = pl.program_id(1)
    pl.when(j == 0)(mixer_out)
    pl.when(j > 0)(mlp_tile)


def _mixer_out_mlp(x2, a, w_out, gain, up, down, *, name, tm=512, tf=1024):
    t, d = x2.shape
    ka = a.shape[1]
    f = up.shape[1]
    tm, tf = min(tm, t), min(tf, f)
    return pl.pallas_call(
        _mixer_out_mlp_kernel,
        grid=(t // tm, 1 + f // tf),
        in_specs=[
            pl.BlockSpec((tm, d), lambda i, j: (i, 0)),
            pl.BlockSpec((tm, ka), lambda i, j: (i, 0)),
            pl.BlockSpec((ka, d), lambda i, j: (0, 0), pipeline_mode=pl.Buffered(1)),
            pl.BlockSpec((1, d), lambda i, j: (0, 0)),
            pl.BlockSpec((d, tf), lambda i, j: (0, jnp.maximum(j - 1, 0))),
            pl.BlockSpec((tf, d), lambda i, j: (jnp.maximum(j - 1, 0), 0)),
        ],
        out_specs=pl.BlockSpec((tm, d), lambda i, j: (i, 0)),
        out_shape=jax.ShapeDtypeStruct((t, d), F32),
        scratch_shapes=[pltpu.VMEM((tm, d), BF16)],
        compiler_params=_params("parallel", "arbitrary"),
        name=name,
    )(x2, a, w_out, gain.reshape(1, d), up, down)


def _ple_kernel(x_ref, g_ref, p_ref, gw_ref, pw_ref, o_ref, h_scr, *, tn):
    j = pl.program_id(1)
    cols = pl.ds(pl.multiple_of(j * tn, tn), tn)

    def run(first):
        for rows in _row_chunks(x_ref.shape[0]):
            if first:
                h = _rms_bf16(x_ref[rows, :], g_ref[...])
                h_scr[rows, :] = h
            else:
                h = h_scr[rows, :]
            gate = jax.nn.sigmoid(jnp.dot(h, gw_ref[...], preferred_element_type=F32))
            emb = jnp.dot(p_ref[rows, :].astype(BF16), pw_ref[...], preferred_element_type=F32)
            o_ref[rows, :] = x_ref[rows, cols] + gate * emb

    pl.when(j == 0)(lambda: run(True))
    pl.when(j > 0)(lambda: run(False))


def _ple(x2, gain, p3, layer, gate_w, proj_w, *, name, tm=1024, tn=1024):
    t, d = x2.shape
    pd = p3.shape[2]
    tm, tn = min(tm, t), min(tn, d)
    return pl.pallas_call(
        functools.partial(_ple_kernel, tn=tn),
        grid=(t // tm, d // tn),
        in_specs=[
            pl.BlockSpec((tm, d), lambda i, j: (i, 0)),
            pl.BlockSpec((1, d), lambda i, j: (0, 0)),
            pl.BlockSpec((None, tm, pd), lambda i, j: (layer, i, 0)),
            pl.BlockSpec((d, tn), lambda i, j: (0, j)),
            pl.BlockSpec((pd, tn), lambda i, j: (0, j)),
        ],
        out_specs=pl.BlockSpec((tm, tn), lambda i, j: (i, j)),
        out_shape=jax.ShapeDtypeStruct((t, d), F32),
        scratch_shapes=[pltpu.VMEM((tm, d), BF16)],
        compiler_params=_params("parallel", "arbitrary"),
        name=name,
    )(x2, gain.reshape(1, d), p3, gate_w, proj_w)


def _hgrn_levels(c):
    return [c >> (l + 1) for l in range(int(math.log2(c)))]


def _hgrn_constants(c):
    idx = np.arange(c)
    x = idx[:, None] ^ idx[None, :]
    masks = [((x >= m) & (x < 2 * m) & ((idx[:, None] & m) != 0)).astype(np.float32) for m in _hgrn_levels(c)]
    return np.tril(np.ones((c, c), np.float32)), np.stack(masks, 0)


def _level_log_decay(g_cum, g, m):
    c, w = g_cum.shape
    if m == 1:
        odd = (lax.broadcasted_iota(jnp.int32, (c, w), 0) & 1) != 0
        return jnp.where(odd, g, 0.0)
    if m < 8:
        g3 = g_cum.reshape(c // 8, 8, w)
        sub = lax.broadcasted_iota(jnp.int32, g3.shape, 1)
        if m == 4:
            mid = g3[:, 3:4, :]
        else:
            mid = jnp.where(sub < 4, g3[:, 1:2, :], g3[:, 5:6, :])
        return (-jnp.abs(g3 - mid)).reshape(c, w)
    g3 = g_cum.reshape(c // (2 * m), 2 * m, w)
    mid = g3[:, m - 1:m, :]
    return jnp.concatenate([mid - g3[:, :m, :], g3[:, m:, :] - mid], axis=1).reshape(c, w)


def _pair_rows(lower, upper, m):
    c, w = lower.shape
    if m < 8:
        return jnp.where((lax.broadcasted_iota(jnp.int32, (c, w), 0) & m) != 0, lower, upper)
    lo3 = lower.reshape(c // (2 * m), 2 * m, w)
    up3 = upper.reshape(c // (2 * m), 2 * m, w)
    return jnp.concatenate([up3[:, :m, :], lo3[:, m:, :]], axis=1).reshape(c, w)


def _hgrn_kernel(q_ref, f_ref, v_ref, gate_ref, tri_ref, msk_ref, on_ref, o_ref, st_scr, *, c, hb, nchunks):
    @pl.when(pl.program_id(2) == 0)
    def _():
        st_scr[...] = jnp.zeros_like(st_scr)

    levels = _hgrn_levels(c)

    def chunk(ci, carry):
        rows = pl.ds(pl.multiple_of(ci * c, c), c)
        g_all = f_ref[0, rows, :]
        g_hi = g_all.astype(BF16)
        g_lo = (g_all - g_hi.astype(F32)).astype(BF16)
        tri = tri_ref[...]
        g_cum_all = (jnp.dot(tri, g_hi, preferred_element_type=F32)
                     + jnp.dot(tri, g_lo, preferred_element_type=F32))
        k_all = 1.0 - jnp.exp2(g_all)
        g_last = g_cum_all[c - 1:c, :]
        q_dec_all = jnp.exp2(g_cum_all)
        k_dec_all = jnp.exp2(g_last - g_cum_all)
        st_dec_all = jnp.exp2(g_last)
        lvl_dec_all = [jnp.exp2(_level_log_decay(g_cum_all, g_all, m)) for m in levels]
        for h in range(hb):
            hl = slice(h * LANES, (h + 1) * LANES)
            qs = q_ref[0, rows, hl].astype(F32)
            vb = v_ref[0, rows, hl]
            v = vb.astype(F32)
            k = k_all[:, hl]
            a = jnp.zeros((c, c), F32)
            for l, m in enumerate(levels):
                x = (_pair_rows(qs, k, m) * lvl_dec_all[l][:, hl]).astype(BF16)
                a = a + msk_ref[l] * lax.dot_general(x, x, _NT, preferred_element_type=F32)
            o = jnp.dot(a.astype(BF16), vb, preferred_element_type=F32)
            o = o + jnp.sum(qs * k, axis=-1, keepdims=True) * v
            st = st_scr[h]
            qg = (qs * q_dec_all[:, hl]).astype(BF16)
            o = o + lax.dot_general(qg, st.astype(BF16), _NT, preferred_element_type=F32)
            kt = (k * k_dec_all[:, hl]).astype(BF16)
            st_scr[h] = st * st_dec_all[:, hl] + lax.dot_general(vb, kt, _TN, preferred_element_type=F32)
            ms = jnp.mean(o * o, axis=-1, keepdims=True)
            y = o * lax.rsqrt(ms + EPS) * on_ref[...] * gate_ref[0, rows, hl].astype(F32)
            o_ref[0, rows, hl] = y.astype(o_ref.dtype)
        return carry

    unroll = HGRN_UNROLL if nchunks % HGRN_UNROLL == 0 else 1

    def trip(u, carry):
        for r in range(unroll):
            chunk(u * unroll + r, carry)
        return carry

    lax.fori_loop(0, nchunks // unroll, trip, 0)


def _hgrn(qig, log_f, onorm, *, name, rows=1024, hb=4):
    b, s, width = log_f.shape
    c = HGRN_CHUNK
    rows = min(rows, s)
    hw = hb * LANES
    nsb = width // hw
    tri, masks = _hgrn_constants(c)
    nl = masks.shape[0]

    def sec(k):
        return pl.BlockSpec((1, rows, hw), lambda bi, hg, r: (bi, r, k * nsb + hg))

    kern = functools.partial(_hgrn_kernel, c=c, hb=hb, nchunks=rows // c)
    return pl.pallas_call(
        kern,
        grid=(b, nsb, s // rows),
        in_specs=[
            sec(0), sec(0), sec(1), sec(2),
            pl.BlockSpec((c, c), lambda bi, hg, r: (0, 0)),
            pl.BlockSpec((nl, c, c), lambda bi, hg, r: (0, 0, 0)),
            pl.BlockSpec((1, LANES), lambda bi, hg, r: (0, 0)),
        ],
        out_specs=pl.BlockSpec((1, rows, hw), lambda bi, hg, r: (bi, r, hg)),
        out_shape=jax.ShapeDtypeStruct((b, s, width), BF16),
        scratch_shapes=[pltpu.VMEM((hb, LANES, LANES), F32)],
        compiler_params=_params("parallel", "parallel", "arbitrary"),
        name=name,
    )(qig, log_f, qig, qig, jnp.asarray(tri, BF16), jnp.asarray(masks, F32), onorm.reshape(1, LANES))


def _attn_kernel(q_ref, k_ref, v_ref, qg_ref, sub_ref, lq1_ref, lk1_ref, lq2_ref, lk2_ref, o_ref,
                 m_scr, l_scr, acc_scr, shift_scr, flag_scr, *, tq, lam_init):
    qi = pl.program_id(2)
    hw = acc_scr.shape[-1]
    maps = [slice(c * LANES, (c + 1) * LANES) for c in range(2)]

    @pl.when(qi == 0)
    def _():
        def key_block(i, carry):
            kb = k_ref[0, pl.ds(pl.multiple_of(i * tq, tq), tq), :].astype(F32)
            sq = [jnp.sum(kb[:, sl] * kb[:, sl], axis=-1, keepdims=True) for sl in maps]
            return tuple(jnp.maximum(carry[c], jnp.max(sq[c], axis=0, keepdims=True)) for c in range(2))

        zero = jnp.zeros((1, 1), F32)
        k_sq = lax.fori_loop(0, k_ref.shape[1] // tq, key_block, (zero, zero))
        bounds = []
        for c, sl in enumerate(maps):
            q_max = math.sqrt(LANES) * jnp.max(jnp.abs(qg_ref[:, sl]), axis=-1, keepdims=True)
            bounds.append(q_max * jnp.sqrt(k_sq[c]))
            shift_scr[c] = jnp.broadcast_to(bounds[c], (1, LANES))
        flag_scr[0] = (jnp.max(jnp.maximum(bounds[0], bounds[1])) < SAFE_LOG2_SHIFT).astype(jnp.int32)

    bounded = flag_scr[0] == 1

    def causal(shape, row0):
        row = lax.broadcasted_iota(jnp.int32, shape, 0) + row0
        return lax.broadcasted_iota(jnp.int32, shape, 1) <= row

    def block(j, masked, running_max, first=False):
        base = pl.multiple_of(j * tq, tq)
        for qrows in _row_chunks(tq):
            nk = qrows.stop if masked else tq
            keys = pl.ds(base, nk)
            vb = v_ref[0, keys, :]
            for c, sl in enumerate(maps):
                s = lax.dot_general(q_ref[0, qrows, sl], k_ref[0, keys, sl], _NT,
                                    preferred_element_type=F32)
                if running_max:
                    if masked:
                        s = jnp.where(causal(s.shape, qrows.start), s, -jnp.inf)
                    m_prev = m_scr[c, qrows, :]
                    m_new = jnp.maximum(m_prev, jnp.max(s, axis=-1, keepdims=True))
                    alpha = jnp.exp2(m_prev - m_new)
                    p = jnp.exp2(s - jnp.tile(m_new, (1, nk // LANES)))
                    l_scr[c, qrows, :] = alpha * l_scr[c, qrows, :] + jnp.sum(p, axis=-1, keepdims=True)
                    acc_scr[c, qrows, :] = (jnp.tile(alpha, (1, hw // LANES)) * acc_scr[c, qrows, :]
                                            + jnp.dot(p.astype(BF16), vb, preferred_element_type=F32))
                    m_scr[c, qrows, :] = m_new
                else:
                    p = jnp.exp2(s - jnp.tile(shift_scr[c], (1, nk // LANES)))
                    if masked:
                        p = jnp.where(causal(p.shape, qrows.start), p, 0.0)
                    lane_sums = p[:, 0:LANES]
                    for g in range(1, nk // LANES):
                        lane_sums = lane_sums + p[:, g * LANES:(g + 1) * LANES]
                    pv = jnp.dot(p.astype(BF16), vb, preferred_element_type=F32)
                    if first:
                        l_scr[c, qrows, :] = lane_sums
                        acc_scr[c, qrows, :] = pv
                    else:
                        l_scr[c, qrows, :] += lane_sums
                        acc_scr[c, qrows, :] += pv

    def full_blocks(running_max):
        if running_max:
            def body(j, carry):
                block(j, False, True)
                return carry

            lax.fori_loop(0, qi, body, 0)
            return

        def pair(u, carry):
            block(2 * u, False, False)
            block(2 * u + 1, False, False)
            return carry

        lax.fori_loop(0, qi // 2, pair, 0)
        pl.when(qi % 2 == 1)(lambda: block(qi - 1, False, False))

    @pl.when(bounded)
    def _():
        block(qi, True, False, first=True)
        full_blocks(False)
        for c in range(2):
            l_scr[c] = jnp.broadcast_to(jnp.sum(l_scr[c], axis=-1, keepdims=True), l_scr.shape[1:])

    @pl.when(jnp.logical_not(bounded))
    def _():
        m_scr[...] = jnp.full_like(m_scr, -1e30)
        l_scr[...] = jnp.zeros_like(l_scr)
        acc_scr[...] = jnp.zeros_like(acc_scr)
        full_blocks(True)
        block(qi, True, True)

    lam = (jnp.exp(jnp.sum(lq1_ref[...] * lk1_ref[...], axis=-1, keepdims=True))
           - jnp.exp(jnp.sum(lq2_ref[...] * lk2_ref[...], axis=-1, keepdims=True))
           + lam_init)
    inv1 = jnp.tile(1.0 / l_scr[0], (1, hw // LANES))
    inv2 = jnp.tile(1.0 / l_scr[1], (1, hw // LANES))
    o = acc_scr[0] * inv1 - lam * (acc_scr[1] * inv2)
    ms = jnp.mean(o * o, axis=-1, keepdims=True)
    o_ref[0] = (o * lax.rsqrt(ms + EPS) * (sub_ref[...] * (1.0 - lam_init))).astype(o_ref.dtype)


def _diff_attention(q, kv, q_gain, subln, lq1, lk1, lq2, lk2, *, lam_init, name, tq=1024):
    b, s, width = q.shape
    hw = 2 * LANES
    nh = width // hw
    tq = min(tq, s)
    vec = pl.BlockSpec((1, LANES), lambda bi, h, i: (0, 0))
    head_vec = pl.BlockSpec((1, hw), lambda bi, h, i: (0, 0))
    return pl.pallas_call(
        functools.partial(_attn_kernel, tq=tq, lam_init=lam_init),
        grid=(b, nh, s // tq),
        in_specs=[
            pl.BlockSpec((1, tq, hw), lambda bi, h, i: (bi, i, h)),
            pl.BlockSpec((1, s, hw), lambda bi, h, i: (bi, 0, h)),
            pl.BlockSpec((1, s, hw), lambda bi, h, i: (bi, 0, nh + h)),
            head_vec, head_vec,
            vec, vec, vec, vec,
        ],
        out_specs=pl.BlockSpec((1, tq, hw), lambda bi, h, i: (bi, i, h)),
        out_shape=jax.ShapeDtypeStruct((b, s, width), BF16),
        scratch_shapes=[
            pltpu.VMEM((2, tq, LANES), F32),
            pltpu.VMEM((2, tq, LANES), F32),
            pltpu.VMEM((2, tq, hw), F32),
            pltpu.VMEM((2, 1, LANES), F32),
            pltpu.SMEM((1,), jnp.int32),
        ],
        compiler_params=_params("parallel", "parallel", "arbitrary"),
        name=name,
    )(q, kv, kv, q_gain.reshape(1, hw), subln.reshape(1, hw), lq1.reshape(1, LANES), lk1.reshape(1, LANES),
      lq2.reshape(1, LANES), lk2.reshape(1, LANES))


def kernel(x, p, ln_mix, ln_mlp, ln_ple, a_w_in, a_lb, a_onorm, a_w_out, kv_norm, w_k, w_v, k_norm, b_w_q, q_norm, lam_q1, lam_k1, lam_q2, lam_k2, b_subln, b_w_out, mlp_up, mlp_down, ple_proj, ple_gate):
    b, s, d = x.shape
    depth = ln_mix.shape[0]
    n_a = a_w_in.shape[0]
    t = b * s
    x2 = x.reshape(t, d)
    kv = None
    for i in range(depth):
        if i < n_a:
            j = i
            width = a_w_out.shape[1]
            qig, log_f = _norm_linear(
                x2, ln_mix[i], (_cast_bf16(a_w_in, j, name=f"cast_hgrn_in_{i}"),), a_lb,
                sections=(("silu", 0, 0, 0), ("plain", 0, 0, 2), ("silu", 0, 0, 3), ("log_forget", 1, 0, 1)),
                aux_section=3, sec_width=width, out_dtypes=(BF16, F32), lb_rows=i + 1, name=f"hgrn_in_{i}")
            o = _hgrn(qig.reshape(b, s, 3 * width), log_f.reshape(b, s, width), a_onorm[j], name=f"hgrn_{i}")
            w_out = _cast_bf16(a_w_out, j, name=f"cast_hgrn_out_{i}")
        else:
            j = i - n_a
            width = b_w_q.shape[2]
            q_scale = math.log2(math.e) / math.sqrt(LANES)
            head_gain = jnp.tile(q_norm[j].reshape(1, -1), (1, width // (2 * LANES)))
            (q,) = _norm_linear(x2, ln_mix[i], (_cast_bf16(b_w_q, j, name=f"cast_attn_q_{i}"),), head_gain,
                                sections=(("head_rms", 0, 0, 0),), aux_section=0, sec_width=width,
                                out_dtypes=(BF16,), scale=q_scale, name=f"attn_q_{i}")
            lam_init = 0.8 - 0.6 * math.exp(-0.3 * i)
            o = _diff_attention(q.reshape(b, s, width), kv, q_norm[j] * q_scale, b_subln[j], lam_q1[j], lam_k1[j],
                                lam_q2[j], lam_k2[j], lam_init=lam_init, name=f"attn_{i}")
            w_out = _cast_bf16(b_w_out, j, name=f"cast_attn_out_{i}")
        x2 = _mixer_out_mlp(x2, o.reshape(t, width), w_out, ln_mlp[i],
                            _cast_bf16(mlp_up, i, name=f"cast_mlp_up_{i}"),
                            _cast_bf16(mlp_down, i, name=f"cast_mlp_down_{i}"), name=f"mlp_{i}")
        x2 = _ple(x2, ln_ple[i], p.reshape(depth, t, -1), i, _cast_bf16(ple_gate, i, name=f"cast_ple_gate_{i}"),
                  _cast_bf16(ple_proj, i, name=f"cast_ple_proj_{i}"), name=f"ple_{i}")
        if i == n_a - 1:
            width = w_k.shape[1]
            head_gain = jnp.tile(k_norm.reshape(1, -1), (1, width // (2 * LANES)))
            (kv,) = _norm_linear(x2, kv_norm, (_cast_bf16(w_k[None], 0, name="cast_w_k"),
                                               _cast_bf16(w_v[None], 0, name="cast_w_v")), head_gain,
                                 sections=(("head_rms", 0, 0, 0), ("plain", 0, 1, 0)), aux_section=0,
                                 sec_width=width, out_dtypes=(BF16,), name="shared_kv")
            kv = kv.reshape(b, s, 2 * width)
    return x2.reshape(b, s, d)
```

```python
import functools
import math

import numpy as np
import jax
import jax.numpy as jnp
from jax import lax
from jax.experimental import pallas as pl
from jax.experimental.pallas import tpu as pltpu

EPS = 1e-6
LANES = 128
HGRN_CHUNK = 128
HGRN_UNROLL = 4
HGRN_SAFE_LOG2_RANGE = 100.0
CAST_BLOCK_ELEMS = 2 * 1024 * 1024
ROW_CHUNK = 256
SAFE_LOG2_SHIFT = 60.0
F32 = jnp.float32
BF16 = jnp.bfloat16
VMEM_LIMIT_BYTES = 56 * 1024 * 1024

_NT = (((1,), (1,)), ((), ()))
_TN = (((0,), (0,)), ((), ()))


def _params(*sem, **extra):
    return pltpu.CompilerParams(dimension_semantics=sem, vmem_limit_bytes=VMEM_LIMIT_BYTES, **extra)


def _rms_bf16(x, gain):
    ms = jnp.mean(x * x, axis=-1, keepdims=True)
    return (x * lax.rsqrt(ms + EPS) * gain).astype(BF16)


def _silu(x):
    return x * jax.nn.sigmoid(x)


def _group_rms(y, gain, scale):
    outs = []
    for c in range(y.shape[1] // LANES):
        sl = slice(c * LANES, (c + 1) * LANES)
        yc = y[:, sl]
        ms = jnp.mean(yc * yc, axis=-1, keepdims=True)
        outs.append(yc * lax.rsqrt(ms + EPS) * (gain[:, sl] * scale))
    return jnp.concatenate(outs, axis=1)


def _row_chunks(tm):
    rc = min(ROW_CHUNK, tm)
    return [slice(r * rc, (r + 1) * rc) for r in range(tm // rc)]


def _norm_linear_kernel(x_ref, g_ref, aux_ref, *rest, sections, n_weights, sec_blocks, lb_rows, scale):
    w_refs, o_refs, h_scr = rest[:n_weights], rest[n_weights:-1], rest[-1]
    j = pl.program_id(1)

    def epilogue(name, acc):
        if name == "silu":
            return _silu(acc)
        if name == "plain":
            return acc
        if name == "head_rms":
            return _group_rms(acc, aux_ref[...], scale)
        if name == "log_forget":
            a = aux_ref[...]
            e = jnp.exp(a - jnp.max(a, axis=0, keepdims=True))
            lb = jnp.sum(e[:lb_rows], axis=0, keepdims=True) / jnp.sum(e, axis=0, keepdims=True)
            return jnp.log2(lb + (1.0 - lb) * jax.nn.sigmoid(acc))
        raise ValueError(name)

    def run(first, name, o_ref, w_ref):
        for rows in _row_chunks(x_ref.shape[0]):
            if first:
                h = _rms_bf16(x_ref[rows, :], g_ref[...])
                h_scr[rows, :] = h
            else:
                h = h_scr[rows, :]
            acc = jnp.dot(h, w_ref[...], preferred_element_type=F32)
            o_ref[rows, :] = epilogue(name, acc).astype(o_ref.dtype)

    for s, (name, out_idx, w_idx, _) in enumerate(sections):
        lo, hi = s * sec_blocks, (s + 1) * sec_blocks
        if s == 0:
            pl.when(j == 0)(functools.partial(run, True, name, o_refs[out_idx], w_refs[w_idx]))
            lo = 1
        if lo < hi:
            pl.when(jnp.logical_and(j >= lo, j < hi))(
                functools.partial(run, False, name, o_refs[out_idx], w_refs[w_idx]))


def _step_table(j, table):
    out = table[0]
    for jj in range(1, len(table)):
        if table[jj] != table[jj - 1]:
            out = jnp.where(j >= jj, table[jj], out)
    return out


def _norm_linear(x2, gain, ws, aux, *, sections, aux_section, sec_width, out_dtypes, name, lb_rows=1, scale=1.0,
                 tm=1024, tn=1024):
    t, k = x2.shape
    tm, tn = min(tm, t), min(tn, sec_width)
    sb = sec_width // tn
    n_steps = len(sections) * sb
    first_step = [None] * len(out_dtypes)
    n_blocks = [0] * len(out_dtypes)
    for s, (_, o, _, _) in enumerate(sections):
        if first_step[o] is None:
            first_step[o] = s * sb
        assert s * sb == first_step[o] + n_blocks[o], "an output's sections must be consecutive"
        n_blocks[o] += sb

    def out_spec(o):
        return pl.BlockSpec((tm, tn), lambda i, j: (i, jnp.clip(j - first_step[o], 0, n_blocks[o] - 1)))

    def weight_spec(w_idx):
        cols = [None] * n_steps
        for s, (_, _, wi, w_sec) in enumerate(sections):
            if wi == w_idx:
                for r in range(sb):
                    cols[s * sb + r] = w_sec * sb + r
        held = next(c for c in reversed(cols) if c is not None)
        for jj in range(n_steps):
            if cols[jj] is None:
                cols[jj] = held
            held = cols[jj]
        return pl.BlockSpec((k, tn), lambda i, j: (0, _step_table(j, cols)))

    kern = functools.partial(_norm_linear_kernel, sections=tuple(sections), n_weights=len(ws), sec_blocks=sb,
                             lb_rows=lb_rows, scale=scale)
    return pl.pallas_call(
        kern,
        grid=(t // tm, n_steps),
        in_specs=[
            pl.BlockSpec((tm, k), lambda i, j: (i, 0)),
            pl.BlockSpec((1, k), lambda i, j: (0, 0)),
            pl.BlockSpec((aux.shape[0], tn), lambda i, j: (0, jnp.clip(j - aux_section * sb, 0, sb - 1))),
        ] + [weight_spec(wi) for wi in range(len(ws))],
        out_specs=[out_spec(o) for o in range(len(out_dtypes))],
        out_shape=[jax.ShapeDtypeStruct((t, n_blocks[o] * tn), out_dtypes[o]) for o in range(len(out_dtypes))],
        scratch_shapes=[pltpu.VMEM((tm, k), BF16)],
        compiler_params=_params("parallel", "arbitrary"),
        name=name,
    )(x2, gain.reshape(1, k), aux, *ws)


def _cast_kernel(w_ref, o_ref):
    o_ref[...] = w_ref[...].astype(o_ref.dtype)


def _cast_bf16(w, layer, *, name):
    _, r, c = w.shape
    tc = min(c, 2048)
    tr = min(r, max(8, CAST_BLOCK_ELEMS // tc))
    return pl.pallas_call(
        _cast_kernel,
        grid=(r // tr, c // tc),
        in_specs=[pl.BlockSpec((None, tr, tc), lambda i, j: (layer, i, j))],
        out_specs=pl.BlockSpec((tr, tc), lambda i, j: (i, j)),
        out_shape=jax.ShapeDtypeStruct((r, c), BF16),
        compiler_params=_params("parallel", "parallel"),
        name=name,
    )(w)


def _mixer_out_mlp_kernel(x_ref, a_ref, wo_ref, g_ref, up_ref, down_ref, o_ref, h_scr):
    def mixer_out():
        for rows in _row_chunks(x_ref.shape[0]):
            x1 = x_ref[rows, :] + jnp.dot(a_ref[rows, :], wo_ref[...], preferred_element_type=F32)
            o_ref[rows, :] = x1
            h_scr[rows, :] = _rms_bf16(x1, g_ref[...])

    def mlp_tile():
        for rows in _row_chunks(x_ref.shape[0]):
            u = jnp.dot(h_scr[rows, :], up_ref[...], preferred_element_type=F32)
            a = jnp.square(jnp.maximum(u, 0.0)).astype(BF16)
            o_ref[rows, :] += jnp.dot(a, down_ref[...], preferred_element_type=F32)

    j = pl.program_id(1)
    pl.when(j == 0)(mixer_out)
    pl.when(j > 0)(mlp_tile)


def _mixer_out_mlp(x2, a, w_out, gain, up, down, *, name, tm=512, tf=1024):
    t, d = x2.shape
    ka = a.shape[1]
    f = up.shape[1]
    tm, tf = min(tm, t), min(tf, f)
    return pl.pallas_call(
        _mixer_out_mlp_kernel,
        grid=(t // tm, 1 + f // tf),
        in_specs=[
            pl.BlockSpec((tm, d), lambda i, j: (i, 0)),
            pl.BlockSpec((tm, ka), lambda i, j: (i, 0)),
            pl.BlockSpec((ka, d), lambda i, j: (0, 0), pipeline_mode=pl.Buffered(1)),
            pl.BlockSpec((1, d), lambda i, j: (0, 0)),
            pl.BlockSpec((d, tf), lambda i, j: (0, jnp.maximum(j - 1, 0))),
            pl.BlockSpec((tf, d), lambda i, j: (jnp.maximum(j - 1, 0), 0)),
        ],
        out_specs=pl.BlockSpec((tm, d), lambda i, j: (i, 0)),
        out_shape=jax.ShapeDtypeStruct((t, d), F32),
        scratch_shapes=[pltpu.VMEM((tm, d), BF16)],
        compiler_params=_params("parallel", "arbitrary"),
        name=name,
    )(x2, a, w_out, gain.reshape(1, d), up, down)


def _ple_kernel(x_ref, g_ref, p_ref, gw_ref, pw_ref, o_ref, h_scr, *, tn):
    j = pl.program_id(1)
    cols = pl.ds(pl.multiple_of(j * tn, tn), tn)

    def run(first):
        for rows in _row_chunks(x_ref.shape[0]):
            if first:
                h = _rms_bf16(x_ref[rows, :], g_ref[...])
                h_scr[rows, :] = h
            else:
                h = h_scr[rows, :]
            gate = jax.nn.sigmoid(jnp.dot(h, gw_ref[...], preferred_element_type=F32))
            emb = jnp.dot(p_ref[rows, :].astype(BF16), pw_ref[...], preferred_element_type=F32)
            o_ref[rows, :] = x_ref[rows, cols] + gate * emb

    pl.when(j == 0)(lambda: run(True))
    pl.when(j > 0)(lambda: run(False))


def _ple(x2, gain, p3, layer, gate_w, proj_w, *, name, tm=1024, tn=1024):
    t, d = x2.shape
    pd = p3.shape[2]
    tm, tn = min(tm, t), min(tn, d)
    return pl.pallas_call(
        functools.partial(_ple_kernel, tn=tn),
        grid=(t // tm, d // tn),
        in_specs=[
            pl.BlockSpec((tm, d), lambda i, j: (i, 0)),
            pl.BlockSpec((1, d), lambda i, j: (0, 0)),
            pl.BlockSpec((None, tm, pd), lambda i, j: (layer, i, 0)),
            pl.BlockSpec((d, tn), lambda i, j: (0, j)),
            pl.BlockSpec((pd, tn), lambda i, j: (0, j)),
        ],
        out_specs=pl.BlockSpec((tm, tn), lambda i, j: (i, j)),
        out_shape=jax.ShapeDtypeStruct((t, d), F32),
        scratch_shapes=[pltpu.VMEM((tm, d), BF16)],
        compiler_params=_params("parallel", "arbitrary"),
        name=name,
    )(x2, gain.reshape(1, d), p3, gate_w, proj_w)


def _hgrn_levels(c):
    return [c >> (l + 1) for l in range(int(math.log2(c)))]


def _hgrn_constants(c):
    idx = np.arange(c)
    x = idx[:, None] ^ idx[None, :]
    masks = [((x >= m) & (x < 2 * m) & ((idx[:, None] & m) != 0)).astype(np.float32) for m in _hgrn_levels(c)]
    return np.tril(np.ones((c, c), np.float32)), np.stack(masks, 0)


def _level_log_decay(g_cum, g, m):
    c, w = g_cum.shape
    if m == 1:
        odd = (lax.broadcasted_iota(jnp.int32, (c, w), 0) & 1) != 0
        return jnp.where(odd, g, 0.0)
    if m < 8:
        g3 = g_cum.reshape(c // 8, 8, w)
        sub = lax.broadcasted_iota(jnp.int32, g3.shape, 1)
        if m == 4:
            mid = g3[:, 3:4, :]
        else:
            mid = jnp.where(sub < 4, g3[:, 1:2, :], g3[:, 5:6, :])
        return (-jnp.abs(g3 - mid)).reshape(c, w)
    g3 = g_cum.reshape(c // (2 * m), 2 * m, w)
    mid = g3[:, m - 1:m, :]
    return jnp.concatenate([mid - g3[:, :m, :], g3[:, m:, :] - mid], axis=1).reshape(c, w)


def _pair_rows(lower, upper, m):
    c, w = lower.shape
    if m < 8:
        return jnp.where((lax.broadcasted_iota(jnp.int32, (c, w), 0) & m) != 0, lower, upper)
    lo3 = lower.reshape(c // (2 * m), 2 * m, w)
    up3 = upper.reshape(c // (2 * m), 2 * m, w)
    return jnp.concatenate([up3[:, :m, :], lo3[:, m:, :]], axis=1).reshape(c, w)


def _hgrn_kernel(q_ref, f_ref, v_ref, gate_ref, tri_ref, msk_ref, on_ref, o_ref, st_scr, *, c, hb, nchunks):
    @pl.when(pl.program_id(2) == 0)
    def _():
        st_scr[...] = jnp.zeros_like(st_scr)

    levels = _hgrn_levels(c)
    hw = hb * LANES

    def chunk_decay(ci, worst):
        g = f_ref[0, pl.ds(pl.multiple_of(ci * c, c), c), :]
        halves = jnp.minimum(jnp.sum(g[:c // 2], axis=0, keepdims=True), jnp.sum(g[c // 2:], axis=0, keepdims=True))
        return jnp.minimum(worst, halves)

    worst = lax.fori_loop(0, nchunks, chunk_decay, jnp.zeros((1, hw), F32))
    mild = jnp.min(worst) > -HGRN_SAFE_LOG2_RANGE

    def chunk(ci, mild_decay):
        rows = pl.ds(pl.multiple_of(ci * c, c), c)
        g_all = f_ref[0, rows, :]
        g_hi = g_all.astype(BF16)
        g_lo = (g_all - g_hi.astype(F32)).astype(BF16)
        tri = tri_ref[...]
        g_cum_all = (jnp.dot(tri, g_hi, preferred_element_type=F32)
                     + jnp.dot(tri, g_lo, preferred_element_type=F32))
        k_all = 1.0 - jnp.exp2(g_all)
        g_last = g_cum_all[c - 1:c, :]
        q_dec_all = jnp.exp2(g_cum_all)
        k_dec_all = jnp.exp2(g_last - g_cum_all)
        st_dec_all = jnp.exp2(g_last)
        if mild_decay:
            g_mid = g_cum_all[c // 2 - 1:c // 2, :]
            q_mid_all = jnp.exp2(g_cum_all - g_mid)
            k_mid_all = jnp.exp2(g_mid - g_cum_all)
            causal = (lax.broadcasted_iota(jnp.int32, (c, c), 0) >= lax.broadcasted_iota(jnp.int32, (c, c), 1))
        else:
            lvl_dec_all = [jnp.exp2(_level_log_decay(g_cum_all, g_all, m)) for m in levels]
        for h in range(hb):
            hl = slice(h * LANES, (h + 1) * LANES)
            qs = q_ref[0, rows, hl].astype(F32)
            vb = v_ref[0, rows, hl]
            v = vb.astype(F32)
            k = k_all[:, hl]
            qg = (qs * q_dec_all[:, hl]).astype(BF16)
            if mild_decay:
                qm = (qs * q_mid_all[:, hl]).astype(BF16)
                km = (k * k_mid_all[:, hl]).astype(BF16)
                a = jnp.where(causal, lax.dot_general(qm, km, _NT, preferred_element_type=F32), 0.0)
                o = jnp.dot(a.astype(BF16), vb, preferred_element_type=F32)
            else:
                a = jnp.zeros((c, c), F32)
                for l, m in enumerate(levels):
                    x = (_pair_rows(qs, k, m) * lvl_dec_all[l][:, hl]).astype(BF16)
                    a = a + msk_ref[l] * lax.dot_general(x, x, _NT, preferred_element_type=F32)
                o = jnp.dot(a.astype(BF16), vb, preferred_element_type=F32)
                o = o + jnp.sum(qs * k, axis=-1, keepdims=True) * v
            st = st_scr[h]
            o = o + lax.dot_general(qg, st.astype(BF16), _NT, preferred_element_type=F32)
            kt = (k * k_dec_all[:, hl]).astype(BF16)
            st_scr[h] = st * st_dec_all[:, hl] + lax.dot_general(vb, kt, _TN, preferred_element_type=F32)
            ms = jnp.mean(o * o, axis=-1, keepdims=True)
            y = o * lax.rsqrt(ms + EPS) * on_ref[...] * gate_ref[0, rows, hl].astype(F32)
            o_ref[0, rows, hl] = y.astype(o_ref.dtype)

    unroll = HGRN_UNROLL if nchunks % HGRN_UNROLL == 0 else 1

    def all_chunks(mild_decay):
        def trip(u, carry):
            for r in range(unroll):
                chunk(u * unroll + r, mild_decay)
            return carry

        lax.fori_loop(0, nchunks // unroll, trip, 0)

    pl.when(mild)(lambda: all_chunks(True))
    pl.when(jnp.logical_not(mild))(lambda: all_chunks(False))


def _hgrn(qig, log_f, onorm, *, name, rows=1024, hb=4):
    b, s, width = log_f.shape
    c = HGRN_CHUNK
    rows = min(rows, s)
    hw = hb * LANES
    nsb = width // hw
    tri, masks = _hgrn_constants(c)
    nl = masks.shape[0]

    def sec(k):
        return pl.BlockSpec((1, rows, hw), lambda bi, hg, r: (bi, r, k * nsb + hg))

    kern = functools.partial(_hgrn_kernel, c=c, hb=hb, nchunks=rows // c)
    return pl.pallas_call(
        kern,
        grid=(b, nsb, s // rows),
        in_specs=[
            sec(0), sec(0), sec(1), sec(2),
            pl.BlockSpec((c, c), lambda bi, hg, r: (0, 0)),
            pl.BlockSpec((nl, c, c), lambda bi, hg, r: (0, 0, 0)),
            pl.BlockSpec((1, LANES), lambda bi, hg, r: (0, 0)),
        ],
        out_specs=pl.BlockSpec((1, rows, hw), lambda bi, hg, r: (bi, r, hg)),
        out_shape=jax.ShapeDtypeStruct((b, s, width), BF16),
        scratch_shapes=[pltpu.VMEM((hb, LANES, LANES), F32)],
        compiler_params=_params("parallel", "parallel", "arbitrary"),
        name=name,
    )(qig, log_f, qig, qig, jnp.asarray(tri, BF16), jnp.asarray(masks, F32), onorm.reshape(1, LANES))


def _attn_kernel(q_ref, k_ref, v_ref, qg_ref, sub_ref, lq1_ref, lk1_ref, lq2_ref, lk2_ref, o_ref,
                 m_scr, l_scr, acc_scr, shift_scr, flag_scr, *, tq, lam_init):
    qi = pl.program_id(2)
    hw = acc_scr.shape[-1]
    maps = [slice(c * LANES, (c + 1) * LANES) for c in range(2)]

    @pl.when(qi == 0)
    def _():
        def key_block(i, carry):
            kb = k_ref[0, pl.ds(pl.multiple_of(i * tq, tq), tq), :].astype(F32)
            sq = [jnp.sum(kb[:, sl] * kb[:, sl], axis=-1, keepdims=True) for sl in maps]
            return tuple(jnp.maximum(carry[c], jnp.max(sq[c], axis=0, keepdims=True)) for c in range(2))

        zero = jnp.zeros((1, 1), F32)
        k_sq = lax.fori_loop(0, k_ref.shape[1] // tq, key_block, (zero, zero))
        bounds = []
        for c, sl in enumerate(maps):
            q_max = math.sqrt(LANES) * jnp.max(jnp.abs(qg_ref[:, sl]), axis=-1, keepdims=True)
            bounds.append(q_max * jnp.sqrt(k_sq[c]))
            shift_scr[c] = jnp.broadcast_to(bounds[c], (1, LANES))
        flag_scr[0] = (jnp.max(jnp.maximum(bounds[0], bounds[1])) < SAFE_LOG2_SHIFT).astype(jnp.int32)

    bounded = flag_scr[0] == 1

    def causal(shape, row0):
        row = lax.broadcasted_iota(jnp.int32, shape, 0) + row0
        return lax.broadcasted_iota(jnp.int32, shape, 1) <= row

    def block(j, masked, running_max, first=False):
        base = pl.multiple_of(j * tq, tq)
        for qrows in _row_chunks(tq):
            nk = qrows.stop if masked else tq
            keys = pl.ds(base, nk)
            vb = v_ref[0, keys, :]
            for c, sl in enumerate(maps):
                s = lax.dot_general(q_ref[0, qrows, sl], k_ref[0, keys, sl], _NT,
                                    preferred_element_type=F32)
                if running_max:
                    if masked:
                        s = jnp.where(causal(s.shape, qrows.start), s, -jnp.inf)
                    m_prev = m_scr[c, qrows, :]
                    m_new = jnp.maximum(m_prev, jnp.max(s, axis=-1, keepdims=True))
                    alpha = jnp.exp2(m_prev - m_new)
                    p = jnp.exp2(s - jnp.tile(m_new, (1, nk // LANES)))
                    l_scr[c, qrows, :] = alpha * l_scr[c, qrows, :] + jnp.sum(p, axis=-1, keepdims=True)
                    acc_scr[c, qrows, :] = (jnp.tile(alpha, (1, hw // LANES)) * acc_scr[c, qrows, :]
                                            + jnp.dot(p.astype(BF16), vb, preferred_element_type=F32))
                    m_scr[c, qrows, :] = m_new
                else:
                    p = jnp.exp2(s - jnp.tile(shift_scr[c], (1, nk // LANES)))
                    if masked:
                        p = jnp.where(causal(p.shape, qrows.start), p, 0.0)
                    lane_sums = p[:, 0:LANES]
                    for g in range(1, nk // LANES):
                        lane_sums = lane_sums + p[:, g * LANES:(g + 1) * LANES]
                    pv = jnp.dot(p.astype(BF16), vb, preferred_element_type=F32)
                    if first:
                        l_scr[c, qrows, :] = lane_sums
                        acc_scr[c, qrows, :] = pv
                    else:
                        l_scr[c, qrows, :] += lane_sums
                        acc_scr[c, qrows, :] += pv

    def full_blocks(running_max):
        if running_max:
            def body(j, carry):
                block(j, False, True)
                return carry

            lax.fori_loop(0, qi, body, 0)
            return

        def pair(u, carry):
            block(2 * u, False, False)
            block(2 * u + 1, False, False)
            return carry

        lax.fori_loop(0, qi // 2, pair, 0)
        pl.when(qi % 2 == 1)(lambda: block(qi - 1, False, False))

    @pl.when(bounded)
    def _():
        block(qi, True, False, first=True)
        full_blocks(False)
        for c in range(2):
            l_scr[c] = jnp.broadcast_to(jnp.sum(l_scr[c], axis=-1, keepdims=True), l_scr.shape[1:])

    @pl.when(jnp.logical_not(bounded))
    def _():
        m_scr[...] = jnp.full_like(m_scr, -1e30)
        l_scr[...] = jnp.zeros_like(l_scr)
        acc_scr[...] = jnp.zeros_like(acc_scr)
        full_blocks(True)
        block(qi, True, True)

    lam = (jnp.exp(jnp.sum(lq1_ref[...] * lk1_ref[...], axis=-1, keepdims=True))
           - jnp.exp(jnp.sum(lq2_ref[...] * lk2_ref[...], axis=-1, keepdims=True))
           + lam_init)
    inv1 = jnp.tile(1.0 / l_scr[0], (1, hw // LANES))
    inv2 = jnp.tile(1.0 / l_scr[1], (1, hw // LANES))
    o = acc_scr[0] * inv1 - lam * (acc_scr[1] * inv2)
    ms = jnp.mean(o * o, axis=-1, keepdims=True)
    o_ref[0] = (o * lax.rsqrt(ms + EPS) * (sub_ref[...] * (1.0 - lam_init))).astype(o_ref.dtype)


def _diff_attention(q, kv, q_gain, subln, lq1, lk1, lq2, lk2, *, lam_init, name, tq=1024):
    b, s, width = q.shape
    hw = 2 * LANES
    nh = width // hw
    tq = min(tq, s)
    vec = pl.BlockSpec((1, LANES), lambda bi, h, i: (0, 0))
    head_vec = pl.BlockSpec((1, hw), lambda bi, h, i: (0, 0))
    return pl.pallas_call(
        functools.partial(_attn_kernel, tq=tq, lam_init=lam_init),
        grid=(b, nh, s // tq),
        in_specs=[
            pl.BlockSpec((1, tq, hw), lambda bi, h, i: (bi, i, h)),
            pl.BlockSpec((1, s, hw), lambda bi, h, i: (bi, 0, h)),
            pl.BlockSpec((1, s, hw), lambda bi, h, i: (bi, 0, nh + h)),
            head_vec, head_vec,
            vec, vec, vec, vec,
        ],
        out_specs=pl.BlockSpec((1, tq, hw), lambda bi, h, i: (bi, i, h)),
        out_shape=jax.ShapeDtypeStruct((b, s, width), BF16),
        scratch_shapes=[
            pltpu.VMEM((2, tq, LANES), F32),
            pltpu.VMEM((2, tq, LANES), F32),
            pltpu.VMEM((2, tq, hw), F32),
            pltpu.VMEM((2, 1, LANES), F32),
            pltpu.SMEM((1,), jnp.int32),
        ],
        compiler_params=_params("parallel", "parallel", "arbitrary"),
        name=name,
    )(q, kv, kv, q_gain.reshape(1, hw), subln.reshape(1, hw), lq1.reshape(1, LANES), lk1.reshape(1, LANES),
      lq2.reshape(1, LANES), lk2.reshape(1, LANES))


def kernel(x, p, ln_mix, ln_mlp, ln_ple, a_w_in, a_lb, a_onorm, a_w_out, kv_norm, w_k, w_v, k_norm, b_w_q, q_norm, lam_q1, lam_k1, lam_q2, lam_k2, b_subln, b_w_out, mlp_up, mlp_down, ple_proj, ple_gate):
    b, s, d = x.shape
    depth = ln_mix.shape[0]
    n_a = a_w_in.shape[0]
    t = b * s
    x2 = x.reshape(t, d)
    kv = None
    for i in range(depth):
        if i < n_a:
            j = i
            width = a_w_out.shape[1]
            qig, log_f = _norm_linear(
                x2, ln_mix[i], (_cast_bf16(a_w_in, j, name=f"cast_hgrn_in_{i}"),), a_lb,
                sections=(("silu", 0, 0, 0), ("plain", 0, 0, 2), ("silu", 0, 0, 3), ("log_forget", 1, 0, 1)),
                aux_section=3, sec_width=width, out_dtypes=(BF16, F32), lb_rows=i + 1, name=f"hgrn_in_{i}")
            o = _hgrn(qig.reshape(b, s, 3 * width), log_f.reshape(b, s, width), a_onorm[j], name=f"hgrn_{i}")
            w_out = _cast_bf16(a_w_out, j, name=f"cast_hgrn_out_{i}")
        else:
            j = i - n_a
            width = b_w_q.shape[2]
            q_scale = math.log2(math.e) / math.sqrt(LANES)
            head_gain = jnp.tile(q_norm[j].reshape(1, -1), (1, width // (2 * LANES)))
            (q,) = _norm_linear(x2, ln_mix[i], (_cast_bf16(b_w_q, j, name=f"cast_attn_q_{i}"),), head_gain,
                                sections=(("head_rms", 0, 0, 0),), aux_section=0, sec_width=width,
                                out_dtypes=(BF16,), scale=q_scale, name=f"attn_q_{i}")
            lam_init = 0.8 - 0.6 * math.exp(-0.3 * i)
            o = _diff_attention(q.reshape(b, s, width), kv, q_norm[j] * q_scale, b_subln[j], lam_q1[j], lam_k1[j],
                                lam_q2[j], lam_k2[j], lam_init=lam_init, name=f"attn_{i}")
            w_out = _cast_bf16(b_w_out, j, name=f"cast_attn_out_{i}")
        x2 = _mixer_out_mlp(x2, o.reshape(t, width), w_out, ln_mlp[i],
                            _cast_bf16(mlp_up, i, name=f"cast_mlp_up_{i}"),
                            _cast_bf16(mlp_down, i, name=f"cast_mlp_down_{i}"), name=f"mlp_{i}")
        x2 = _ple(x2, ln_ple[i], p.reshape(depth, t, -1), i, _cast_bf16(ple_gate, i, name=f"cast_ple_gate_{i}"),
                  _cast_bf16(ple_proj, i, name=f"cast_ple_proj_{i}"), name=f"ple_{i}")
        if i == n_a - 1:
            width = w_k.shape[1]
            head_gain = jnp.tile(k_norm.reshape(1, -1), (1, width // (2 * LANES)))
            (kv,) = _norm_linear(x2, kv_norm, (_cast_bf16(w_k[None], 0, name="cast_w_k"),
                                               _cast_bf16(w_v[None], 0, name="cast_w_v")), head_gain,
                                 sections=(("head_rms", 0, 0, 0), ("plain", 0, 1, 0)), aux_section=0,
                                 sec_width=width, out_dtypes=(BF16,), name="shared_kv")
            kv = kv.reshape(b, s, 2 * width)
    return x2.reshape(b, s, d)
```

```python
import functools
import math

import numpy as np
import jax
import jax.numpy as jnp
from jax import lax
from jax.experimental import pallas as pl
from jax.experimental.pallas import tpu as pltpu

EPS = 1e-6
LANES = 128
HGRN_CHUNK = 128
HGRN_UNROLL = 8
HGRN_SAFE_LOG2_RANGE = 100.0
CAST_BLOCK_ELEMS = 2 * 1024 * 1024
ROW_CHUNK = 256
SAFE_LOG2_SHIFT = 60.0
F32 = jnp.float32
BF16 = jnp.bfloat16
VMEM_LIMIT_BYTES = 56 * 1024 * 1024

_NT = (((1,), (1,)), ((), ()))
_TN = (((0,), (0,)), ((), ()))


def _params(*sem, **extra):
    return pltpu.CompilerParams(dimension_semantics=sem, vmem_limit_bytes=VMEM_LIMIT_BYTES, **extra)


def _rms_bf16(x, gain):
    ms = jnp.mean(x * x, axis=-1, keepdims=True)
    return (x * lax.rsqrt(ms + EPS) * gain).astype(BF16)


def _silu(x):
    return x * jax.nn.sigmoid(x)


def _group_rms(y, gain, scale):
    outs = []
    for c in range(y.shape[1] // LANES):
        sl = slice(c * LANES, (c + 1) * LANES)
        yc = y[:, sl]
        ms = jnp.mean(yc * yc, axis=-1, keepdims=True)
        outs.append(yc * lax.rsqrt(ms + EPS) * (gain[:, sl] * scale))
    return jnp.concatenate(outs, axis=1)


def _row_chunks(tm):
    rc = min(ROW_CHUNK, tm)
    return [slice(r * rc, (r + 1) * rc) for r in range(tm // rc)]


def _norm_linear_kernel(x_ref, g_ref, aux_ref, *rest, sections, n_weights, sec_blocks, lb_rows, scale):
    w_refs, o_refs, h_scr = rest[:n_weights], rest[n_weights:-1], rest[-1]
    j = pl.program_id(1)

    def epilogue(name, acc):
        if name == "silu":
            return _silu(acc)
        if name == "plain":
            return acc
        if name == "head_rms":
            return _group_rms(acc, aux_ref[...], scale)
        if name == "log_forget":
            a = aux_ref[...]
            e = jnp.exp(a - jnp.max(a, axis=0, keepdims=True))
            lb = jnp.sum(e[:lb_rows], axis=0, keepdims=True) / jnp.sum(e, axis=0, keepdims=True)
            return jnp.log2(lb + (1.0 - lb) * jax.nn.sigmoid(acc))
        raise ValueError(name)

    def run(first, name, o_ref, w_ref):
        for rows in _row_chunks(x_ref.shape[0]):
            if first:
                h = _rms_bf16(x_ref[rows, :], g_ref[...])
                h_scr[rows, :] = h
            else:
                h = h_scr[rows, :]
            acc = jnp.dot(h, w_ref[...], preferred_element_type=F32)
            o_ref[rows, :] = epilogue(name, acc).astype(o_ref.dtype)

    for s, (name, out_idx, w_idx, _) in enumerate(sections):
        lo, hi = s * sec_blocks, (s + 1) * sec_blocks
        if s == 0:
            pl.when(j == 0)(functools.partial(run, True, name, o_refs[out_idx], w_refs[w_idx]))
            lo = 1
        if lo < hi:
            pl.when(jnp.logical_and(j >= lo, j < hi))(
                functools.partial(run, False, name, o_refs[out_idx], w_refs[w_idx]))


def _step_table(j, table):
    out = table[0]
    for jj in range(1, len(table)):
        if table[jj] != table[jj - 1]:
            out = jnp.where(j >= jj, table[jj], out)
    return out


def _norm_linear(x2, gain, ws, aux, *, sections, aux_section, sec_width, out_dtypes, name, lb_rows=1, scale=1.0,
                 tm=1024, tn=1024):
    t, k = x2.shape
    tm, tn = min(tm, t), min(tn, sec_width)
    sb = sec_width // tn
    n_steps = len(sections) * sb
    first_step = [None] * len(out_dtypes)
    n_blocks = [0] * len(out_dtypes)
    for s, (_, o, _, _) in enumerate(sections):
        if first_step[o] is None:
            first_step[o] = s * sb
        assert s * sb == first_step[o] + n_blocks[o], "an output's sections must be consecutive"
        n_blocks[o] += sb

    def out_spec(o):
        return pl.BlockSpec((tm, tn), lambda i, j: (i, jnp.clip(j - first_step[o], 0, n_blocks[o] - 1)))

    def weight_spec(w_idx):
        cols = [None] * n_steps
        for s, (_, _, wi, w_sec) in enumerate(sections):
            if wi == w_idx:
                for r in range(sb):
                    cols[s * sb + r] = w_sec * sb + r
        held = next(c for c in reversed(cols) if c is not None)
        for jj in range(n_steps):
            if cols[jj] is None:
                cols[jj] = held
            held = cols[jj]
        return pl.BlockSpec((k, tn), lambda i, j: (0, _step_table(j, cols)))

    kern = functools.partial(_norm_linear_kernel, sections=tuple(sections), n_weights=len(ws), sec_blocks=sb,
                             lb_rows=lb_rows, scale=scale)
    return pl.pallas_call(
        kern,
        grid=(t // tm, n_steps),
        in_specs=[
            pl.BlockSpec((tm, k), lambda i, j: (i, 0)),
            pl.BlockSpec((1, k), lambda i, j: (0, 0)),
            pl.BlockSpec((aux.shape[0], tn), lambda i, j: (0, jnp.clip(j - aux_section * sb, 0, sb - 1))),
        ] + [weight_spec(wi) for wi in range(len(ws))],
        out_specs=[out_spec(o) for o in range(len(out_dtypes))],
        out_shape=[jax.ShapeDtypeStruct((t, n_blocks[o] * tn), out_dtypes[o]) for o in range(len(out_dtypes))],
        scratch_shapes=[pltpu.VMEM((tm, k), BF16)],
        compiler_params=_params("parallel", "arbitrary"),
        name=name,
    )(x2, gain.reshape(1, k), aux, *ws)


def _cast_kernel(w_ref, o_ref):
    o_ref[...] = w_ref[...].astype(o_ref.dtype)


def _cast_bf16(w, layer, *, name):
    _, r, c = w.shape
    tc = min(c, 2048)
    tr = min(r, max(8, CAST_BLOCK_ELEMS // tc))
    return pl.pallas_call(
        _cast_kernel,
        grid=(r // tr, c // tc),
        in_specs=[pl.BlockSpec((None, tr, tc), lambda i, j: (layer, i, j))],
        out_specs=pl.BlockSpec((tr, tc), lambda i, j: (i, j)),
        out_shape=jax.ShapeDtypeStruct((r, c), BF16),
        compiler_params=_params("parallel", "parallel"),
        name=name,
    )(w)


def _mixer_out_mlp_kernel(x_ref, a_ref, wo_ref, g_ref, up_ref, down_ref, o_ref, h_scr):
    def mixer_out():
        for rows in _row_chunks(x_ref.shape[0]):
            x1 = x_ref[rows, :] + jnp.dot(a_ref[rows, :], wo_ref[...], preferred_element_type=F32)
            o_ref[rows, :] = x1
            h_scr[rows, :] = _rms_bf16(x1, g_ref[...])

    def mlp_tile():
        for rows in _row_chunks(x_ref.shape[0]):
            u = jnp.dot(h_scr[rows, :], up_ref[...], preferred_element_type=F32)
            a = jnp.square(jnp.maximum(u, 0.0)).astype(BF16)
            o_ref[rows, :] += jnp.dot(a, down_ref[...], preferred_element_type=F32)

    j = pl.program_id(1)
    pl.when(j == 0)(mixer_out)
    pl.when(j > 0)(mlp_tile)


def _mixer_out_mlp(x2, a, w_out, gain, up, down, *, name, tm=512, tf=1024):
    t, d = x2.shape
    ka = a.shape[1]
    f = up.shape[1]
    tm, tf = min(tm, t), min(tf, f)
    return pl.pallas_call(
        _mixer_out_mlp_kernel,
        grid=(t // tm, 1 + f // tf),
        in_specs=[
            pl.BlockSpec((tm, d), lambda i, j: (i, 0)),
            pl.BlockSpec((tm, ka), lambda i, j: (i, 0)),
            pl.BlockSpec((ka, d), lambda i, j: (0, 0), pipeline_mode=pl.Buffered(1)),
            pl.BlockSpec((1, d), lambda i, j: (0, 0)),
            pl.BlockSpec((d, tf), lambda i, j: (0, jnp.maximum(j - 1, 0))),
            pl.BlockSpec((tf, d), lambda i, j: (jnp.maximum(j - 1, 0), 0)),
        ],
        out_specs=pl.BlockSpec((tm, d), lambda i, j: (i, 0)),
        out_shape=jax.ShapeDtypeStruct((t, d), F32),
        scratch_shapes=[pltpu.VMEM((tm, d), BF16)],
        compiler_params=_params("parallel", "arbitrary"),
        name=name,
    )(x2, a, w_out, gain.reshape(1, d), up, down)


def _ple_kernel(x_ref, g_ref, p_ref, gw_ref, pw_ref, o_ref, h_scr, *, tn):
    j = pl.program_id(1)
    cols = pl.ds(pl.multiple_of(j * tn, tn), tn)

    def run(first):
        for rows in _row_chunks(x_ref.shape[0]):
            if first:
                h = _rms_bf16(x_ref[rows, :], g_ref[...])
                h_scr[rows, :] = h
            else:
                h = h_scr[rows, :]
            gate = jax.nn.sigmoid(jnp.dot(h, gw_ref[...], preferred_element_type=F32))
            emb = jnp.dot(p_ref[rows, :].astype(BF16), pw_ref[...], preferred_element_type=F32)
            o_ref[rows, :] = x_ref[rows, cols] + gate * emb

    pl.when(j == 0)(lambda: run(True))
    pl.when(j > 0)(lambda: run(False))


def _ple(x2, gain, p3, layer, gate_w, proj_w, *, name, tm=1024, tn=1024):
    t, d = x2.shape
    pd = p3.shape[2]
    tm, tn = min(tm, t), min(tn, d)
    return pl.pallas_call(
        functools.partial(_ple_kernel, tn=tn),
        grid=(t // tm, d // tn),
        in_specs=[
            pl.BlockSpec((tm, d), lambda i, j: (i, 0)),
            pl.BlockSpec((1, d), lambda i, j: (0, 0)),
            pl.BlockSpec((None, tm, pd), lambda i, j: (layer, i, 0)),
            pl.BlockSpec((d, tn), lambda i, j: (0, j)),
            pl.BlockSpec((pd, tn), lambda i, j: (0, j)),
        ],
        out_specs=pl.BlockSpec((tm, tn), lambda i, j: (i, j)),
        out_shape=jax.ShapeDtypeStruct((t, d), F32),
        scratch_shapes=[pltpu.VMEM((tm, d), BF16)],
        compiler_params=_params("parallel", "arbitrary"),
        name=name,
    )(x2, gain.reshape(1, d), p3, gate_w, proj_w)


def _hgrn_levels(c):
    return [c >> (l + 1) for l in range(int(math.log2(c)))]


def _hgrn_constants(c):
    idx = np.arange(c)
    x = idx[:, None] ^ idx[None, :]
    masks = [((x >= m) & (x < 2 * m) & ((idx[:, None] & m) != 0)).astype(np.float32) for m in _hgrn_levels(c)]
    return np.tril(np.ones((c, c), np.float32)), np.stack(masks, 0)


def _level_log_decay(g_cum, g, m):
    c, w = g_cum.shape
    if m == 1:
        odd = (lax.broadcasted_iota(jnp.int32, (c, w), 0) & 1) != 0
        return jnp.where(odd, g, 0.0)
    if m < 8:
        g3 = g_cum.reshape(c // 8, 8, w)
        sub = lax.broadcasted_iota(jnp.int32, g3.shape, 1)
        if m == 4:
            mid = g3[:, 3:4, :]
        else:
            mid = jnp.where(sub < 4, g3[:, 1:2, :], g3[:, 5:6, :])
        return (-jnp.abs(g3 - mid)).reshape(c, w)
    g3 = g_cum.reshape(c // (2 * m), 2 * m, w)
    mid = g3[:, m - 1:m, :]
    return jnp.concatenate([mid - g3[:, :m, :], g3[:, m:, :] - mid], axis=1).reshape(c, w)


def _pair_rows(lower, upper, m):
    c, w = lower.shape
    if m < 8:
        return jnp.where((lax.broadcasted_iota(jnp.int32, (c, w), 0) & m) != 0, lower, upper)
    lo3 = lower.reshape(c // (2 * m), 2 * m, w)
    up3 = upper.reshape(c // (2 * m), 2 * m, w)
    return jnp.concatenate([up3[:, :m, :], lo3[:, m:, :]], axis=1).reshape(c, w)


def _hgrn_kernel(q_ref, f_ref, v_ref, gate_ref, tri_ref, msk_ref, on_ref, o_ref, st_scr, *, c, hb, nchunks):
    @pl.when(pl.program_id(2) == 0)
    def _():
        st_scr[...] = jnp.zeros_like(st_scr)

    levels = _hgrn_levels(c)
    hw = hb * LANES

    def chunk_decay(ci, worst):
        g = f_ref[0, pl.ds(pl.multiple_of(ci * c, c), c), :]
        halves = jnp.minimum(jnp.sum(g[:c // 2], axis=0, keepdims=True), jnp.sum(g[c // 2:], axis=0, keepdims=True))
        return jnp.minimum(worst, halves)

    worst = lax.fori_loop(0, nchunks, chunk_decay, jnp.zeros((1, hw), F32))
    mild = jnp.min(worst) > -HGRN_SAFE_LOG2_RANGE

    def chunk(ci, mild_decay):
        rows = pl.ds(pl.multiple_of(ci * c, c), c)
        g_all = f_ref[0, rows, :]
        g_hi = g_all.astype(BF16)
        g_lo = (g_all - g_hi.astype(F32)).astype(BF16)
        tri = tri_ref[...]
        g_cum_all = (jnp.dot(tri, g_hi, preferred_element_type=F32)
                     + jnp.dot(tri, g_lo, preferred_element_type=F32))
        k_all = 1.0 - jnp.exp2(g_all)
        g_last = g_cum_all[c - 1:c, :]
        if mild_decay:
            g_mid = g_cum_all[c // 2 - 1:c // 2, :]
            q_mid_all = jnp.exp2(g_cum_all - g_mid)
            k_mid_all = jnp.exp2(g_mid - g_cum_all)
            to_mid_all = jnp.exp2(g_mid)
            mid_to_end_all = jnp.exp2(g_last - g_mid)
            causal = (lax.broadcasted_iota(jnp.int32, (c, c), 0) >= lax.broadcasted_iota(jnp.int32, (c, c), 1))
        else:
            q_dec_all = jnp.exp2(g_cum_all)
            k_dec_all = jnp.exp2(g_last - g_cum_all)
            st_dec_all = jnp.exp2(g_last)
            lvl_dec_all = [jnp.exp2(_level_log_decay(g_cum_all, g_all, m)) for m in levels]
        for h in range(hb):
            hl = slice(h * LANES, (h + 1) * LANES)
            qs = q_ref[0, rows, hl].astype(F32)
            vb = v_ref[0, rows, hl]
            k = k_all[:, hl]
            st = st_scr[h]
            if mild_decay:
                qm = (qs * q_mid_all[:, hl]).astype(BF16)
                km = (k * k_mid_all[:, hl]).astype(BF16)
                a = jnp.where(causal, lax.dot_general(qm, km, _NT, preferred_element_type=F32), 0.0)
                st_mid = st * to_mid_all[:, hl]
                o = (jnp.dot(a.astype(BF16), vb, preferred_element_type=F32)
                     + lax.dot_general(qm, st_mid.astype(BF16), _NT, preferred_element_type=F32))
                st_scr[h] = ((st_mid + lax.dot_general(vb, km, _TN, preferred_element_type=F32))
                             * mid_to_end_all[:, hl])
            else:
                a = jnp.zeros((c, c), F32)
                for l, m in enumerate(levels):
                    x = (_pair_rows(qs, k, m) * lvl_dec_all[l][:, hl]).astype(BF16)
                    a = a + msk_ref[l] * lax.dot_general(x, x, _NT, preferred_element_type=F32)
                o = jnp.dot(a.astype(BF16), vb, preferred_element_type=F32)
                o = o + jnp.sum(qs * k, axis=-1, keepdims=True) * vb.astype(F32)
                qg = (qs * q_dec_all[:, hl]).astype(BF16)
                o = o + lax.dot_general(qg, st.astype(BF16), _NT, preferred_element_type=F32)
                kt = (k * k_dec_all[:, hl]).astype(BF16)
                st_scr[h] = st * st_dec_all[:, hl] + lax.dot_general(vb, kt, _TN, preferred_element_type=F32)
            ms = jnp.mean(o * o, axis=-1, keepdims=True)
            y = o * lax.rsqrt(ms + EPS) * on_ref[...] * gate_ref[0, rows, hl].astype(F32)
            o_ref[0, rows, hl] = y.astype(o_ref.dtype)

    unroll = HGRN_UNROLL if nchunks % HGRN_UNROLL == 0 else 1

    def all_chunks(mild_decay):
        def trip(u, carry):
            for r in range(unroll):
                chunk(u * unroll + r, mild_decay)
            return carry

        lax.fori_loop(0, nchunks // unroll, trip, 0)

    pl.when(mild)(lambda: all_chunks(True))
    pl.when(jnp.logical_not(mild))(lambda: all_chunks(False))


def _hgrn(qig, log_f, onorm, *, name, rows=1024, hb=4):
    b, s, width = log_f.shape
    c = HGRN_CHUNK
    rows = min(rows, s)
    hw = hb * LANES
    nsb = width // hw
    tri, masks = _hgrn_constants(c)
    nl = masks.shape[0]

    def sec(k):
        return pl.BlockSpec((1, rows, hw), lambda bi, hg, r: (bi, r, k * nsb + hg))

    kern = functools.partial(_hgrn_kernel, c=c, hb=hb, nchunks=rows // c)
    return pl.pallas_call(
        kern,
        grid=(b, nsb, s // rows),
        in_specs=[
            sec(0), sec(0), sec(1), sec(2),
            pl.BlockSpec((c, c), lambda bi, hg, r: (0, 0)),
            pl.BlockSpec((nl, c, c), lambda bi, hg, r: (0, 0, 0)),
            pl.BlockSpec((1, LANES), lambda bi, hg, r: (0, 0)),
        ],
        out_specs=pl.BlockSpec((1, rows, hw), lambda bi, hg, r: (bi, r, hg)),
        out_shape=jax.ShapeDtypeStruct((b, s, width), BF16),
        scratch_shapes=[pltpu.VMEM((hb, LANES, LANES), F32)],
        compiler_params=_params("parallel", "parallel", "arbitrary"),
        name=name,
    )(qig, log_f, qig, qig, jnp.asarray(tri, BF16), jnp.asarray(masks, F32), onorm.reshape(1, LANES))


def _attn_kernel(q_ref, k_ref, v_ref, qg_ref, sub_ref, lq1_ref, lk1_ref, lq2_ref, lk2_ref, o_ref,
                 m_scr, l_scr, acc_scr, shift_scr, flag_scr, *, tq, lam_init):
    qi = pl.program_id(2)
    hw = acc_scr.shape[-1]
    maps = [slice(c * LANES, (c + 1) * LANES) for c in range(2)]

    @pl.when(qi == 0)
    def _():
        def key_block(i, carry):
            kb = k_ref[0, pl.ds(pl.multiple_of(i * tq, tq), tq), :].astype(F32)
            sq = [jnp.sum(kb[:, sl] * kb[:, sl], axis=-1, keepdims=True) for sl in maps]
            return tuple(jnp.maximum(carry[c], jnp.max(sq[c], axis=0, keepdims=True)) for c in range(2))

        zero = jnp.zeros((1, 1), F32)
        k_sq = lax.fori_loop(0, k_ref.shape[1] // tq, key_block, (zero, zero))
        bounds = []
        for c, sl in enumerate(maps):
            q_max = math.sqrt(LANES) * jnp.max(jnp.abs(qg_ref[:, sl]), axis=-1, keepdims=True)
            bounds.append(q_max * jnp.sqrt(k_sq[c]))
            shift_scr[c] = jnp.broadcast_to(bounds[c], (1, LANES))
        flag_scr[0] = (jnp.max(jnp.maximum(bounds[0], bounds[1])) < SAFE_LOG2_SHIFT).astype(jnp.int32)

    bounded = flag_scr[0] == 1

    def causal(shape, row0):
        row = lax.broadcasted_iota(jnp.int32, shape, 0) + row0
        return lax.broadcasted_iota(jnp.int32, shape, 1) <= row

    def block(j, masked, running_max, first=False):
        base = pl.multiple_of(j * tq, tq)
        for qrows in _row_chunks(tq):
            nk = qrows.stop if masked else tq
            keys = pl.ds(base, nk)
            vb = v_ref[0, keys, :]
            for c, sl in enumerate(maps):
                s = lax.dot_general(q_ref[0, qrows, sl], k_ref[0, keys, sl], _NT,
                                    preferred_element_type=F32)
                if running_max:
                    if masked:
                        s = jnp.where(causal(s.shape, qrows.start), s, -jnp.inf)
                    m_prev = m_scr[c, qrows, :]
                    m_new = jnp.maximum(m_prev, jnp.max(s, axis=-1, keepdims=True))
                    alpha = jnp.exp2(m_prev - m_new)
                    p = jnp.exp2(s - jnp.tile(m_new, (1, nk // LANES)))
                    l_scr[c, qrows, :] = alpha * l_scr[c, qrows, :] + jnp.sum(p, axis=-1, keepdims=True)
                    acc_scr[c, qrows, :] = (jnp.tile(alpha, (1, hw // LANES)) * acc_scr[c, qrows, :]
                                            + jnp.dot(p.astype(BF16), vb, preferred_element_type=F32))
                    m_scr[c, qrows, :] = m_new
                else:
                    p = jnp.exp2(s - jnp.tile(shift_scr[c], (1, nk // LANES)))
                    if masked:
                        p = jnp.where(causal(p.shape, qrows.start), p, 0.0)
                    lane_sums = p[:, 0:LANES]
                    for g in range(1, nk // LANES):
                        lane_sums = lane_sums + p[:, g * LANES:(g + 1) * LANES]
                    pv = jnp.dot(p.astype(BF16), vb, preferred_element_type=F32)
                    if first:
                        l_scr[c, qrows, :] = lane_sums
                        acc_scr[c, qrows, :] = pv
                    else:
                        l_scr[c, qrows, :] += lane_sums
                        acc_scr[c, qrows, :] += pv

    def full_blocks(running_max):
        if running_max:
            def body(j, carry):
                block(j, False, True)
                return carry

            lax.fori_loop(0, qi, body, 0)
            return

        def pair(u, carry):
            block(2 * u, False, False)
            block(2 * u + 1, False, False)
            return carry

        lax.fori_loop(0, qi // 2, pair, 0)
        pl.when(qi % 2 == 1)(lambda: block(qi - 1, False, False))

    @pl.when(bounded)
    def _():
        block(qi, True, False, first=True)
        full_blocks(False)
        for c in range(2):
            l_scr[c] = jnp.broadcast_to(jnp.sum(l_scr[c], axis=-1, keepdims=True), l_scr.shape[1:])

    @pl.when(jnp.logical_not(bounded))
    def _():
        m_scr[...] = jnp.full_like(m_scr, -1e30)
        l_scr[...] = jnp.zeros_like(l_scr)
        acc_scr[...] = jnp.zeros_like(acc_scr)
        full_blocks(True)
        block(qi, True, True)

    lam = (jnp.exp(jnp.sum(lq1_ref[...] * lk1_ref[...], axis=-1, keepdims=True))
           - jnp.exp(jnp.sum(lq2_ref[...] * lk2_ref[...], axis=-1, keepdims=True))
           + lam_init)
    inv1 = jnp.tile(1.0 / l_scr[0], (1, hw // LANES))
    inv2 = jnp.tile(1.0 / l_scr[1], (1, hw // LANES))
    o = acc_scr[0] * inv1 - lam * (acc_scr[1] * inv2)
    ms = jnp.mean(o * o, axis=-1, keepdims=True)
    o_ref[0] = (o * lax.rsqrt(ms + EPS) * (sub_ref[...] * (1.0 - lam_init))).astype(o_ref.dtype)


def _diff_attention(q, kv, q_gain, subln, lq1, lk1, lq2, lk2, *, lam_init, name, tq=1024):
    b, s, width = q.shape
    hw = 2 * LANES
    nh = width // hw
    tq = min(tq, s)
    vec = pl.BlockSpec((1, LANES), lambda bi, h, i: (0, 0))
    head_vec = pl.BlockSpec((1, hw), lambda bi, h, i: (0, 0))
    return pl.pallas_call(
        functools.partial(_attn_kernel, tq=tq, lam_init=lam_init),
        grid=(b, nh, s // tq),
        in_specs=[
            pl.BlockSpec((1, tq, hw), lambda bi, h, i: (bi, i, h)),
            pl.BlockSpec((1, s, hw), lambda bi, h, i: (bi, 0, h)),
            pl.BlockSpec((1, s, hw), lambda bi, h, i: (bi, 0, nh + h)),
            head_vec, head_vec,
            vec, vec, vec, vec,
        ],
        out_specs=pl.BlockSpec((1, tq, hw), lambda bi, h, i: (bi, i, h)),
        out_shape=jax.ShapeDtypeStruct((b, s, width), BF16),
        scratch_shapes=[
            pltpu.VMEM((2, tq, LANES), F32),
            pltpu.VMEM((2, tq, LANES), F32),
            pltpu.VMEM((2, tq, hw), F32),
            pltpu.VMEM((2, 1, LANES), F32),
            pltpu.SMEM((1,), jnp.int32),
        ],
        compiler_params=_params("parallel", "parallel", "arbitrary"),
        name=name,
    )(q, kv, kv, q_gain.reshape(1, hw), subln.reshape(1, hw), lq1.reshape(1, LANES), lk1.reshape(1, LANES),
      lq2.reshape(1, LANES), lk2.reshape(1, LANES))


def kernel(x, p, ln_mix, ln_mlp, ln_ple, a_w_in, a_lb, a_onorm, a_w_out, kv_norm, w_k, w_v, k_norm, b_w_q, q_norm, lam_q1, lam_k1, lam_q2, lam_k2, b_subln, b_w_out, mlp_up, mlp_down, ple_proj, ple_gate):
    b, s, d = x.shape
    depth = ln_mix.shape[0]
    n_a = a_w_in.shape[0]
    t = b * s
    x2 = x.reshape(t, d)
    kv = None
    for i in range(depth):
        if i < n_a:
            j = i
            width = a_w_out.shape[1]
            qig, log_f = _norm_linear(
                x2, ln_mix[i], (_cast_bf16(a_w_in, j, name=f"cast_hgrn_in_{i}"),), a_lb,
                sections=(("silu", 0, 0, 0), ("plain", 0, 0, 2), ("silu", 0, 0, 3), ("log_forget", 1, 0, 1)),
                aux_section=3, sec_width=width, out_dtypes=(BF16, F32), lb_rows=i + 1, name=f"hgrn_in_{i}")
            o = _hgrn(qig.reshape(b, s, 3 * width), log_f.reshape(b, s, width), a_onorm[j], name=f"hgrn_{i}")
            w_out = _cast_bf16(a_w_out, j, name=f"cast_hgrn_out_{i}")
        else:
            j = i - n_a
            width = b_w_q.shape[2]
            q_scale = math.log2(math.e) / math.sqrt(LANES)
            head_gain = jnp.tile(q_norm[j].reshape(1, -1), (1, width // (2 * LANES)))
            (q,) = _norm_linear(x2, ln_mix[i], (_cast_bf16(b_w_q, j, name=f"cast_attn_q_{i}"),), head_gain,
                                sections=(("head_rms", 0, 0, 0),), aux_section=0, sec_width=width,
                                out_dtypes=(BF16,), scale=q_scale, name=f"attn_q_{i}")
            lam_init = 0.8 - 0.6 * math.exp(-0.3 * i)
            o = _diff_attention(q.reshape(b, s, width), kv, q_norm[j] * q_scale, b_subln[j], lam_q1[j], lam_k1[j],
                                lam_q2[j], lam_k2[j], lam_init=lam_init, name=f"attn_{i}")
            w_out = _cast_bf16(b_w_out, j, name=f"cast_attn_out_{i}")
        x2 = _mixer_out_mlp(x2, o.reshape(t, width), w_out, ln_mlp[i],
                            _cast_bf16(mlp_up, i, name=f"cast_mlp_up_{i}"),
                            _cast_bf16(mlp_down, i, name=f"cast_mlp_down_{i}"), name=f"mlp_{i}")
        x2 = _ple(x2, ln_ple[i], p.reshape(depth, t, -1), i, _cast_bf16(ple_gate, i, name=f"cast_ple_gate_{i}"),
                  _cast_bf16(ple_proj, i, name=f"cast_ple_proj_{i}"), name=f"ple_{i}")
        if i == n_a - 1:
            width = w_k.shape[1]
            head_gain = jnp.tile(k_norm.reshape(1, -1), (1, width // (2 * LANES)))
            (kv,) = _norm_linear(x2, kv_norm, (_cast_bf16(w_k[None], 0, name="cast_w_k"),
                                               _cast_bf16(w_v[None], 0, name="cast_w_v")), head_gain,
                                 sections=(("head_rms", 0, 0, 0), ("plain", 0, 1, 0)), aux_section=0,
                                 sec_width=width, out_dtypes=(BF16,), name="shared_kv")
            kv = kv.reshape(b, s, 2 * width)
    return x2.reshape(b, s, d)
```

```python
import functools
import math

import numpy as np
import jax
import jax.numpy as jnp
from jax import lax
from jax.experimental import pallas as pl
from jax.experimental.pallas import tpu as pltpu

EPS = 1e-6
LANES = 128
HGRN_CHUNK = 128
HGRN_UNROLL = 8
HGRN_SAFE_LOG2_RANGE = 100.0
CAST_BLOCK_ELEMS = 2 * 1024 * 1024
ROW_CHUNK = 256
SAFE_LOG2_SHIFT = 60.0
F32 = jnp.float32
BF16 = jnp.bfloat16
VMEM_LIMIT_BYTES = 56 * 1024 * 1024

_NT = (((1,), (1,)), ((), ()))
_TN = (((0,), (0,)), ((), ()))


def _params(*sem, **extra):
    return pltpu.CompilerParams(dimension_semantics=sem, vmem_limit_bytes=VMEM_LIMIT_BYTES, **extra)


def _rms_bf16(x, gain):
    ms = jnp.mean(x * x, axis=-1, keepdims=True)
    return (x * lax.rsqrt(ms + EPS) * gain).astype(BF16)


def _silu(x):
    return x * jax.nn.sigmoid(x)


def _group_rms(y, gain, scale):
    outs = []
    for c in range(y.shape[1] // LANES):
        sl = slice(c * LANES, (c + 1) * LANES)
        yc = y[:, sl]
        ms = jnp.mean(yc * yc, axis=-1, keepdims=True)
        outs.append(yc * lax.rsqrt(ms + EPS) * (gain[:, sl] * scale))
    return jnp.concatenate(outs, axis=1)


def _aligned(start, multiple):
    return start if isinstance(start, int) else pl.multiple_of(start, multiple)


def _row_chunks(tm):
    rc = min(ROW_CHUNK, tm)
    return [slice(r * rc, (r + 1) * rc) for r in range(tm // rc)]


def _norm_linear_kernel(x_ref, g_ref, aux_ref, *rest, sections, n_weights, sec_blocks, lb_rows, scale):
    w_refs, o_refs, h_scr = rest[:n_weights], rest[n_weights:-1], rest[-1]
    j = pl.program_id(1)

    def epilogue(name, acc):
        if name == "silu":
            return _silu(acc)
        if name == "plain":
            return acc
        if name == "head_rms":
            return _group_rms(acc, aux_ref[...], scale)
        if name == "log_forget":
            a = aux_ref[...]
            e = jnp.exp(a - jnp.max(a, axis=0, keepdims=True))
            lb = jnp.sum(e[:lb_rows], axis=0, keepdims=True) / jnp.sum(e, axis=0, keepdims=True)
            return jnp.log2(lb + (1.0 - lb) * jax.nn.sigmoid(acc))
        raise ValueError(name)

    def run(first, name, o_ref, w_ref):
        for rows in _row_chunks(x_ref.shape[0]):
            if first:
                h = _rms_bf16(x_ref[rows, :], g_ref[...])
                h_scr[rows, :] = h
            else:
                h = h_scr[rows, :]
            acc = jnp.dot(h, w_ref[...], preferred_element_type=F32)
            o_ref[rows, :] = epilogue(name, acc).astype(o_ref.dtype)

    for s, (name, out_idx, w_idx, _) in enumerate(sections):
        lo, hi = s * sec_blocks, (s + 1) * sec_blocks
        if s == 0:
            pl.when(j == 0)(functools.partial(run, True, name, o_refs[out_idx], w_refs[w_idx]))
            lo = 1
        if lo < hi:
            pl.when(jnp.logical_and(j >= lo, j < hi))(
                functools.partial(run, False, name, o_refs[out_idx], w_refs[w_idx]))


def _step_table(j, table):
    out = table[0]
    for jj in range(1, len(table)):
        if table[jj] != table[jj - 1]:
            out = jnp.where(j >= jj, table[jj], out)
    return out


def _norm_linear(x2, gain, ws, aux, *, sections, aux_section, sec_width, out_dtypes, name, lb_rows=1, scale=1.0,
                 tm=1024, tn=1024):
    t, k = x2.shape
    tm, tn = min(tm, t), min(tn, sec_width)
    sb = sec_width // tn
    n_steps = len(sections) * sb
    first_step = [None] * len(out_dtypes)
    n_blocks = [0] * len(out_dtypes)
    for s, (_, o, _, _) in enumerate(sections):
        if first_step[o] is None:
            first_step[o] = s * sb
        assert s * sb == first_step[o] + n_blocks[o], "an output's sections must be consecutive"
        n_blocks[o] += sb

    def out_spec(o):
        return pl.BlockSpec((tm, tn), lambda i, j: (i, jnp.clip(j - first_step[o], 0, n_blocks[o] - 1)))

    def weight_spec(w_idx):
        cols = [None] * n_steps
        for s, (_, _, wi, w_sec) in enumerate(sections):
            if wi == w_idx:
                for r in range(sb):
                    cols[s * sb + r] = w_sec * sb + r
        held = next(c for c in reversed(cols) if c is not None)
        for jj in range(n_steps):
            if cols[jj] is None:
                cols[jj] = held
            held = cols[jj]
        return pl.BlockSpec((k, tn), lambda i, j: (0, _step_table(j, cols)))

    kern = functools.partial(_norm_linear_kernel, sections=tuple(sections), n_weights=len(ws), sec_blocks=sb,
                             lb_rows=lb_rows, scale=scale)
    return pl.pallas_call(
        kern,
        grid=(t // tm, n_steps),
        in_specs=[
            pl.BlockSpec((tm, k), lambda i, j: (i, 0)),
            pl.BlockSpec((1, k), lambda i, j: (0, 0)),
            pl.BlockSpec((aux.shape[0], tn), lambda i, j: (0, jnp.clip(j - aux_section * sb, 0, sb - 1))),
        ] + [weight_spec(wi) for wi in range(len(ws))],
        out_specs=[out_spec(o) for o in range(len(out_dtypes))],
        out_shape=[jax.ShapeDtypeStruct((t, n_blocks[o] * tn), out_dtypes[o]) for o in range(len(out_dtypes))],
        scratch_shapes=[pltpu.VMEM((tm, k), BF16)],
        compiler_params=_params("parallel", "arbitrary"),
        name=name,
    )(x2, gain.reshape(1, k), aux, *ws)


def _cast_kernel(w_ref, o_ref):
    o_ref[...] = w_ref[...].astype(o_ref.dtype)


def _cast_bf16(w, layer, *, name):
    _, r, c = w.shape
    tc = min(c, 2048)
    tr = min(r, max(8, CAST_BLOCK_ELEMS // tc))
    return pl.pallas_call(
        _cast_kernel,
        grid=(r // tr, c // tc),
        in_specs=[pl.BlockSpec((None, tr, tc), lambda i, j: (layer, i, j))],
        out_specs=pl.BlockSpec((tr, tc), lambda i, j: (i, j)),
        out_shape=jax.ShapeDtypeStruct((r, c), BF16),
        compiler_params=_params("parallel", "parallel"),
        name=name,
    )(w)


def _mixer_out_mlp_kernel(x_ref, a_ref, wo_ref, g_ref, up_ref, down_ref, o_ref, h_scr):
    def mixer_out():
        for rows in _row_chunks(x_ref.shape[0]):
            x1 = x_ref[rows, :] + jnp.dot(a_ref[rows, :], wo_ref[...], preferred_element_type=F32)
            o_ref[rows, :] = x1
            h_scr[rows, :] = _rms_bf16(x1, g_ref[...])

    def mlp_tile():
        for rows in _row_chunks(x_ref.shape[0]):
            u = jnp.dot(h_scr[rows, :], up_ref[...], preferred_element_type=F32)
            a = jnp.square(jnp.maximum(u, 0.0)).astype(BF16)
            o_ref[rows, :] += jnp.dot(a, down_ref[...], preferred_element_type=F32)

    j = pl.program_id(1)
    pl.when(j == 0)(mixer_out)
    pl.when(j > 0)(mlp_tile)


def _mixer_out_mlp(x2, a, w_out, gain, up, down, *, name, tm=512, tf=1024):
    t, d = x2.shape
    ka = a.shape[1]
    f = up.shape[1]
    tm, tf = min(tm, t), min(tf, f)
    return pl.pallas_call(
        _mixer_out_mlp_kernel,
        grid=(t // tm, 1 + f // tf),
        in_specs=[
            pl.BlockSpec((tm, d), lambda i, j: (i, 0)),
            pl.BlockSpec((tm, ka), lambda i, j: (i, 0)),
            pl.BlockSpec((ka, d), lambda i, j: (0, 0), pipeline_mode=pl.Buffered(1)),
            pl.BlockSpec((1, d), lambda i, j: (0, 0)),
            pl.BlockSpec((d, tf), lambda i, j: (0, jnp.maximum(j - 1, 0))),
            pl.BlockSpec((tf, d), lambda i, j: (jnp.maximum(j - 1, 0), 0)),
        ],
        out_specs=pl.BlockSpec((tm, d), lambda i, j: (i, 0)),
        out_shape=jax.ShapeDtypeStruct((t, d), F32),
        scratch_shapes=[pltpu.VMEM((tm, d), BF16)],
        compiler_params=_params("parallel", "arbitrary"),
        name=name,
    )(x2, a, w_out, gain.reshape(1, d), up, down)


def _ple_kernel(x_ref, g_ref, p_ref, gw_ref, pw_ref, o_ref, h_scr, *, tn):
    j = pl.program_id(1)
    cols = pl.ds(pl.multiple_of(j * tn, tn), tn)

    def run(first):
        for rows in _row_chunks(x_ref.shape[0]):
            if first:
                h = _rms_bf16(x_ref[rows, :], g_ref[...])
                h_scr[rows, :] = h
            else:
                h = h_scr[rows, :]
            gate = jax.nn.sigmoid(jnp.dot(h, gw_ref[...], preferred_element_type=F32))
            emb = jnp.dot(p_ref[rows, :].astype(BF16), pw_ref[...], preferred_element_type=F32)
            o_ref[rows, :] = x_ref[rows, cols] + gate * emb

    pl.when(j == 0)(lambda: run(True))
    pl.when(j > 0)(lambda: run(False))


def _ple(x2, gain, p3, layer, gate_w, proj_w, *, name, tm=1024, tn=1024):
    t, d = x2.shape
    pd = p3.shape[2]
    tm, tn = min(tm, t), min(tn, d)
    return pl.pallas_call(
        functools.partial(_ple_kernel, tn=tn),
        grid=(t // tm, d // tn),
        in_specs=[
            pl.BlockSpec((tm, d), lambda i, j: (i, 0)),
            pl.BlockSpec((1, d), lambda i, j: (0, 0)),
            pl.BlockSpec((None, tm, pd), lambda i, j: (layer, i, 0)),
            pl.BlockSpec((d, tn), lambda i, j: (0, j)),
            pl.BlockSpec((pd, tn), lambda i, j: (0, j)),
        ],
        out_specs=pl.BlockSpec((tm, tn), lambda i, j: (i, j)),
        out_shape=jax.ShapeDtypeStruct((t, d), F32),
        scratch_shapes=[pltpu.VMEM((tm, d), BF16)],
        compiler_params=_params("parallel", "arbitrary"),
        name=name,
    )(x2, gain.reshape(1, d), p3, gate_w, proj_w)


def _hgrn_levels(c):
    return [c >> (l + 1) for l in range(int(math.log2(c)))]


def _hgrn_constants(c):
    idx = np.arange(c)
    x = idx[:, None] ^ idx[None, :]
    masks = [((x >= m) & (x < 2 * m) & ((idx[:, None] & m) != 0)).astype(np.float32) for m in _hgrn_levels(c)]
    return np.tril(np.ones((c, c), np.float32)), np.stack(masks, 0)


def _level_log_decay(g_cum, g, m):
    c, w = g_cum.shape
    if m == 1:
        odd = (lax.broadcasted_iota(jnp.int32, (c, w), 0) & 1) != 0
        return jnp.where(odd, g, 0.0)
    if m < 8:
        g3 = g_cum.reshape(c // 8, 8, w)
        sub = lax.broadcasted_iota(jnp.int32, g3.shape, 1)
        if m == 4:
            mid = g3[:, 3:4, :]
        else:
            mid = jnp.where(sub < 4, g3[:, 1:2, :], g3[:, 5:6, :])
        return (-jnp.abs(g3 - mid)).reshape(c, w)
    g3 = g_cum.reshape(c // (2 * m), 2 * m, w)
    mid = g3[:, m - 1:m, :]
    return jnp.concatenate([mid - g3[:, :m, :], g3[:, m:, :] - mid], axis=1).reshape(c, w)


def _pair_rows(lower, upper, m):
    c, w = lower.shape
    if m < 8:
        return jnp.where((lax.broadcasted_iota(jnp.int32, (c, w), 0) & m) != 0, lower, upper)
    lo3 = lower.reshape(c // (2 * m), 2 * m, w)
    up3 = upper.reshape(c // (2 * m), 2 * m, w)
    return jnp.concatenate([up3[:, :m, :], lo3[:, m:, :]], axis=1).reshape(c, w)


def _hgrn_kernel(q_ref, f_ref, v_ref, gate_ref, tri_ref, msk_ref, on_ref, o_ref, st_scr, *, c, hb, nchunks):
    @pl.when(pl.program_id(2) == 0)
    def _():
        st_scr[...] = jnp.zeros_like(st_scr)

    levels = _hgrn_levels(c)
    hw = hb * LANES

    def chunk_decay(ci, worst):
        g = f_ref[0, pl.ds(pl.multiple_of(ci * c, c), c), :]
        halves = jnp.minimum(jnp.sum(g[:c // 2], axis=0, keepdims=True), jnp.sum(g[c // 2:], axis=0, keepdims=True))
        return jnp.minimum(worst, halves)

    worst = lax.fori_loop(0, nchunks, chunk_decay, jnp.zeros((1, hw), F32))
    mild = jnp.min(worst) > -HGRN_SAFE_LOG2_RANGE

    def chunk(ci, mild_decay):
        rows = pl.ds(pl.multiple_of(ci * c, c), c)
        g_all = f_ref[0, rows, :]
        g_hi = g_all.astype(BF16)
        g_lo = (g_all - g_hi.astype(F32)).astype(BF16)
        tri = tri_ref[...]
        g_cum_all = (jnp.dot(tri, g_hi, preferred_element_type=F32)
                     + jnp.dot(tri, g_lo, preferred_element_type=F32))
        k_all = 1.0 - jnp.exp2(g_all)
        g_last = g_cum_all[c - 1:c, :]
        if mild_decay:
            g_mid = g_cum_all[c // 2 - 1:c // 2, :]
            q_mid_all = jnp.exp2(g_cum_all - g_mid)
            k_mid_all = jnp.exp2(g_mid - g_cum_all)
            to_mid_all = jnp.exp2(g_mid)
            mid_to_end_all = jnp.exp2(g_last - g_mid)
            causal = (lax.broadcasted_iota(jnp.int32, (c, c), 0) >= lax.broadcasted_iota(jnp.int32, (c, c), 1))
        else:
            q_dec_all = jnp.exp2(g_cum_all)
            k_dec_all = jnp.exp2(g_last - g_cum_all)
            st_dec_all = jnp.exp2(g_last)
            lvl_dec_all = [jnp.exp2(_level_log_decay(g_cum_all, g_all, m)) for m in levels]
        for h in range(hb):
            hl = slice(h * LANES, (h + 1) * LANES)
            qs = q_ref[0, rows, hl].astype(F32)
            vb = v_ref[0, rows, hl]
            k = k_all[:, hl]
            st = st_scr[h]
            if mild_decay:
                qm = (qs * q_mid_all[:, hl]).astype(BF16)
                km = (k * k_mid_all[:, hl]).astype(BF16)
                a = jnp.where(causal, lax.dot_general(qm, km, _NT, preferred_element_type=F32), 0.0)
                st_mid = st * to_mid_all[:, hl]
                o = (jnp.dot(a.astype(BF16), vb, preferred_element_type=F32)
                     + lax.dot_general(qm, st_mid.astype(BF16), _NT, preferred_element_type=F32))
                st_scr[h] = ((st_mid + lax.dot_general(vb, km, _TN, preferred_element_type=F32))
                             * mid_to_end_all[:, hl])
            else:
                a = jnp.zeros((c, c), F32)
                for l, m in enumerate(levels):
                    x = (_pair_rows(qs, k, m) * lvl_dec_all[l][:, hl]).astype(BF16)
                    a = a + msk_ref[l] * lax.dot_general(x, x, _NT, preferred_element_type=F32)
                o = jnp.dot(a.astype(BF16), vb, preferred_element_type=F32)
                o = o + jnp.sum(qs * k, axis=-1, keepdims=True) * vb.astype(F32)
                qg = (qs * q_dec_all[:, hl]).astype(BF16)
                o = o + lax.dot_general(qg, st.astype(BF16), _NT, preferred_element_type=F32)
                kt = (k * k_dec_all[:, hl]).astype(BF16)
                st_scr[h] = st * st_dec_all[:, hl] + lax.dot_general(vb, kt, _TN, preferred_element_type=F32)
            ms = jnp.mean(o * o, axis=-1, keepdims=True)
            y = o * lax.rsqrt(ms + EPS) * on_ref[...] * gate_ref[0, rows, hl].astype(F32)
            o_ref[0, rows, hl] = y.astype(o_ref.dtype)

    unroll = HGRN_UNROLL if nchunks % HGRN_UNROLL == 0 else 1

    def all_chunks(mild_decay):
        def trip(u, carry):
            for r in range(unroll):
                chunk(u * unroll + r, mild_decay)
            return carry

        lax.fori_loop(0, nchunks // unroll, trip, 0)

    pl.when(mild)(lambda: all_chunks(True))
    pl.when(jnp.logical_not(mild))(lambda: all_chunks(False))


def _hgrn(qig, log_f, onorm, *, name, rows=1024, hb=4):
    b, s, width = log_f.shape
    c = HGRN_CHUNK
    rows = min(rows, s)
    hw = hb * LANES
    nsb = width // hw
    tri, masks = _hgrn_constants(c)
    nl = masks.shape[0]

    def sec(k):
        return pl.BlockSpec((1, rows, hw), lambda bi, hg, r: (bi, r, k * nsb + hg))

    kern = functools.partial(_hgrn_kernel, c=c, hb=hb, nchunks=rows // c)
    return pl.pallas_call(
        kern,
        grid=(b, nsb, s // rows),
        in_specs=[
            sec(0), sec(0), sec(1), sec(2),
            pl.BlockSpec((c, c), lambda bi, hg, r: (0, 0)),
            pl.BlockSpec((nl, c, c), lambda bi, hg, r: (0, 0, 0)),
            pl.BlockSpec((1, LANES), lambda bi, hg, r: (0, 0)),
        ],
        out_specs=pl.BlockSpec((1, rows, hw), lambda bi, hg, r: (bi, r, hg)),
        out_shape=jax.ShapeDtypeStruct((b, s, width), BF16),
        scratch_shapes=[pltpu.VMEM((hb, LANES, LANES), F32)],
        compiler_params=_params("parallel", "parallel", "arbitrary"),
        name=name,
    )(qig, log_f, qig, qig, jnp.asarray(tri, BF16), jnp.asarray(masks, F32), onorm.reshape(1, LANES))


def _attn_kernel(q_ref, k_ref, v_ref, qg_ref, sub_ref, lq1_ref, lk1_ref, lq2_ref, lk2_ref, o_ref,
                 m_scr, l_scr, acc_scr, shift_scr, *, tq, lam_init):
    hw = acc_scr.shape[-1]
    nq = q_ref.shape[1] // tq
    maps = [slice(c * LANES, (c + 1) * LANES) for c in range(2)]

    def key_block(i, carry):
        kb = k_ref[0, pl.ds(pl.multiple_of(i * tq, tq), tq), :].astype(F32)
        sq = [jnp.sum(kb[:, sl] * kb[:, sl], axis=-1, keepdims=True) for sl in maps]
        return tuple(jnp.maximum(carry[c], jnp.max(sq[c], axis=0, keepdims=True)) for c in range(2))

    zero = jnp.zeros((1, 1), F32)
    k_sq = lax.fori_loop(0, nq, key_block, (zero, zero))
    bounds = []
    for c, sl in enumerate(maps):
        q_max = math.sqrt(LANES) * jnp.max(jnp.abs(qg_ref[:, sl]), axis=-1, keepdims=True)
        bounds.append(q_max * jnp.sqrt(k_sq[c]))
        shift_scr[c] = jnp.broadcast_to(bounds[c], (1, LANES))
    bounded = jnp.max(jnp.maximum(bounds[0], bounds[1])) < SAFE_LOG2_SHIFT

    def causal(shape, row0):
        row = lax.broadcasted_iota(jnp.int32, shape, 0) + row0
        return lax.broadcasted_iota(jnp.int32, shape, 1) <= row

    def block(qi, slot, j, masked, running_max, first=False):
        for qrows in _row_chunks(tq):
            nk = qrows.stop if masked else tq
            keys = pl.ds(_aligned(j * tq, tq), nk)
            qsel = pl.ds(_aligned(qi * tq + qrows.start, qrows.stop - qrows.start), qrows.stop - qrows.start)
            vb = v_ref[0, keys, :]
            for c, sl in enumerate(maps):
                s = lax.dot_general(q_ref[0, qsel, sl], k_ref[0, keys, sl], _NT,
                                    preferred_element_type=F32)
                if running_max:
                    if masked:
                        s = jnp.where(causal(s.shape, qrows.start), s, -jnp.inf)
                    m_prev = m_scr[c, qrows, :]
                    m_new = jnp.maximum(m_prev, jnp.max(s, axis=-1, keepdims=True))
                    alpha = jnp.exp2(m_prev - m_new)
                    p = jnp.exp2(s - jnp.tile(m_new, (1, nk // LANES)))
                    l_scr[slot, c, qrows, :] = (alpha * l_scr[slot, c, qrows, :]
                                                + jnp.sum(p, axis=-1, keepdims=True))
                    acc_scr[slot, c, qrows, :] = (jnp.tile(alpha, (1, hw // LANES)) * acc_scr[slot, c, qrows, :]
                                                  + jnp.dot(p.astype(BF16), vb, preferred_element_type=F32))
                    m_scr[c, qrows, :] = m_new
                else:
                    p = jnp.exp2(s - jnp.tile(shift_scr[c], (1, nk // LANES)))
                    if masked:
                        p = jnp.where(causal(p.shape, qrows.start), p, 0.0)
                    lane_sums = p[:, 0:LANES]
                    for g in range(1, nk // LANES):
                        lane_sums = lane_sums + p[:, g * LANES:(g + 1) * LANES]
                    pv = jnp.dot(p.astype(BF16), vb, preferred_element_type=F32)
                    if first:
                        l_scr[slot, c, qrows, :] = lane_sums
                        acc_scr[slot, c, qrows, :] = pv
                    else:
                        l_scr[slot, c, qrows, :] += lane_sums
                        acc_scr[slot, c, qrows, :] += pv

    lam = (jnp.exp(jnp.sum(lq1_ref[...] * lk1_ref[...], axis=-1, keepdims=True))
           - jnp.exp(jnp.sum(lq2_ref[...] * lk2_ref[...], axis=-1, keepdims=True))
           + lam_init)

    def finalize(qi, slot, lanes_summed):
        ls = [l_scr[slot, c] for c in range(2)]
        if not lanes_summed:
            ls = [jnp.sum(l, axis=-1, keepdims=True) for l in ls]
            invs = [jnp.broadcast_to(1.0 / l, (tq, hw)) for l in ls]
        else:
            invs = [jnp.tile(1.0 / l, (1, hw // LANES)) for l in ls]
        o = acc_scr[slot, 0] * invs[0] - lam * (acc_scr[slot, 1] * invs[1])
        ms = jnp.mean(o * o, axis=-1, keepdims=True)
        o_ref[0, pl.ds(_aligned(qi * tq, tq), tq), :] = (o * lax.rsqrt(ms + EPS)
                                           * (sub_ref[...] * (1.0 - lam_init))).astype(o_ref.dtype)

    @pl.when(bounded)
    def _():
        for qi in range(nq):
            slot = qi % 2
            block(qi, slot, qi, True, False, first=True)
            for j in range(qi):
                block(qi, slot, j, False, False)
            finalize(qi, slot, lanes_summed=False)

    @pl.when(jnp.logical_not(bounded))
    def _():
        def query_block(qi, carry):
            m_scr[...] = jnp.full_like(m_scr, -1e30)
            l_scr[0] = jnp.zeros_like(l_scr[0])
            acc_scr[0] = jnp.zeros_like(acc_scr[0])

            def body(j, c2):
                block(qi, 0, j, False, True)
                return c2

            lax.fori_loop(0, qi, body, 0)
            block(qi, 0, qi, True, True)
            finalize(qi, 0, lanes_summed=True)
            return carry

        lax.fori_loop(0, nq, query_block, 0)


def _diff_attention(q, kv, q_gain, subln, lq1, lk1, lq2, lk2, *, lam_init, name, tq=1024):
    b, s, width = q.shape
    hw = 2 * LANES
    nh = width // hw
    tq = min(tq, s)
    vec = pl.BlockSpec((1, LANES), lambda bi, h: (0, 0))
    head_vec = pl.BlockSpec((1, hw), lambda bi, h: (0, 0))
    return pl.pallas_call(
        functools.partial(_attn_kernel, tq=tq, lam_init=lam_init),
        grid=(b, nh),
        in_specs=[
            pl.BlockSpec((1, s, hw), lambda bi, h: (bi, 0, h)),
            pl.BlockSpec((1, s, hw), lambda bi, h: (bi, 0, h)),
            pl.BlockSpec((1, s, hw), lambda bi, h: (bi, 0, nh + h)),
            head_vec, head_vec,
            vec, vec, vec, vec,
        ],
        out_specs=pl.BlockSpec((1, s, hw), lambda bi, h: (bi, 0, h)),
        out_shape=jax.ShapeDtypeStruct((b, s, width), BF16),
        scratch_shapes=[
            pltpu.VMEM((2, tq, LANES), F32),
            pltpu.VMEM((2, 2, tq, LANES), F32),
            pltpu.VMEM((2, 2, tq, hw), F32),
            pltpu.VMEM((2, 1, LANES), F32),
        ],
        compiler_params=_params("parallel", "parallel"),
        name=name,
    )(q, kv, kv, q_gain.reshape(1, hw), subln.reshape(1, hw), lq1.reshape(1, LANES), lk1.reshape(1, LANES),
      lq2.reshape(1, LANES), lk2.reshape(1, LANES))


def kernel(x, p, ln_mix, ln_mlp, ln_ple, a_w_in, a_lb, a_onorm, a_w_out, kv_norm, w_k, w_v, k_norm, b_w_q, q_norm, lam_q1, lam_k1, lam_q2, lam_k2, b_subln, b_w_out, mlp_up, mlp_down, ple_proj, ple_gate):
    b, s, d = x.shape
    depth = ln_mix.shape[0]
    n_a = a_w_in.shape[0]
    t = b * s
    x2 = x.reshape(t, d)
    kv = None
    for i in range(depth):
        if i < n_a:
            j = i
            width = a_w_out.shape[1]
            qig, log_f = _norm_linear(
                x2, ln_mix[i], (_cast_bf16(a_w_in, j, name=f"cast_hgrn_in_{i}"),), a_lb,
                sections=(("silu", 0, 0, 0), ("plain", 0, 0, 2), ("silu", 0, 0, 3), ("log_forget", 1, 0, 1)),
                aux_section=3, sec_width=width, out_dtypes=(BF16, F32), lb_rows=i + 1, name=f"hgrn_in_{i}")
            o = _hgrn(qig.reshape(b, s, 3 * width), log_f.reshape(b, s, width), a_onorm[j], name=f"hgrn_{i}")
            w_out = _cast_bf16(a_w_out, j, name=f"cast_hgrn_out_{i}")
        else:
            j = i - n_a
            width = b_w_q.shape[2]
            q_scale = math.log2(math.e) / math.sqrt(LANES)
            head_gain = jnp.tile(q_norm[j].reshape(1, -1), (1, width // (2 * LANES)))
            (q,) = _norm_linear(x2, ln_mix[i], (_cast_bf16(b_w_q, j, name=f"cast_attn_q_{i}"),), head_gain,
                                sections=(("head_rms", 0, 0, 0),), aux_section=0, sec_width=width,
                                out_dtypes=(BF16,), scale=q_scale, name=f"attn_q_{i}")
            lam_init = 0.8 - 0.6 * math.exp(-0.3 * i)
            o = _diff_attention(q.reshape(b, s, width), kv, q_norm[j] * q_scale, b_subln[j], lam_q1[j], lam_k1[j],
                                lam_q2[j], lam_k2[j], lam_init=lam_init, name=f"attn_{i}")
            w_out = _cast_bf16(b_w_out, j, name=f"cast_attn_out_{i}")
        x2 = _mixer_out_mlp(x2, o.reshape(t, width), w_out, ln_mlp[i],
                            _cast_bf16(mlp_up, i, name=f"cast_mlp_up_{i}"),
                            _cast_bf16(mlp_down, i, name=f"cast_mlp_down_{i}"), name=f"mlp_{i}")
        x2 = _ple(x2, ln_ple[i], p.reshape(depth, t, -1), i, _cast_bf16(ple_gate, i, name=f"cast_ple_gate_{i}"),
                  _cast_bf16(ple_proj, i, name=f"cast_ple_proj_{i}"), name=f"ple_{i}")
        if i == n_a - 1:
            width = w_k.shape[1]
            head_gain = jnp.tile(k_norm.reshape(1, -1), (1, width // (2 * LANES)))
            (kv,) = _norm_linear(x2, kv_norm, (_cast_bf16(w_k[None], 0, name="cast_w_k"),
                                               _cast_bf16(w_v[None], 0, name="cast_w_v")), head_gain,
                                 sections=(("head_rms", 0, 0, 0), ("plain", 0, 1, 0)), aux_section=0,
                                 sec_width=width, out_dtypes=(BF16,), name="shared_kv")
            kv = kv.reshape(b, s, 2 * width)
    return x2.reshape(b, s, d)
```

```python
import functools
import math

import numpy as np
import jax
import jax.numpy as jnp
from jax import lax
from jax.experimental import pallas as pl
from jax.experimental.pallas import tpu as pltpu

EPS = 1e-6
LANES = 128
HGRN_CHUNK = 128
HGRN_UNROLL = 8
HGRN_SAFE_LOG2_RANGE = 100.0
CAST_BLOCK_ELEMS = 2 * 1024 * 1024
ROW_CHUNK = 256
SAFE_LOG2_SHIFT = 60.0
BOUND_SLACK = 1.01
F32 = jnp.float32
BF16 = jnp.bfloat16
VMEM_LIMIT_BYTES = 56 * 1024 * 1024

_NT = (((1,), (1,)), ((), ()))
_TN = (((0,), (0,)), ((), ()))


def _params(*sem, **extra):
    return pltpu.CompilerParams(dimension_semantics=sem, vmem_limit_bytes=VMEM_LIMIT_BYTES, **extra)


def _rms_bf16(x, gain):
    ms = jnp.mean(x * x, axis=-1, keepdims=True)
    return (x * lax.rsqrt(ms + EPS) * gain).astype(BF16)


def _silu(x):
    return x * jax.nn.sigmoid(x)


def _group_rms(y, gain, scale):
    outs = []
    for c in range(y.shape[1] // LANES):
        sl = slice(c * LANES, (c + 1) * LANES)
        yc = y[:, sl]
        ms = jnp.mean(yc * yc, axis=-1, keepdims=True)
        outs.append(yc * lax.rsqrt(ms + EPS) * (gain[:, sl] * scale))
    return jnp.concatenate(outs, axis=1)


def _aligned(start, multiple):
    return start if isinstance(start, int) else pl.multiple_of(start, multiple)


def _row_chunks(tm):
    rc = min(ROW_CHUNK, tm)
    return [slice(r * rc, (r + 1) * rc) for r in range(tm // rc)]


def _norm_linear_kernel(x_ref, g_ref, aux_ref, *rest, sections, n_weights, sec_blocks, lb_rows, scale):
    w_refs, o_refs, h_scr = rest[:n_weights], rest[n_weights:-1], rest[-1]
    j = pl.program_id(1)

    def epilogue(name, acc):
        if name == "silu":
            return _silu(acc)
        if name == "plain":
            return acc
        if name == "head_rms":
            return _group_rms(acc, aux_ref[...], scale)
        if name == "log_forget":
            a = aux_ref[...]
            e = jnp.exp(a - jnp.max(a, axis=0, keepdims=True))
            lb = jnp.sum(e[:lb_rows], axis=0, keepdims=True) / jnp.sum(e, axis=0, keepdims=True)
            return jnp.log2(lb + (1.0 - lb) * jax.nn.sigmoid(acc))
        raise ValueError(name)

    def run(first, name, o_ref, w_ref):
        for rows in _row_chunks(x_ref.shape[0]):
            if first:
                h = _rms_bf16(x_ref[rows, :], g_ref[...])
                h_scr[rows, :] = h
            else:
                h = h_scr[rows, :]
            acc = jnp.dot(h, w_ref[...], preferred_element_type=F32)
            o_ref[rows, :] = epilogue(name, acc).astype(o_ref.dtype)

    for s, (name, out_idx, w_idx, _) in enumerate(sections):
        lo, hi = s * sec_blocks, (s + 1) * sec_blocks
        if s == 0:
            pl.when(j == 0)(functools.partial(run, True, name, o_refs[out_idx], w_refs[w_idx]))
            lo = 1
        if lo < hi:
            pl.when(jnp.logical_and(j >= lo, j < hi))(
                functools.partial(run, False, name, o_refs[out_idx], w_refs[w_idx]))


def _step_table(j, table):
    out = table[0]
    for jj in range(1, len(table)):
        if table[jj] != table[jj - 1]:
            out = jnp.where(j >= jj, table[jj], out)
    return out


def _norm_linear(x2, gain, ws, aux, *, sections, aux_section, sec_width, out_dtypes, name, lb_rows=1, scale=1.0,
                 tm=1024, tn=1024):
    t, k = x2.shape
    tm, tn = min(tm, t), min(tn, sec_width)
    sb = sec_width // tn
    n_steps = len(sections) * sb
    first_step = [None] * len(out_dtypes)
    n_blocks = [0] * len(out_dtypes)
    for s, (_, o, _, _) in enumerate(sections):
        if first_step[o] is None:
            first_step[o] = s * sb
        assert s * sb == first_step[o] + n_blocks[o], "an output's sections must be consecutive"
        n_blocks[o] += sb

    def out_spec(o):
        return pl.BlockSpec((tm, tn), lambda i, j: (i, jnp.clip(j - first_step[o], 0, n_blocks[o] - 1)))

    def weight_spec(w_idx):
        cols = [None] * n_steps
        for s, (_, _, wi, w_sec) in enumerate(sections):
            if wi == w_idx:
                for r in range(sb):
                    cols[s * sb + r] = w_sec * sb + r
        held = next(c for c in reversed(cols) if c is not None)
        for jj in range(n_steps):
            if cols[jj] is None:
                cols[jj] = held
            held = cols[jj]
        return pl.BlockSpec((k, tn), lambda i, j: (0, _step_table(j, cols)))

    kern = functools.partial(_norm_linear_kernel, sections=tuple(sections), n_weights=len(ws), sec_blocks=sb,
                             lb_rows=lb_rows, scale=scale)
    return pl.pallas_call(
        kern,
        grid=(t // tm, n_steps),
        in_specs=[
            pl.BlockSpec((tm, k), lambda i, j: (i, 0)),
            pl.BlockSpec((1, k), lambda i, j: (0, 0)),
            pl.BlockSpec((aux.shape[0], tn), lambda i, j: (0, jnp.clip(j - aux_section * sb, 0, sb - 1))),
        ] + [weight_spec(wi) for wi in range(len(ws))],
        out_specs=[out_spec(o) for o in range(len(out_dtypes))],
        out_shape=[jax.ShapeDtypeStruct((t, n_blocks[o] * tn), out_dtypes[o]) for o in range(len(out_dtypes))],
        scratch_shapes=[pltpu.VMEM((tm, k), BF16)],
        compiler_params=_params("parallel", "arbitrary"),
        name=name,
    )(x2, gain.reshape(1, k), aux, *ws)


def _cast_kernel(w_ref, o_ref):
    o_ref[...] = w_ref[...].astype(o_ref.dtype)


def _cast_bf16(w, layer, *, name):
    _, r, c = w.shape
    tc = min(c, 2048)
    tr = min(r, max(8, CAST_BLOCK_ELEMS // tc))
    return pl.pallas_call(
        _cast_kernel,
        grid=(r // tr, c // tc),
        in_specs=[pl.BlockSpec((None, tr, tc), lambda i, j: (layer, i, j))],
        out_specs=pl.BlockSpec((tr, tc), lambda i, j: (i, j)),
        out_shape=jax.ShapeDtypeStruct((r, c), BF16),
        compiler_params=_params("parallel", "parallel"),
        name=name,
    )(w)


def _mixer_out_mlp_kernel(x_ref, a_ref, wo_ref, g_ref, up_ref, down_ref, o_ref, h_scr):
    def mixer_out():
        for rows in _row_chunks(x_ref.shape[0]):
            x1 = x_ref[rows, :] + jnp.dot(a_ref[rows, :], wo_ref[...], preferred_element_type=F32)
            o_ref[rows, :] = x1
            h_scr[rows, :] = _rms_bf16(x1, g_ref[...])

    def mlp_tile():
        for rows in _row_chunks(x_ref.shape[0]):
            u = jnp.dot(h_scr[rows, :], up_ref[...], preferred_element_type=F32)
            a = jnp.square(jnp.maximum(u, 0.0)).astype(BF16)
            o_ref[rows, :] += jnp.dot(a, down_ref[...], preferred_element_type=F32)

    j = pl.program_id(1)
    pl.when(j == 0)(mixer_out)
    pl.when(j > 0)(mlp_tile)


def _mixer_out_mlp(x2, a, w_out, gain, up, down, *, name, tm=512, tf=1024):
    t, d = x2.shape
    ka = a.shape[1]
    f = up.shape[1]
    tm, tf = min(tm, t), min(tf, f)
    return pl.pallas_call(
        _mixer_out_mlp_kernel,
        grid=(t // tm, 1 + f // tf),
        in_specs=[
            pl.BlockSpec((tm, d), lambda i, j: (i, 0)),
            pl.BlockSpec((tm, ka), lambda i, j: (i, 0)),
            pl.BlockSpec((ka, d), lambda i, j: (0, 0), pipeline_mode=pl.Buffered(1)),
            pl.BlockSpec((1, d), lambda i, j: (0, 0)),
            pl.BlockSpec((d, tf), lambda i, j: (0, jnp.maximum(j - 1, 0))),
            pl.BlockSpec((tf, d), lambda i, j: (jnp.maximum(j - 1, 0), 0)),
        ],
        out_specs=pl.BlockSpec((tm, d), lambda i, j: (i, 0)),
        out_shape=jax.ShapeDtypeStruct((t, d), F32),
        scratch_shapes=[pltpu.VMEM((tm, d), BF16)],
        compiler_params=_params("parallel", "arbitrary"),
        name=name,
    )(x2, a, w_out, gain.reshape(1, d), up, down)


def _ple_kernel(x_ref, g_ref, p_ref, gw_ref, pw_ref, o_ref, h_scr, *, tn):
    j = pl.program_id(1)
    cols = pl.ds(pl.multiple_of(j * tn, tn), tn)

    def run(first):
        for rows in _row_chunks(x_ref.shape[0]):
            if first:
                h = _rms_bf16(x_ref[rows, :], g_ref[...])
                h_scr[rows, :] = h
            else:
                h = h_scr[rows, :]
            gate = jax.nn.sigmoid(jnp.dot(h, gw_ref[...], preferred_element_type=F32))
            emb = jnp.dot(p_ref[rows, :].astype(BF16), pw_ref[...], preferred_element_type=F32)
            o_ref[rows, :] = x_ref[rows, cols] + gate * emb

    pl.when(j == 0)(lambda: run(True))
    pl.when(j > 0)(lambda: run(False))


def _ple(x2, gain, p3, layer, gate_w, proj_w, *, name, tm=1024, tn=1024):
    t, d = x2.shape
    pd = p3.shape[2]
    tm, tn = min(tm, t), min(tn, d)
    return pl.pallas_call(
        functools.partial(_ple_kernel, tn=tn),
        grid=(t // tm, d // tn),
        in_specs=[
            pl.BlockSpec((tm, d), lambda i, j: (i, 0)),
            pl.BlockSpec((1, d), lambda i, j: (0, 0)),
            pl.BlockSpec((None, tm, pd), lambda i, j: (layer, i, 0)),
            pl.BlockSpec((d, tn), lambda i, j: (0, j)),
            pl.BlockSpec((pd, tn), lambda i, j: (0, j)),
        ],
        out_specs=pl.BlockSpec((tm, tn), lambda i, j: (i, j)),
        out_shape=jax.ShapeDtypeStruct((t, d), F32),
        scratch_shapes=[pltpu.VMEM((tm, d), BF16)],
        compiler_params=_params("parallel", "arbitrary"),
        name=name,
    )(x2, gain.reshape(1, d), p3, gate_w, proj_w)


def _hgrn_levels(c):
    return [c >> (l + 1) for l in range(int(math.log2(c)))]


def _hgrn_constants(c):
    idx = np.arange(c)
    x = idx[:, None] ^ idx[None, :]
    masks = [((x >= m) & (x < 2 * m) & ((idx[:, None] & m) != 0)).astype(np.float32) for m in _hgrn_levels(c)]
    return np.tril(np.ones((c, c), np.float32)), np.stack(masks, 0)


def _level_log_decay(g_cum, g, m):
    c, w = g_cum.shape
    if m == 1:
        odd = (lax.broadcasted_iota(jnp.int32, (c, w), 0) & 1) != 0
        return jnp.where(odd, g, 0.0)
    if m < 8:
        g3 = g_cum.reshape(c // 8, 8, w)
        sub = lax.broadcasted_iota(jnp.int32, g3.shape, 1)
        if m == 4:
            mid = g3[:, 3:4, :]
        else:
            mid = jnp.where(sub < 4, g3[:, 1:2, :], g3[:, 5:6, :])
        return (-jnp.abs(g3 - mid)).reshape(c, w)
    g3 = g_cum.reshape(c // (2 * m), 2 * m, w)
    mid = g3[:, m - 1:m, :]
    return jnp.concatenate([mid - g3[:, :m, :], g3[:, m:, :] - mid], axis=1).reshape(c, w)


def _pair_rows(lower, upper, m):
    c, w = lower.shape
    if m < 8:
        return jnp.where((lax.broadcasted_iota(jnp.int32, (c, w), 0) & m) != 0, lower, upper)
    lo3 = lower.reshape(c // (2 * m), 2 * m, w)
    up3 = upper.reshape(c // (2 * m), 2 * m, w)
    return jnp.concatenate([up3[:, :m, :], lo3[:, m:, :]], axis=1).reshape(c, w)


def _hgrn_kernel(q_ref, f_ref, v_ref, gate_ref, tri_ref, msk_ref, on_ref, o_ref, st_scr, *, c, hb, nchunks):
    @pl.when(pl.program_id(2) == 0)
    def _():
        st_scr[...] = jnp.zeros_like(st_scr)

    levels = _hgrn_levels(c)
    hw = hb * LANES

    def chunk_decay(ci, worst):
        g = f_ref[0, pl.ds(pl.multiple_of(ci * c, c), c), :]
        halves = jnp.minimum(jnp.sum(g[:c // 2], axis=0, keepdims=True), jnp.sum(g[c // 2:], axis=0, keepdims=True))
        return jnp.minimum(worst, halves)

    worst = lax.fori_loop(0, nchunks, chunk_decay, jnp.zeros((1, hw), F32))
    mild = jnp.min(worst) > -HGRN_SAFE_LOG2_RANGE

    def chunk(ci, mild_decay):
        rows = pl.ds(pl.multiple_of(ci * c, c), c)
        g_all = f_ref[0, rows, :]
        g_hi = g_all.astype(BF16)
        g_lo = (g_all - g_hi.astype(F32)).astype(BF16)
        tri = tri_ref[...]
        g_cum_all = (jnp.dot(tri, g_hi, preferred_element_type=F32)
                     + jnp.dot(tri, g_lo, preferred_element_type=F32))
        k_all = 1.0 - jnp.exp2(g_all)
        g_last = g_cum_all[c - 1:c, :]
        if mild_decay:
            g_mid = g_cum_all[c // 2 - 1:c // 2, :]
            q_mid_all = jnp.exp2(g_cum_all - g_mid)
            k_mid_all = jnp.exp2(g_mid - g_cum_all)
            to_mid_all = jnp.exp2(g_mid)
            mid_to_end_all = jnp.exp2(g_last - g_mid)
            causal = (lax.broadcasted_iota(jnp.int32, (c, c), 0) >= lax.broadcasted_iota(jnp.int32, (c, c), 1))
        else:
            q_dec_all = jnp.exp2(g_cum_all)
            k_dec_all = jnp.exp2(g_last - g_cum_all)
            st_dec_all = jnp.exp2(g_last)
            lvl_dec_all = [jnp.exp2(_level_log_decay(g_cum_all, g_all, m)) for m in levels]
        for h in range(hb):
            hl = slice(h * LANES, (h + 1) * LANES)
            qs = q_ref[0, rows, hl].astype(F32)
            vb = v_ref[0, rows, hl]
            k = k_all[:, hl]
            st = st_scr[h]
            if mild_decay:
                qm = (qs * q_mid_all[:, hl]).astype(BF16)
                km = (k * k_mid_all[:, hl]).astype(BF16)
                a = jnp.where(causal, lax.dot_general(qm, km, _NT, preferred_element_type=F32), 0.0)
                st_mid = st * to_mid_all[:, hl]
                o = (jnp.dot(a.astype(BF16), vb, preferred_element_type=F32)
                     + lax.dot_general(qm, st_mid.astype(BF16), _NT, preferred_element_type=F32))
                st_scr[h] = ((st_mid + lax.dot_general(vb, km, _TN, preferred_element_type=F32))
                             * mid_to_end_all[:, hl])
            else:
                a = jnp.zeros((c, c), F32)
                for l, m in enumerate(levels):
                    x = (_pair_rows(qs, k, m) * lvl_dec_all[l][:, hl]).astype(BF16)
                    a = a + msk_ref[l] * lax.dot_general(x, x, _NT, preferred_element_type=F32)
                o = jnp.dot(a.astype(BF16), vb, preferred_element_type=F32)
                o = o + jnp.sum(qs * k, axis=-1, keepdims=True) * vb.astype(F32)
                qg = (qs * q_dec_all[:, hl]).astype(BF16)
                o = o + lax.dot_general(qg, st.astype(BF16), _NT, preferred_element_type=F32)
                kt = (k * k_dec_all[:, hl]).astype(BF16)
                st_scr[h] = st * st_dec_all[:, hl] + lax.dot_general(vb, kt, _TN, preferred_element_type=F32)
            ms = jnp.mean(o * o, axis=-1, keepdims=True)
            y = o * lax.rsqrt(ms + EPS) * on_ref[...] * gate_ref[0, rows, hl].astype(F32)
            o_ref[0, rows, hl] = y.astype(o_ref.dtype)

    unroll = HGRN_UNROLL if nchunks % HGRN_UNROLL == 0 else 1

    def all_chunks(mild_decay):
        def trip(u, carry):
            for r in range(unroll):
                chunk(u * unroll + r, mild_decay)
            return carry

        lax.fori_loop(0, nchunks // unroll, trip, 0)

    pl.when(mild)(lambda: all_chunks(True))
    pl.when(jnp.logical_not(mild))(lambda: all_chunks(False))


def _hgrn(qig, log_f, onorm, *, name, rows=1024, hb=4):
    b, s, width = log_f.shape
    c = HGRN_CHUNK
    rows = min(rows, s)
    hw = hb * LANES
    nsb = width // hw
    tri, masks = _hgrn_constants(c)
    nl = masks.shape[0]

    def sec(k):
        return pl.BlockSpec((1, rows, hw), lambda bi, hg, r: (bi, r, k * nsb + hg))

    kern = functools.partial(_hgrn_kernel, c=c, hb=hb, nchunks=rows // c)
    return pl.pallas_call(
        kern,
        grid=(b, nsb, s // rows),
        in_specs=[
            sec(0), sec(0), sec(1), sec(2),
            pl.BlockSpec((c, c), lambda bi, hg, r: (0, 0)),
            pl.BlockSpec((nl, c, c), lambda bi, hg, r: (0, 0, 0)),
            pl.BlockSpec((1, LANES), lambda bi, hg, r: (0, 0)),
        ],
        out_specs=pl.BlockSpec((1, rows, hw), lambda bi, hg, r: (bi, r, hg)),
        out_shape=jax.ShapeDtypeStruct((b, s, width), BF16),
        scratch_shapes=[pltpu.VMEM((hb, LANES, LANES), F32)],
        compiler_params=_params("parallel", "parallel", "arbitrary"),
        name=name,
    )(qig, log_f, qig, qig, jnp.asarray(tri, BF16), jnp.asarray(masks, F32), onorm.reshape(1, LANES))


def _attn_kernel(q_ref, k_ref, v_ref, qg_ref, kg_ref, sub_ref, lq1_ref, lk1_ref, lq2_ref, lk2_ref, o_ref,
                 m_scr, l_scr, acc_scr, shift_scr, *, tq, lam_init):
    hw = acc_scr.shape[-1]
    nq = q_ref.shape[1] // tq
    maps = [slice(c * LANES, (c + 1) * LANES) for c in range(2)]

    bounds = []
    for c, sl in enumerate(maps):
        q_max = jnp.max(jnp.abs(qg_ref[:, sl]), axis=-1, keepdims=True)
        k_max = jnp.max(jnp.abs(kg_ref[:, sl]), axis=-1, keepdims=True)
        bounds.append(BOUND_SLACK * LANES * q_max * k_max)
        shift_scr[c] = jnp.broadcast_to(bounds[c], (1, LANES))
    bounded = jnp.max(jnp.maximum(bounds[0], bounds[1])) < SAFE_LOG2_SHIFT

    def causal(shape, row0):
        row = lax.broadcasted_iota(jnp.int32, shape, 0) + row0
        return lax.broadcasted_iota(jnp.int32, shape, 1) <= row

    def block(qi, slot, j, masked, running_max, first=False):
        for qrows in _row_chunks(tq):
            nk = qrows.stop if masked else tq
            keys = pl.ds(_aligned(j * tq, tq), nk)
            qsel = pl.ds(_aligned(qi * tq + qrows.start, qrows.stop - qrows.start), qrows.stop - qrows.start)
            vb = v_ref[0, keys, :]
            for c, sl in enumerate(maps):
                s = lax.dot_general(q_ref[0, qsel, sl], k_ref[0, keys, sl], _NT,
                                    preferred_element_type=F32)
                if running_max:
                    if masked:
                        s = jnp.where(causal(s.shape, qrows.start), s, -jnp.inf)
                    m_prev = m_scr[c, qrows, :]
                    m_new = jnp.maximum(m_prev, jnp.max(s, axis=-1, keepdims=True))
                    alpha = jnp.exp2(m_prev - m_new)
                    p = jnp.exp2(s - jnp.tile(m_new, (1, nk // LANES)))
                    l_scr[slot, c, qrows, :] = (alpha * l_scr[slot, c, qrows, :]
                                                + jnp.sum(p, axis=-1, keepdims=True))
                    acc_scr[slot, c, qrows, :] = (jnp.tile(alpha, (1, hw // LANES)) * acc_scr[slot, c, qrows, :]
                                                  + jnp.dot(p.astype(BF16), vb, preferred_element_type=F32))
                    m_scr[c, qrows, :] = m_new
                else:
                    p = jnp.exp2(s - jnp.tile(shift_scr[c], (1, nk // LANES)))
                    if masked:
                        p = jnp.where(causal(p.shape, qrows.start), p, 0.0)
                    lane_sums = p[:, 0:LANES]
                    for g in range(1, nk // LANES):
                        lane_sums = lane_sums + p[:, g * LANES:(g + 1) * LANES]
                    pv = jnp.dot(p.astype(BF16), vb, preferred_element_type=F32)
                    if first:
                        l_scr[slot, c, qrows, :] = lane_sums
                        acc_scr[slot, c, qrows, :] = pv
                    else:
                        l_scr[slot, c, qrows, :] += lane_sums
                        acc_scr[slot, c, qrows, :] += pv

    lam = (jnp.exp(jnp.sum(lq1_ref[...] * lk1_ref[...], axis=-1, keepdims=True))
           - jnp.exp(jnp.sum(lq2_ref[...] * lk2_ref[...], axis=-1, keepdims=True))
           + lam_init)

    def finalize(qi, slot, lanes_summed):
        ls = [l_scr[slot, c] for c in range(2)]
        if not lanes_summed:
            ls = [jnp.sum(l, axis=-1, keepdims=True) for l in ls]
            invs = [jnp.broadcast_to(1.0 / l, (tq, hw)) for l in ls]
        else:
            invs = [jnp.tile(1.0 / l, (1, hw // LANES)) for l in ls]
        o = acc_scr[slot, 0] * invs[0] - lam * (acc_scr[slot, 1] * invs[1])
        ms = jnp.mean(o * o, axis=-1, keepdims=True)
        o_ref[0, pl.ds(_aligned(qi * tq, tq), tq), :] = (o * lax.rsqrt(ms + EPS)
                                           * (sub_ref[...] * (1.0 - lam_init))).astype(o_ref.dtype)

    @pl.when(bounded)
    def _():
        for qi in range(nq):
            slot = qi % 2
            block(qi, slot, qi, True, False, first=True)
            for j in range(qi):
                block(qi, slot, j, False, False)
            finalize(qi, slot, lanes_summed=False)

    @pl.when(jnp.logical_not(bounded))
    def _():
        def query_block(qi, carry):
            m_scr[...] = jnp.full_like(m_scr, -1e30)
            l_scr[0] = jnp.zeros_like(l_scr[0])
            acc_scr[0] = jnp.zeros_like(acc_scr[0])

            def body(j, c2):
                block(qi, 0, j, False, True)
                return c2

            lax.fori_loop(0, qi, body, 0)
            block(qi, 0, qi, True, True)
            finalize(qi, 0, lanes_summed=True)
            return carry

        lax.fori_loop(0, nq, query_block, 0)


def _diff_attention(q, kv, q_gain, k_gain, subln, lq1, lk1, lq2, lk2, *, lam_init, name, tq=1024):
    b, s, width = q.shape
    hw = 2 * LANES
    nh = width // hw
    tq = min(tq, s)
    vec = pl.BlockSpec((1, LANES), lambda bi, h: (0, 0))
    head_vec = pl.BlockSpec((1, hw), lambda bi, h: (0, 0))
    return pl.pallas_call(
        functools.partial(_attn_kernel, tq=tq, lam_init=lam_init),
        grid=(b, nh),
        in_specs=[
            pl.BlockSpec((1, s, hw), lambda bi, h: (bi, 0, h)),
            pl.BlockSpec((1, s, hw), lambda bi, h: (bi, 0, h)),
            pl.BlockSpec((1, s, hw), lambda bi, h: (bi, 0, nh + h)),
            head_vec, head_vec, head_vec,
            vec, vec, vec, vec,
        ],
        out_specs=pl.BlockSpec((1, s, hw), lambda bi, h: (bi, 0, h)),
        out_shape=jax.ShapeDtypeStruct((b, s, width), BF16),
        scratch_shapes=[
            pltpu.VMEM((2, tq, LANES), F32),
            pltpu.VMEM((2, 2, tq, LANES), F32),
            pltpu.VMEM((2, 2, tq, hw), F32),
            pltpu.VMEM((2, 1, LANES), F32),
        ],
        compiler_params=_params("parallel", "parallel"),
        name=name,
    )(q, kv, kv, q_gain.reshape(1, hw), k_gain.reshape(1, hw), subln.reshape(1, hw), lq1.reshape(1, LANES),
      lk1.reshape(1, LANES),
      lq2.reshape(1, LANES), lk2.reshape(1, LANES))


def kernel(x, p, ln_mix, ln_mlp, ln_ple, a_w_in, a_lb, a_onorm, a_w_out, kv_norm, w_k, w_v, k_norm, b_w_q, q_norm, lam_q1, lam_k1, lam_q2, lam_k2, b_subln, b_w_out, mlp_up, mlp_down, ple_proj, ple_gate):
    b, s, d = x.shape
    depth = ln_mix.shape[0]
    n_a = a_w_in.shape[0]
    t = b * s
    x2 = x.reshape(t, d)
    kv = None
    for i in range(depth):
        if i < n_a:
            j = i
            width = a_w_out.shape[1]
            qig, log_f = _norm_linear(
                x2, ln_mix[i], (_cast_bf16(a_w_in, j, name=f"cast_hgrn_in_{i}"),), a_lb,
                sections=(("silu", 0, 0, 0), ("plain", 0, 0, 2), ("silu", 0, 0, 3), ("log_forget", 1, 0, 1)),
                aux_section=3, sec_width=width, out_dtypes=(BF16, F32), lb_rows=i + 1, name=f"hgrn_in_{i}")
            o = _hgrn(qig.reshape(b, s, 3 * width), log_f.reshape(b, s, width), a_onorm[j], name=f"hgrn_{i}")
            w_out = _cast_bf16(a_w_out, j, name=f"cast_hgrn_out_{i}")
        else:
            j = i - n_a
            width = b_w_q.shape[2]
            q_scale = math.log2(math.e) / math.sqrt(LANES)
            head_gain = jnp.tile(q_norm[j].reshape(1, -1), (1, width // (2 * LANES)))
            (q,) = _norm_linear(x2, ln_mix[i], (_cast_bf16(b_w_q, j, name=f"cast_attn_q_{i}"),), head_gain,
                                sections=(("head_rms", 0, 0, 0),), aux_section=0, sec_width=width,
                                out_dtypes=(BF16,), scale=q_scale, name=f"attn_q_{i}")
            lam_init = 0.8 - 0.6 * math.exp(-0.3 * i)
            o = _diff_attention(q.reshape(b, s, width), kv, q_norm[j] * q_scale, k_norm, b_subln[j], lam_q1[j],
                                lam_k1[j], lam_q2[j], lam_k2[j], lam_init=lam_init, name=f"attn_{i}")
            w_out = _cast_bf16(b_w_out, j, name=f"cast_attn_out_{i}")
        x2 = _mixer_out_mlp(x2, o.reshape(t, width), w_out, ln_mlp[i],
                            _cast_bf16(mlp_up, i, name=f"cast_mlp_up_{i}"),
                            _cast_bf16(mlp_down, i, name=f"cast_mlp_down_{i}"), name=f"mlp_{i}")
        x2 = _ple(x2, ln_ple[i], p.reshape(depth, t, -1), i, _cast_bf16(ple_gate, i, name=f"cast_ple_gate_{i}"),
                  _cast_bf16(ple_proj, i, name=f"cast_ple_proj_{i}"), name=f"ple_{i}")
        if i == n_a - 1:
            width = w_k.shape[1]
            head_gain = jnp.tile(k_norm.reshape(1, -1), (1, width // (2 * LANES)))
            (kv,) = _norm_linear(x2, kv_norm, (_cast_bf16(w_k[None], 0, name="cast_w_k"),
                                               _cast_bf16(w_v[None], 0, name="cast_w_v")), head_gain,
                                 sections=(("head_rms", 0, 0, 0), ("plain", 0, 1, 0)), aux_section=0,
                                 sec_width=width, out_dtypes=(BF16,), name="shared_kv")
            kv = kv.reshape(b, s, 2 * width)
    return x2.reshape(b, s, d)
```

```python
import functools
import math

import numpy as np
import jax
import jax.numpy as jnp
from jax import lax
from jax.experimental import pallas as pl
from jax.experimental.pallas import tpu as pltpu

EPS = 1e-6
LANES = 128
HGRN_CHUNK = 128
HGRN_UNROLL = 8
HGRN_SAFE_LOG2_RANGE = 100.0
CAST_BLOCK_ELEMS = 2 * 1024 * 1024
ROW_CHUNK = 256
SAFE_LOG2_SHIFT = 60.0
BOUND_SLACK = 1.01
BF16_TILE_ROWS = 16
F32 = jnp.float32
BF16 = jnp.bfloat16
VMEM_LIMIT_BYTES = 56 * 1024 * 1024

_NT = (((1,), (1,)), ((), ()))
_TN = (((0,), (0,)), ((), ()))


def _params(*sem, **extra):
    return pltpu.CompilerParams(dimension_semantics=sem, vmem_limit_bytes=VMEM_LIMIT_BYTES, **extra)


def _rms_bf16(x, gain):
    ms = jnp.mean(x * x, axis=-1, keepdims=True)
    return (x * lax.rsqrt(ms + EPS) * gain).astype(BF16)


def _silu(x):
    return x * jax.nn.sigmoid(x)


def _group_rms(y, gain, scale):
    outs = []
    for c in range(y.shape[1] // LANES):
        sl = slice(c * LANES, (c + 1) * LANES)
        yc = y[:, sl]
        ms = jnp.mean(yc * yc, axis=-1, keepdims=True)
        outs.append(yc * lax.rsqrt(ms + EPS) * (gain[:, sl] * scale))
    return jnp.concatenate(outs, axis=1)


def _aligned(start, multiple):
    return start if isinstance(start, int) else pl.multiple_of(start, multiple)


def _row_chunks(tm):
    rc = min(ROW_CHUNK, tm)
    return [slice(r * rc, (r + 1) * rc) for r in range(tm // rc)]


def _side_cast_plan(side, n_steps):
    plans = []
    for w, _ in side:
        r = w.shape[1]
        nb = next(n for n in range(min(n_steps, r // BF16_TILE_ROWS), 0, -1)
                  if r % n == 0 and (r // n) % BF16_TILE_ROWS == 0)
        plans.append((r // nb, nb))
    return plans


def _side_cast_specs(side, plans, step_of):
    in_specs, out_specs, out_shapes = [], [], []
    for (w, layer), (br, nb) in zip(side, plans):
        _, r, c = w.shape
        in_specs.append(pl.BlockSpec((None, br, c),
                                     lambda *g, layer=layer, nb=nb: (layer, jnp.minimum(step_of(*g), nb - 1), 0)))
        out_specs.append(pl.BlockSpec((br, c), lambda *g, nb=nb: (jnp.minimum(step_of(*g), nb - 1), 0)))
        out_shapes.append(jax.ShapeDtypeStruct((r, c), BF16))
    return in_specs, out_specs, out_shapes


def _run_side_casts(in_refs, out_refs):
    for i_ref, o_ref in zip(in_refs, out_refs):
        o_ref[...] = i_ref[...].astype(o_ref.dtype)


def _norm_linear_kernel(x_ref, g_ref, aux_ref, *rest, sections, n_weights, n_side, sec_blocks, lb_rows, scale):
    n_out = len(rest) - 1 - n_weights - 2 * n_side
    w_refs, side_in = rest[:n_weights], rest[n_weights:n_weights + n_side]
    o_refs = rest[n_weights + n_side:n_weights + n_side + n_out]
    side_out, h_scr = rest[n_weights + n_side + n_out:-1], rest[-1]
    j = pl.program_id(1)

    def epilogue(name, acc):
        if name == "silu":
            return _silu(acc)
        if name == "plain":
            return acc
        if name == "head_rms":
            return _group_rms(acc, aux_ref[...], scale)
        if name == "log_forget":
            a = aux_ref[...]
            e = jnp.exp(a - jnp.max(a, axis=0, keepdims=True))
            lb = jnp.sum(e[:lb_rows], axis=0, keepdims=True) / jnp.sum(e, axis=0, keepdims=True)
            return jnp.log2(lb + (1.0 - lb) * jax.nn.sigmoid(acc))
        raise ValueError(name)

    def run(first, name, o_ref, w_ref):
        _run_side_casts(side_in, side_out)
        for rows in _row_chunks(x_ref.shape[0]):
            if first:
                h = _rms_bf16(x_ref[rows, :], g_ref[...])
                h_scr[rows, :] = h
            else:
                h = h_scr[rows, :]
            acc = jnp.dot(h, w_ref[...], preferred_element_type=F32)
            o_ref[rows, :] = epilogue(name, acc).astype(o_ref.dtype)

    for s, (name, out_idx, w_idx, _) in enumerate(sections):
        lo, hi = s * sec_blocks, (s + 1) * sec_blocks
        if s == 0:
            pl.when(j == 0)(functools.partial(run, True, name, o_refs[out_idx], w_refs[w_idx]))
            lo = 1
        if lo < hi:
            pl.when(jnp.logical_and(j >= lo, j < hi))(
                functools.partial(run, False, name, o_refs[out_idx], w_refs[w_idx]))


def _step_table(j, table):
    out = table[0]
    for jj in range(1, len(table)):
        if table[jj] != table[jj - 1]:
            out = jnp.where(j >= jj, table[jj], out)
    return out


def _norm_linear(x2, gain, ws, aux, *, sections, aux_section, sec_width, out_dtypes, name, lb_rows=1, scale=1.0,
                 side=(), tm=1024, tn=1024):
    t, k = x2.shape
    tm, tn = min(tm, t), min(tn, sec_width)
    sb = sec_width // tn
    n_steps = len(sections) * sb
    plans = _side_cast_plan(side, (t // tm) * n_steps)
    side_in, side_out, side_shapes = _side_cast_specs(side, plans, lambda i, j: i * n_steps + j)
    first_step = [None] * len(out_dtypes)
    n_blocks = [0] * len(out_dtypes)
    for s, (_, o, _, _) in enumerate(sections):
        if first_step[o] is None:
            first_step[o] = s * sb
        assert s * sb == first_step[o] + n_blocks[o], "an output's sections must be consecutive"
        n_blocks[o] += sb

    def out_spec(o):
        return pl.BlockSpec((tm, tn), lambda i, j: (i, jnp.clip(j - first_step[o], 0, n_blocks[o] - 1)))

    def weight_spec(w_idx):
        cols = [None] * n_steps
        for s, (_, _, wi, w_sec) in enumerate(sections):
            if wi == w_idx:
                for r in range(sb):
                    cols[s * sb + r] = w_sec * sb + r
        held = next(c for c in reversed(cols) if c is not None)
        for jj in range(n_steps):
            if cols[jj] is None:
                cols[jj] = held
            held = cols[jj]
        return pl.BlockSpec((k, tn), lambda i, j: (0, _step_table(j, cols)))

    kern = functools.partial(_norm_linear_kernel, sections=tuple(sections), n_weights=len(ws), n_side=len(side),
                             sec_blocks=sb, lb_rows=lb_rows, scale=scale)
    return pl.pallas_call(
        kern,
        grid=(t // tm, n_steps),
        in_specs=[
            pl.BlockSpec((tm, k), lambda i, j: (i, 0)),
            pl.BlockSpec((1, k), lambda i, j: (0, 0)),
            pl.BlockSpec((aux.shape[0], tn), lambda i, j: (0, jnp.clip(j - aux_section * sb, 0, sb - 1))),
        ] + [weight_spec(wi) for wi in range(len(ws))] + side_in,
        out_specs=[out_spec(o) for o in range(len(out_dtypes))] + side_out,
        out_shape=[jax.ShapeDtypeStruct((t, n_blocks[o] * tn), out_dtypes[o])
                   for o in range(len(out_dtypes))] + side_shapes,
        scratch_shapes=[pltpu.VMEM((tm, k), BF16)],
        compiler_params=_params("arbitrary" if side else "parallel", "arbitrary"),
        name=name,
    )(x2, gain.reshape(1, k), aux, *ws, *[w for w, _ in side])


def _cast_kernel(w_ref, o_ref):
    o_ref[...] = w_ref[...].astype(o_ref.dtype)


def _cast_bf16(w, layer, *, name):
    _, r, c = w.shape
    tc = min(c, 2048)
    tr = min(r, max(8, CAST_BLOCK_ELEMS // tc))
    return pl.pallas_call(
        _cast_kernel,
        grid=(r // tr, c // tc),
        in_specs=[pl.BlockSpec((None, tr, tc), lambda i, j: (layer, i, j))],
        out_specs=pl.BlockSpec((tr, tc), lambda i, j: (i, j)),
        out_shape=jax.ShapeDtypeStruct((r, c), BF16),
        compiler_params=_params("parallel", "parallel"),
        name=name,
    )(w)


def _mixer_out_mlp_kernel(x_ref, a_ref, wo_ref, g_ref, up_ref, down_ref, *rest, n_side):
    side_in, o_ref, side_out, h_scr = rest[:n_side], rest[n_side], rest[n_side + 1:-1], rest[-1]

    def mixer_out():
        _run_side_casts(side_in, side_out)
        for rows in _row_chunks(x_ref.shape[0]):
            x1 = x_ref[rows, :] + jnp.dot(a_ref[rows, :], wo_ref[...], preferred_element_type=F32)
            o_ref[rows, :] = x1
            h_scr[rows, :] = _rms_bf16(x1, g_ref[...])

    def mlp_tile():
        _run_side_casts(side_in, side_out)
        for rows in _row_chunks(x_ref.shape[0]):
            u = jnp.dot(h_scr[rows, :], up_ref[...], preferred_element_type=F32)
            a = jnp.square(jnp.maximum(u, 0.0)).astype(BF16)
            o_ref[rows, :] += jnp.dot(a, down_ref[...], preferred_element_type=F32)

    j = pl.program_id(1)
    pl.when(j == 0)(mixer_out)
    pl.when(j > 0)(mlp_tile)


def _mixer_out_mlp(x2, a, w_out, gain, up, down, *, name, side=(), tm=512, tf=1024):
    t, d = x2.shape
    ka = a.shape[1]
    f = up.shape[1]
    tm, tf = min(tm, t), min(tf, f)
    n_inner = 1 + f // tf
    plans = _side_cast_plan(side, (t // tm) * n_inner)
    side_in, side_out, side_shapes = _side_cast_specs(side, plans, lambda i, j: i * n_inner + j)
    return pl.pallas_call(
        functools.partial(_mixer_out_mlp_kernel, n_side=len(side)),
        grid=(t // tm, n_inner),
        in_specs=[
            pl.BlockSpec((tm, d), lambda i, j: (i, 0)),
            pl.BlockSpec((tm, ka), lambda i, j: (i, 0)),
            pl.BlockSpec((ka, d), lambda i, j: (0, 0), pipeline_mode=pl.Buffered(1)),
            pl.BlockSpec((1, d), lambda i, j: (0, 0)),
            pl.BlockSpec((d, tf), lambda i, j: (0, jnp.maximum(j - 1, 0))),
            pl.BlockSpec((tf, d), lambda i, j: (jnp.maximum(j - 1, 0), 0)),
        ] + side_in,
        out_specs=[pl.BlockSpec((tm, d), lambda i, j: (i, 0))] + side_out,
        out_shape=[jax.ShapeDtypeStruct((t, d), F32)] + side_shapes,
        scratch_shapes=[pltpu.VMEM((tm, d), BF16)],
        compiler_params=_params("arbitrary" if side else "parallel", "arbitrary"),
        name=name,
    )(x2, a, w_out, gain.reshape(1, d), up, down, *[w for w, _ in side])


def _ple_kernel(x_ref, g_ref, p_ref, gw_ref, pw_ref, o_ref, h_scr, *, tn):
    j = pl.program_id(1)
    cols = pl.ds(pl.multiple_of(j * tn, tn), tn)

    def run(first):
        for rows in _row_chunks(x_ref.shape[0]):
            if first:
                h = _rms_bf16(x_ref[rows, :], g_ref[...])
                h_scr[rows, :] = h
            else:
                h = h_scr[rows, :]
            gate = jax.nn.sigmoid(jnp.dot(h, gw_ref[...], preferred_element_type=F32))
            emb = jnp.dot(p_ref[rows, :].astype(BF16), pw_ref[...], preferred_element_type=F32)
            o_ref[rows, :] = x_ref[rows, cols] + gate * emb

    pl.when(j == 0)(lambda: run(True))
    pl.when(j > 0)(lambda: run(False))


def _ple(x2, gain, p3, layer, gate_w, proj_w, *, name, tm=1024, tn=1024):
    t, d = x2.shape
    pd = p3.shape[2]
    tm, tn = min(tm, t), min(tn, d)
    return pl.pallas_call(
        functools.partial(_ple_kernel, tn=tn),
        grid=(t // tm, d // tn),
        in_specs=[
            pl.BlockSpec((tm, d), lambda i, j: (i, 0)),
            pl.BlockSpec((1, d), lambda i, j: (0, 0)),
            pl.BlockSpec((None, tm, pd), lambda i, j: (layer, i, 0)),
            pl.BlockSpec((d, tn), lambda i, j: (0, j)),
            pl.BlockSpec((pd, tn), lambda i, j: (0, j)),
        ],
        out_specs=pl.BlockSpec((tm, tn), lambda i, j: (i, j)),
        out_shape=jax.ShapeDtypeStruct((t, d), F32),
        scratch_shapes=[pltpu.VMEM((tm, d), BF16)],
        compiler_params=_params("parallel", "arbitrary"),
        name=name,
    )(x2, gain.reshape(1, d), p3, gate_w, proj_w)


def _hgrn_levels(c):
    return [c >> (l + 1) for l in range(int(math.log2(c)))]


def _hgrn_constants(c):
    idx = np.arange(c)
    x = idx[:, None] ^ idx[None, :]
    masks = [((x >= m) & (x < 2 * m) & ((idx[:, None] & m) != 0)).astype(np.float32) for m in _hgrn_levels(c)]
    return np.tril(np.ones((c, c), np.float32)), np.stack(masks, 0)


def _level_log_decay(g_cum, g, m):
    c, w = g_cum.shape
    if m == 1:
        odd = (lax.broadcasted_iota(jnp.int32, (c, w), 0) & 1) != 0
        return jnp.where(odd, g, 0.0)
    if m < 8:
        g3 = g_cum.reshape(c // 8, 8, w)
        sub = lax.broadcasted_iota(jnp.int32, g3.shape, 1)
        if m == 4:
            mid = g3[:, 3:4, :]
        else:
            mid = jnp.where(sub < 4, g3[:, 1:2, :], g3[:, 5:6, :])
        return (-jnp.abs(g3 - mid)).reshape(c, w)
    g3 = g_cum.reshape(c // (2 * m), 2 * m, w)
    mid = g3[:, m - 1:m, :]
    return jnp.concatenate([mid - g3[:, :m, :], g3[:, m:, :] - mid], axis=1).reshape(c, w)


def _pair_rows(lower, upper, m):
    c, w = lower.shape
    if m < 8:
        return jnp.where((lax.broadcasted_iota(jnp.int32, (c, w), 0) & m) != 0, lower, upper)
    lo3 = lower.reshape(c // (2 * m), 2 * m, w)
    up3 = upper.reshape(c // (2 * m), 2 * m, w)
    return jnp.concatenate([up3[:, :m, :], lo3[:, m:, :]], axis=1).reshape(c, w)


def _hgrn_kernel(q_ref, f_ref, v_ref, gate_ref, tri_ref, msk_ref, on_ref, o_ref, st_scr, *, c, hb, nchunks):
    @pl.when(pl.program_id(2) == 0)
    def _():
        st_scr[...] = jnp.zeros_like(st_scr)

    levels = _hgrn_levels(c)
    hw = hb * LANES

    def chunk_decay(ci, worst):
        g = f_ref[0, pl.ds(pl.multiple_of(ci * c, c), c), :]
        halves = jnp.minimum(jnp.sum(g[:c // 2], axis=0, keepdims=True), jnp.sum(g[c // 2:], axis=0, keepdims=True))
        return jnp.minimum(worst, halves)

    worst = lax.fori_loop(0, nchunks, chunk_decay, jnp.zeros((1, hw), F32))
    mild = jnp.min(worst) > -HGRN_SAFE_LOG2_RANGE

    def chunk(ci, mild_decay):
        rows = pl.ds(pl.multiple_of(ci * c, c), c)
        g_all = f_ref[0, rows, :]
        g_hi = g_all.astype(BF16)
        g_lo = (g_all - g_hi.astype(F32)).astype(BF16)
        tri = tri_ref[...]
        g_cum_all = (jnp.dot(tri, g_hi, preferred_element_type=F32)
                     + jnp.dot(tri, g_lo, preferred_element_type=F32))
        k_all = 1.0 - jnp.exp2(g_all)
        g_last = g_cum_all[c - 1:c, :]
        if mild_decay:
            g_mid = g_cum_all[c // 2 - 1:c // 2, :]
            q_mid_all = jnp.exp2(g_cum_all - g_mid)
            k_mid_all = jnp.exp2(g_mid - g_cum_all)
            to_mid_all = jnp.exp2(g_mid)
            mid_to_end_all = jnp.exp2(g_last - g_mid)
            causal = (lax.broadcasted_iota(jnp.int32, (c, c), 0) >= lax.broadcasted_iota(jnp.int32, (c, c), 1))
        else:
            q_dec_all = jnp.exp2(g_cum_all)
            k_dec_all = jnp.exp2(g_last - g_cum_all)
            st_dec_all = jnp.exp2(g_last)
            lvl_dec_all = [jnp.exp2(_level_log_decay(g_cum_all, g_all, m)) for m in levels]
        for h in range(hb):
            hl = slice(h * LANES, (h + 1) * LANES)
            qs = q_ref[0, rows, hl].astype(F32)
            vb = v_ref[0, rows, hl]
            k = k_all[:, hl]
            st = st_scr[h]
            if mild_decay:
                qm = (qs * q_mid_all[:, hl]).astype(BF16)
                km = (k * k_mid_all[:, hl]).astype(BF16)
                a = jnp.where(causal, lax.dot_general(qm, km, _NT, preferred_element_type=F32), 0.0)
                st_mid = st * to_mid_all[:, hl]
                o = (jnp.dot(a.astype(BF16), vb, preferred_element_type=F32)
                     + lax.dot_general(qm, st_mid.astype(BF16), _NT, preferred_element_type=F32))
                st_scr[h] = ((st_mid + lax.dot_general(vb, km, _TN, preferred_element_type=F32))
                             * mid_to_end_all[:, hl])
            else:
                a = jnp.zeros((c, c), F32)
                for l, m in enumerate(levels):
                    x = (_pair_rows(qs, k, m) * lvl_dec_all[l][:, hl]).astype(BF16)
                    a = a + msk_ref[l] * lax.dot_general(x, x, _NT, preferred_element_type=F32)
                o = jnp.dot(a.astype(BF16), vb, preferred_element_type=F32)
                o = o + jnp.sum(qs * k, axis=-1, keepdims=True) * vb.astype(F32)
                qg = (qs * q_dec_all[:, hl]).astype(BF16)
                o = o + lax.dot_general(qg, st.astype(BF16), _NT, preferred_element_type=F32)
                kt = (k * k_dec_all[:, hl]).astype(BF16)
                st_scr[h] = st * st_dec_all[:, hl] + lax.dot_general(vb, kt, _TN, preferred_element_type=F32)
            ms = jnp.mean(o * o, axis=-1, keepdims=True)
            y = o * lax.rsqrt(ms + EPS) * on_ref[...] * gate_ref[0, rows, hl].astype(F32)
            o_ref[0, rows, hl] = y.astype(o_ref.dtype)

    unroll = HGRN_UNROLL if nchunks % HGRN_UNROLL == 0 else 1

    def all_chunks(mild_decay):
        def trip(u, carry):
            for r in range(unroll):
                chunk(u * unroll + r, mild_decay)
            return carry

        lax.fori_loop(0, nchunks // unroll, trip, 0)

    pl.when(mild)(lambda: all_chunks(True))
    pl.when(jnp.logical_not(mild))(lambda: all_chunks(False))


def _hgrn(qig, log_f, onorm, *, name, rows=1024, hb=4):
    b, s, width = log_f.shape
    c = HGRN_CHUNK
    rows = min(rows, s)
    hw = hb * LANES
    nsb = width // hw
    tri, masks = _hgrn_constants(c)
    nl = masks.shape[0]

    def sec(k):
        return pl.BlockSpec((1, rows, hw), lambda bi, hg, r: (bi, r, k * nsb + hg))

    kern = functools.partial(_hgrn_kernel, c=c, hb=hb, nchunks=rows // c)
    return pl.pallas_call(
        kern,
        grid=(b, nsb, s // rows),
        in_specs=[
            sec(0), sec(0), sec(1), sec(2),
            pl.BlockSpec((c, c), lambda bi, hg, r: (0, 0)),
            pl.BlockSpec((nl, c, c), lambda bi, hg, r: (0, 0, 0)),
            pl.BlockSpec((1, LANES), lambda bi, hg, r: (0, 0)),
        ],
        out_specs=pl.BlockSpec((1, rows, hw), lambda bi, hg, r: (bi, r, hg)),
        out_shape=jax.ShapeDtypeStruct((b, s, width), BF16),
        scratch_shapes=[pltpu.VMEM((hb, LANES, LANES), F32)],
        compiler_params=_params("parallel", "parallel", "arbitrary"),
        name=name,
    )(qig, log_f, qig, qig, jnp.asarray(tri, BF16), jnp.asarray(masks, F32), onorm.reshape(1, LANES))


def _attn_kernel(q_ref, k_ref, v_ref, qg_ref, kg_ref, sub_ref, lq1_ref, lk1_ref, lq2_ref, lk2_ref, o_ref,
                 m_scr, l_scr, acc_scr, shift_scr, *, tq, lam_init):
    hw = acc_scr.shape[-1]
    nq = q_ref.shape[1] // tq
    maps = [slice(c * LANES, (c + 1) * LANES) for c in range(2)]

    bounds = []
    for c, sl in enumerate(maps):
        q_max = jnp.max(jnp.abs(qg_ref[:, sl]), axis=-1, keepdims=True)
        k_max = jnp.max(jnp.abs(kg_ref[:, sl]), axis=-1, keepdims=True)
        bounds.append(BOUND_SLACK * LANES * q_max * k_max)
        shift_scr[c] = jnp.broadcast_to(bounds[c], (1, LANES))
    bounded = jnp.max(jnp.maximum(bounds[0], bounds[1])) < SAFE_LOG2_SHIFT

    def causal(shape, row0):
        row = lax.broadcasted_iota(jnp.int32, shape, 0) + row0
        return lax.broadcasted_iota(jnp.int32, shape, 1) <= row

    def block(qi, slot, j, masked, running_max, first=False):
        for qrows in _row_chunks(tq):
            nk = qrows.stop if masked else tq
            keys = pl.ds(_aligned(j * tq, tq), nk)
            qsel = pl.ds(_aligned(qi * tq + qrows.start, qrows.stop - qrows.start), qrows.stop - qrows.start)
            vb = v_ref[0, keys, :]
            for c, sl in enumerate(maps):
                s = lax.dot_general(q_ref[0, qsel, sl], k_ref[0, keys, sl], _NT,
                                    preferred_element_type=F32)
                if running_max:
                    if masked:
                        s = jnp.where(causal(s.shape, qrows.start), s, -jnp.inf)
                    m_prev = m_scr[c, qrows, :]
                    m_new = jnp.maximum(m_prev, jnp.max(s, axis=-1, keepdims=True))
                    alpha = jnp.exp2(m_prev - m_new)
                    p = jnp.exp2(s - jnp.tile(m_new, (1, nk // LANES)))
                    l_scr[slot, c, qrows, :] = (alpha * l_scr[slot, c, qrows, :]
                                                + jnp.sum(p, axis=-1, keepdims=True))
                    acc_scr[slot, c, qrows, :] = (jnp.tile(alpha, (1, hw // LANES)) * acc_scr[slot, c, qrows, :]
                                                  + jnp.dot(p.astype(BF16), vb, preferred_element_type=F32))
                    m_scr[c, qrows, :] = m_new
                else:
                    p = jnp.exp2(s - jnp.tile(shift_scr[c], (1, nk // LANES)))
                    if masked:
                        p = jnp.where(causal(p.shape, qrows.start), p, 0.0)
                    lane_sums = p[:, 0:LANES]
                    for g in range(1, nk // LANES):
                        lane_sums = lane_sums + p[:, g * LANES:(g + 1) * LANES]
                    pv = jnp.dot(p.astype(BF16), vb, preferred_element_type=F32)
                    if first:
                        l_scr[slot, c, qrows, :] = lane_sums
                        acc_scr[slot, c, qrows, :] = pv
                    else:
                        l_scr[slot, c, qrows, :] += lane_sums
                        acc_scr[slot, c, qrows, :] += pv

    lam = (jnp.exp(jnp.sum(lq1_ref[...] * lk1_ref[...], axis=-1, keepdims=True))
           - jnp.exp(jnp.sum(lq2_ref[...] * lk2_ref[...], axis=-1, keepdims=True))
           + lam_init)

    def finalize(qi, slot, lanes_summed):
        ls = [l_scr[slot, c] for c in range(2)]
        if not lanes_summed:
            ls = [jnp.sum(l, axis=-1, keepdims=True) for l in ls]
            invs = [jnp.broadcast_to(1.0 / l, (tq, hw)) for l in ls]
        else:
            invs = [jnp.tile(1.0 / l, (1, hw // LANES)) for l in ls]
        o = acc_scr[slot, 0] * invs[0] - lam * (acc_scr[slot, 1] * invs[1])
        ms = jnp.mean(o * o, axis=-1, keepdims=True)
        o_ref[0, pl.ds(_aligned(qi * tq, tq), tq), :] = (o * lax.rsqrt(ms + EPS)
                                           * (sub_ref[...] * (1.0 - lam_init))).astype(o_ref.dtype)

    @pl.when(bounded)
    def _():
        for qi in range(nq):
            slot = qi % 2
            block(qi, slot, qi, True, False, first=True)
            for j in range(qi):
                block(qi, slot, j, False, False)
            finalize(qi, slot, lanes_summed=False)

    @pl.when(jnp.logical_not(bounded))
    def _():
        def query_block(qi, carry):
            m_scr[...] = jnp.full_like(m_scr, -1e30)
            l_scr[0] = jnp.zeros_like(l_scr[0])
            acc_scr[0] = jnp.zeros_like(acc_scr[0])

            def body(j, c2):
                block(qi, 0, j, False, True)
                return c2

            lax.fori_loop(0, qi, body, 0)
            block(qi, 0, qi, True, True)
            finalize(qi, 0, lanes_summed=True)
            return carry

        lax.fori_loop(0, nq, query_block, 0)


def _diff_attention(q, kv, q_gain, k_gain, subln, lq1, lk1, lq2, lk2, *, lam_init, name, tq=1024):
    b, s, width = q.shape
    hw = 2 * LANES
    nh = width // hw
    tq = min(tq, s)
    vec = pl.BlockSpec((1, LANES), lambda bi, h: (0, 0))
    head_vec = pl.BlockSpec((1, hw), lambda bi, h: (0, 0))
    return pl.pallas_call(
        functools.partial(_attn_kernel, tq=tq, lam_init=lam_init),
        grid=(b, nh),
        in_specs=[
            pl.BlockSpec((1, s, hw), lambda bi, h: (bi, 0, h)),
            pl.BlockSpec((1, s, hw), lambda bi, h: (bi, 0, h)),
            pl.BlockSpec((1, s, hw), lambda bi, h: (bi, 0, nh + h)),
            head_vec, head_vec, head_vec,
            vec, vec, vec, vec,
        ],
        out_specs=pl.BlockSpec((1, s, hw), lambda bi, h: (bi, 0, h)),
        out_shape=jax.ShapeDtypeStruct((b, s, width), BF16),
        scratch_shapes=[
            pltpu.VMEM((2, tq, LANES), F32),
            pltpu.VMEM((2, 2, tq, LANES), F32),
            pltpu.VMEM((2, 2, tq, hw), F32),
            pltpu.VMEM((2, 1, LANES), F32),
        ],
        compiler_params=_params("parallel", "parallel"),
        name=name,
    )(q, kv, kv, q_gain.reshape(1, hw), k_gain.reshape(1, hw), subln.reshape(1, hw), lq1.reshape(1, LANES),
      lk1.reshape(1, LANES),
      lq2.reshape(1, LANES), lk2.reshape(1, LANES))


def kernel(x, p, ln_mix, ln_mlp, ln_ple, a_w_in, a_lb, a_onorm, a_w_out, kv_norm, w_k, w_v, k_norm, b_w_q, q_norm, lam_q1, lam_k1, lam_q2, lam_k2, b_subln, b_w_out, mlp_up, mlp_down, ple_proj, ple_gate):
    b, s, d = x.shape
    depth = ln_mix.shape[0]
    n_a = a_w_in.shape[0]
    t = b * s
    x2 = x.reshape(t, d)
    kv = None

    def in_proj_weight(i):
        return (a_w_in, i) if i < n_a else (b_w_q, i - n_a)

    def layer_weights(i):
        w_out = (a_w_out, i) if i < n_a else (b_w_out, i - n_a)
        return [(("w_out", i), w_out), (("up", i), (mlp_up, i)), (("down", i), (mlp_down, i)),
                (("gate", i), (ple_gate, i)), (("proj", i), (ple_proj, i))]

    bf16w = {("in", 0): _cast_bf16(*in_proj_weight(0), name="cast_in_proj_0")}
    for i in range(depth):
        carried = layer_weights(i) if i == 0 else []
        side = [w for _, w in carried]
        if i < n_a:
            j = i
            width = a_w_out.shape[1]
            qig, log_f, *cast = _norm_linear(
                x2, ln_mix[i], (bf16w["in", i],), a_lb,
                sections=(("silu", 0, 0, 0), ("plain", 0, 0, 2), ("silu", 0, 0, 3), ("log_forget", 1, 0, 1)),
                aux_section=3, sec_width=width, out_dtypes=(BF16, F32), lb_rows=i + 1, side=side,
                name=f"hgrn_in_{i}")
            o = _hgrn(qig.reshape(b, s, 3 * width), log_f.reshape(b, s, width), a_onorm[j], name=f"hgrn_{i}")
        else:
            j = i - n_a
            width = b_w_q.shape[2]
            q_scale = math.log2(math.e) / math.sqrt(LANES)
            head_gain = jnp.tile(q_norm[j].reshape(1, -1), (1, width // (2 * LANES)))
            q, *cast = _norm_linear(x2, ln_mix[i], (bf16w["in", i],), head_gain,
                                    sections=(("head_rms", 0, 0, 0),), aux_section=0, sec_width=width,
                                    out_dtypes=(BF16,), scale=q_scale, side=side, name=f"attn_q_{i}")
            lam_init = 0.8 - 0.6 * math.exp(-0.3 * i)
            o = _diff_attention(q.reshape(b, s, width), kv, q_norm[j] * q_scale, k_norm, b_subln[j], lam_q1[j],
                                lam_k1[j], lam_q2[j], lam_k2[j], lam_init=lam_init, name=f"attn_{i}")
        bf16w.update({key: w for (key, _), w in zip(carried, cast)})

        carried = []
        if i == n_a - 1:
            carried += [(("k", 0), (w_k[None], 0)), (("v", 0), (w_v[None], 0))]
        if i + 1 < depth:
            carried += [(("in", i + 1), in_proj_weight(i + 1))] + layer_weights(i + 1)
        x2, *cast = _mixer_out_mlp(x2, o.reshape(t, width), bf16w["w_out", i], ln_mlp[i], bf16w["up", i],
                                   bf16w["down", i], side=[w for _, w in carried], name=f"mlp_{i}")
        bf16w.update({key: w for (key, _), w in zip(carried, cast)})
        x2 = _ple(x2, ln_ple[i], p.reshape(depth, t, -1), i, bf16w["gate", i], bf16w["proj", i], name=f"ple_{i}")
        if i == n_a - 1:
            width = w_k.shape[1]
            head_gain = jnp.tile(k_norm.reshape(1, -1), (1, width // (2 * LANES)))
            (kv,) = _norm_linear(x2, kv_norm, (bf16w["k", 0], bf16w["v", 0]), head_gain,
                                 sections=(("head_rms", 0, 0, 0), ("plain", 0, 1, 0)), aux_section=0,
                                 sec_width=width, out_dtypes=(BF16,), name="shared_kv")
            kv = kv.reshape(b, s, 2 * width)
    return x2.reshape(b, s, d)
```

```python
import functools
import math

import numpy as np
import jax
import jax.numpy as jnp
from jax import lax
from jax.experimental import pallas as pl
from jax.experimental.pallas import tpu as pltpu

EPS = 1e-6
LANES = 128
HGRN_CHUNK = 128
HGRN_UNROLL = 8
HGRN_SAFE_LOG2_RANGE = 100.0
CAST_BLOCK_ELEMS = 2 * 1024 * 1024
ROW_CHUNK = 256
SAFE_LOG2_SHIFT = 60.0
BOUND_SLACK = 1.01
BF16_TILE_ROWS = 16
SIDE_CAST_MIN_COLS = 512
F32 = jnp.float32
BF16 = jnp.bfloat16
VMEM_LIMIT_BYTES = 56 * 1024 * 1024

_NT = (((1,), (1,)), ((), ()))
_TN = (((0,), (0,)), ((), ()))


def _params(*sem, **extra):
    return pltpu.CompilerParams(dimension_semantics=sem, vmem_limit_bytes=VMEM_LIMIT_BYTES, **extra)


def _rms_bf16(x, gain):
    ms = jnp.mean(x * x, axis=-1, keepdims=True)
    return (x * lax.rsqrt(ms + EPS) * gain).astype(BF16)


def _silu(x):
    return x * jax.nn.sigmoid(x)


def _group_rms(y, gain, scale):
    outs = []
    for c in range(y.shape[1] // LANES):
        sl = slice(c * LANES, (c + 1) * LANES)
        yc = y[:, sl]
        ms = jnp.mean(yc * yc, axis=-1, keepdims=True)
        outs.append(yc * lax.rsqrt(ms + EPS) * (gain[:, sl] * scale))
    return jnp.concatenate(outs, axis=1)


def _aligned(start, multiple):
    return start if isinstance(start, int) else pl.multiple_of(start, multiple)


def _row_chunks(tm):
    rc = min(ROW_CHUNK, tm)
    return [slice(r * rc, (r + 1) * rc) for r in range(tm // rc)]


def _side_cast_plan(side, n_steps):
    plans = []
    for w, _ in side:
        _, r, c = w.shape
        n_rows = next(n for n in range(min(n_steps, r // BF16_TILE_ROWS), 0, -1)
                      if r % n == 0 and (r // n) % BF16_TILE_ROWS == 0)
        n_cols = 1
        while n_rows * n_cols * 2 <= n_steps and c % (n_cols * 2) == 0 and c // (n_cols * 2) >= SIDE_CAST_MIN_COLS:
            n_cols *= 2
        plans.append((r // n_rows, c // n_cols))
    return plans


def _side_cast_specs(side, plans, step_of):
    in_specs, out_specs, out_shapes = [], [], []
    for (w, layer), (br, bc) in zip(side, plans):
        _, r, c = w.shape
        n_cols = c // bc
        last = (r // br) * n_cols - 1

        def block_of(*g, n_cols=n_cols, last=last):
            idx = jnp.minimum(step_of(*g), last)
            return idx // n_cols, idx % n_cols

        in_specs.append(pl.BlockSpec((None, br, bc), lambda *g, layer=layer, f=block_of: (layer, *f(*g))))
        out_specs.append(pl.BlockSpec((br, bc), lambda *g, f=block_of: f(*g)))
        out_shapes.append(jax.ShapeDtypeStruct((r, c), BF16))
    return in_specs, out_specs, out_shapes


def _run_side_casts(in_refs, out_refs):
    for i_ref, o_ref in zip(in_refs, out_refs):
        o_ref[...] = i_ref[...].astype(o_ref.dtype)


def _norm_linear_kernel(x_ref, g_ref, aux_ref, *rest, sections, n_weights, n_side, sec_blocks, lb_rows, scale):
    n_out = len(rest) - 1 - n_weights - 2 * n_side
    w_refs, side_in = rest[:n_weights], rest[n_weights:n_weights + n_side]
    o_refs = rest[n_weights + n_side:n_weights + n_side + n_out]
    side_out, h_scr = rest[n_weights + n_side + n_out:-1], rest[-1]
    j = pl.program_id(1)

    def epilogue(name, acc):
        if name == "silu":
            return _silu(acc)
        if name == "plain":
            return acc
        if name == "head_rms":
            return _group_rms(acc, aux_ref[...], scale)
        if name == "log_forget":
            a = aux_ref[...]
            e = jnp.exp(a - jnp.max(a, axis=0, keepdims=True))
            lb = jnp.sum(e[:lb_rows], axis=0, keepdims=True) / jnp.sum(e, axis=0, keepdims=True)
            return jnp.log2(lb + (1.0 - lb) * jax.nn.sigmoid(acc))
        raise ValueError(name)

    def run(first, name, o_ref, w_ref):
        _run_side_casts(side_in, side_out)
        for rows in _row_chunks(x_ref.shape[0]):
            if first:
                h = _rms_bf16(x_ref[rows, :], g_ref[...])
                h_scr[rows, :] = h
            else:
                h = h_scr[rows, :]
            acc = jnp.dot(h, w_ref[...], preferred_element_type=F32)
            o_ref[rows, :] = epilogue(name, acc).astype(o_ref.dtype)

    for s, (name, out_idx, w_idx, _) in enumerate(sections):
        lo, hi = s * sec_blocks, (s + 1) * sec_blocks
        if s == 0:
            pl.when(j == 0)(functools.partial(run, True, name, o_refs[out_idx], w_refs[w_idx]))
            lo = 1
        if lo < hi:
            pl.when(jnp.logical_and(j >= lo, j < hi))(
                functools.partial(run, False, name, o_refs[out_idx], w_refs[w_idx]))


def _step_table(j, table):
    out = table[0]
    for jj in range(1, len(table)):
        if table[jj] != table[jj - 1]:
            out = jnp.where(j >= jj, table[jj], out)
    return out


def _norm_linear(x2, gain, ws, aux, *, sections, aux_section, sec_width, out_dtypes, name, lb_rows=1, scale=1.0,
                 side=(), tm=1024, tn=1024):
    t, k = x2.shape
    tm, tn = min(tm, t), min(tn, sec_width)
    sb = sec_width // tn
    n_steps = len(sections) * sb
    plans = _side_cast_plan(side, (t // tm) * n_steps)
    side_in, side_out, side_shapes = _side_cast_specs(side, plans, lambda i, j: i * n_steps + j)
    first_step = [None] * len(out_dtypes)
    n_blocks = [0] * len(out_dtypes)
    for s, (_, o, _, _) in enumerate(sections):
        if first_step[o] is None:
            first_step[o] = s * sb
        assert s * sb == first_step[o] + n_blocks[o], "an output's sections must be consecutive"
        n_blocks[o] += sb

    def out_spec(o):
        return pl.BlockSpec((tm, tn), lambda i, j: (i, jnp.clip(j - first_step[o], 0, n_blocks[o] - 1)))

    def weight_spec(w_idx):
        cols = [None] * n_steps
        for s, (_, _, wi, w_sec) in enumerate(sections):
            if wi == w_idx:
                for r in range(sb):
                    cols[s * sb + r] = w_sec * sb + r
        held = next(c for c in reversed(cols) if c is not None)
        for jj in range(n_steps):
            if cols[jj] is None:
                cols[jj] = held
            held = cols[jj]
        return pl.BlockSpec((k, tn), lambda i, j: (0, _step_table(j, cols)))

    kern = functools.partial(_norm_linear_kernel, sections=tuple(sections), n_weights=len(ws), n_side=len(side),
                             sec_blocks=sb, lb_rows=lb_rows, scale=scale)
    return pl.pallas_call(
        kern,
        grid=(t // tm, n_steps),
        in_specs=[
            pl.BlockSpec((tm, k), lambda i, j: (i, 0)),
            pl.BlockSpec((1, k), lambda i, j: (0, 0)),
            pl.BlockSpec((aux.shape[0], tn), lambda i, j: (0, jnp.clip(j - aux_section * sb, 0, sb - 1))),
        ] + [weight_spec(wi) for wi in range(len(ws))] + side_in,
        out_specs=[out_spec(o) for o in range(len(out_dtypes))] + side_out,
        out_shape=[jax.ShapeDtypeStruct((t, n_blocks[o] * tn), out_dtypes[o])
                   for o in range(len(out_dtypes))] + side_shapes,
        scratch_shapes=[pltpu.VMEM((tm, k), BF16)],
        compiler_params=_params("arbitrary" if side else "parallel", "arbitrary"),
        name=name,
    )(x2, gain.reshape(1, k), aux, *ws, *[w for w, _ in side])


def _cast_kernel(w_ref, o_ref):
    o_ref[...] = w_ref[...].astype(o_ref.dtype)


def _cast_bf16(w, layer, *, name):
    _, r, c = w.shape
    tc = min(c, 2048)
    tr = min(r, max(8, CAST_BLOCK_ELEMS // tc))
    return pl.pallas_call(
        _cast_kernel,
        grid=(r // tr, c // tc),
        in_specs=[pl.BlockSpec((None, tr, tc), lambda i, j: (layer, i, j))],
        out_specs=pl.BlockSpec((tr, tc), lambda i, j: (i, j)),
        out_shape=jax.ShapeDtypeStruct((r, c), BF16),
        compiler_params=_params("parallel", "parallel"),
        name=name,
    )(w)


def _mixer_out_mlp_kernel(x_ref, a_ref, wo_ref, g_ref, up_ref, down_ref, *rest, n_side):
    side_in, o_ref, side_out, h_scr = rest[:n_side], rest[n_side], rest[n_side + 1:-1], rest[-1]

    def mixer_out():
        _run_side_casts(side_in, side_out)
        for rows in _row_chunks(x_ref.shape[0]):
            x1 = x_ref[rows, :] + jnp.dot(a_ref[rows, :], wo_ref[...], preferred_element_type=F32)
            o_ref[rows, :] = x1
            h_scr[rows, :] = _rms_bf16(x1, g_ref[...])

    def mlp_tile():
        _run_side_casts(side_in, side_out)
        for rows in _row_chunks(x_ref.shape[0]):
            u = jnp.dot(h_scr[rows, :], up_ref[...], preferred_element_type=F32)
            a = jnp.square(jnp.maximum(u, 0.0)).astype(BF16)
            o_ref[rows, :] += jnp.dot(a, down_ref[...], preferred_element_type=F32)

    j = pl.program_id(1)
    pl.when(j == 0)(mixer_out)
    pl.when(j > 0)(mlp_tile)


def _mixer_out_mlp(x2, a, w_out, gain, up, down, *, name, side=(), tm=512, tf=1024):
    t, d = x2.shape
    ka = a.shape[1]
    f = up.shape[1]
    tm, tf = min(tm, t), min(tf, f)
    n_inner = 1 + f // tf
    plans = _side_cast_plan(side, (t // tm) * n_inner)
    side_in, side_out, side_shapes = _side_cast_specs(side, plans, lambda i, j: i * n_inner + j)
    return pl.pallas_call(
        functools.partial(_mixer_out_mlp_kernel, n_side=len(side)),
        grid=(t // tm, n_inner),
        in_specs=[
            pl.BlockSpec((tm, d), lambda i, j: (i, 0)),
            pl.BlockSpec((tm, ka), lambda i, j: (i, 0)),
            pl.BlockSpec((ka, d), lambda i, j: (0, 0), pipeline_mode=pl.Buffered(1)),
            pl.BlockSpec((1, d), lambda i, j: (0, 0)),
            pl.BlockSpec((d, tf), lambda i, j: (0, jnp.maximum(j - 1, 0))),
            pl.BlockSpec((tf, d), lambda i, j: (jnp.maximum(j - 1, 0), 0)),
        ] + side_in,
        out_specs=[pl.BlockSpec((tm, d), lambda i, j: (i, 0))] + side_out,
        out_shape=[jax.ShapeDtypeStruct((t, d), F32)] + side_shapes,
        scratch_shapes=[pltpu.VMEM((tm, d), BF16)],
        compiler_params=_params("arbitrary" if side else "parallel", "arbitrary"),
        name=name,
    )(x2, a, w_out, gain.reshape(1, d), up, down, *[w for w, _ in side])


def _ple_kernel(x_ref, g_ref, p_ref, gw_ref, pw_ref, o_ref, h_scr, *, tn):
    j = pl.program_id(1)
    cols = pl.ds(pl.multiple_of(j * tn, tn), tn)

    def run(first):
        for rows in _row_chunks(x_ref.shape[0]):
            if first:
                h = _rms_bf16(x_ref[rows, :], g_ref[...])
                h_scr[rows, :] = h
            else:
                h = h_scr[rows, :]
            gate = jax.nn.sigmoid(jnp.dot(h, gw_ref[...], preferred_element_type=F32))
            emb = jnp.dot(p_ref[rows, :].astype(BF16), pw_ref[...], preferred_element_type=F32)
            o_ref[rows, :] = x_ref[rows, cols] + gate * emb

    pl.when(j == 0)(lambda: run(True))
    pl.when(j > 0)(lambda: run(False))


def _ple(x2, gain, p3, layer, gate_w, proj_w, *, name, tm=1024, tn=1024):
    t, d = x2.shape
    pd = p3.shape[2]
    tm, tn = min(tm, t), min(tn, d)
    return pl.pallas_call(
        functools.partial(_ple_kernel, tn=tn),
        grid=(t // tm, d // tn),
        in_specs=[
            pl.BlockSpec((tm, d), lambda i, j: (i, 0)),
            pl.BlockSpec((1, d), lambda i, j: (0, 0)),
            pl.BlockSpec((None, tm, pd), lambda i, j: (layer, i, 0)),
            pl.BlockSpec((d, tn), lambda i, j: (0, j)),
            pl.BlockSpec((pd, tn), lambda i, j: (0, j)),
        ],
        out_specs=pl.BlockSpec((tm, tn), lambda i, j: (i, j)),
        out_shape=jax.ShapeDtypeStruct((t, d), F32),
        scratch_shapes=[pltpu.VMEM((tm, d), BF16)],
        compiler_params=_params("parallel", "arbitrary"),
        name=name,
    )(x2, gain.reshape(1, d), p3, gate_w, proj_w)


def _hgrn_levels(c):
    return [c >> (l + 1) for l in range(int(math.log2(c)))]


def _hgrn_constants(c):
    idx = np.arange(c)
    x = idx[:, None] ^ idx[None, :]
    masks = [((x >= m) & (x < 2 * m) & ((idx[:, None] & m) != 0)).astype(np.float32) for m in _hgrn_levels(c)]
    return np.tril(np.ones((c, c), np.float32)), np.stack(masks, 0)


def _level_log_decay(g_cum, g, m):
    c, w = g_cum.shape
    if m == 1:
        odd = (lax.broadcasted_iota(jnp.int32, (c, w), 0) & 1) != 0
        return jnp.where(odd, g, 0.0)
    if m < 8:
        g3 = g_cum.reshape(c // 8, 8, w)
        sub = lax.broadcasted_iota(jnp.int32, g3.shape, 1)
        if m == 4:
            mid = g3[:, 3:4, :]
        else:
            mid = jnp.where(sub < 4, g3[:, 1:2, :], g3[:, 5:6, :])
        return (-jnp.abs(g3 - mid)).reshape(c, w)
    g3 = g_cum.reshape(c // (2 * m), 2 * m, w)
    mid = g3[:, m - 1:m, :]
    return jnp.concatenate([mid - g3[:, :m, :], g3[:, m:, :] - mid], axis=1).reshape(c, w)


def _pair_rows(lower, upper, m):
    c, w = lower.shape
    if m < 8:
        return jnp.where((lax.broadcasted_iota(jnp.int32, (c, w), 0) & m) != 0, lower, upper)
    lo3 = lower.reshape(c // (2 * m), 2 * m, w)
    up3 = upper.reshape(c // (2 * m), 2 * m, w)
    return jnp.concatenate([up3[:, :m, :], lo3[:, m:, :]], axis=1).reshape(c, w)


def _hgrn_kernel(q_ref, f_ref, v_ref, gate_ref, tri_ref, msk_ref, on_ref, o_ref, st_scr, *, c, hb, nchunks):
    @pl.when(pl.program_id(2) == 0)
    def _():
        st_scr[...] = jnp.zeros_like(st_scr)

    levels = _hgrn_levels(c)
    hw = hb * LANES

    def chunk_decay(ci, worst):
        g = f_ref[0, pl.ds(pl.multiple_of(ci * c, c), c), :]
        halves = jnp.minimum(jnp.sum(g[:c // 2], axis=0, keepdims=True), jnp.sum(g[c // 2:], axis=0, keepdims=True))
        return jnp.minimum(worst, halves)

    worst = lax.fori_loop(0, nchunks, chunk_decay, jnp.zeros((1, hw), F32))
    mild = jnp.min(worst) > -HGRN_SAFE_LOG2_RANGE

    def chunk(ci, mild_decay):
        rows = pl.ds(pl.multiple_of(ci * c, c), c)
        g_all = f_ref[0, rows, :]
        g_hi = g_all.astype(BF16)
        g_lo = (g_all - g_hi.astype(F32)).astype(BF16)
        tri = tri_ref[...]
        g_cum_all = (jnp.dot(tri, g_hi, preferred_element_type=F32)
                     + jnp.dot(tri, g_lo, preferred_element_type=F32))
        k_all = 1.0 - jnp.exp2(g_all)
        g_last = g_cum_all[c - 1:c, :]
        if mild_decay:
            g_mid = g_cum_all[c // 2 - 1:c // 2, :]
            q_mid_all = jnp.exp2(g_cum_all - g_mid)
            k_mid_all = jnp.exp2(g_mid - g_cum_all)
            to_mid_all = jnp.exp2(g_mid)
            mid_to_end_all = jnp.exp2(g_last - g_mid)
            causal = (lax.broadcasted_iota(jnp.int32, (c, c), 0) >= lax.broadcasted_iota(jnp.int32, (c, c), 1))
        else:
            q_dec_all = jnp.exp2(g_cum_all)
            k_dec_all = jnp.exp2(g_last - g_cum_all)
            st_dec_all = jnp.exp2(g_last)
            lvl_dec_all = [jnp.exp2(_level_log_decay(g_cum_all, g_all, m)) for m in levels]
        for h in range(hb):
            hl = slice(h * LANES, (h + 1) * LANES)
            qs = q_ref[0, rows, hl].astype(F32)
            vb = v_ref[0, rows, hl]
            k = k_all[:, hl]
            st = st_scr[h]
            if mild_decay:
                qm = (qs * q_mid_all[:, hl]).astype(BF16)
                km = (k * k_mid_all[:, hl]).astype(BF16)
                a = jnp.where(causal, lax.dot_general(qm, km, _NT, preferred_element_type=F32), 0.0)
                st_mid = st * to_mid_all[:, hl]
                o = (jnp.dot(a.astype(BF16), vb, preferred_element_type=F32)
                     + lax.dot_general(qm, st_mid.astype(BF16), _NT, preferred_element_type=F32))
                st_scr[h] = ((st_mid + lax.dot_general(vb, km, _TN, preferred_element_type=F32))
                             * mid_to_end_all[:, hl])
            else:
                a = jnp.zeros((c, c), F32)
                for l, m in enumerate(levels):
                    x = (_pair_rows(qs, k, m) * lvl_dec_all[l][:, hl]).astype(BF16)
                    a = a + msk_ref[l] * lax.dot_general(x, x, _NT, preferred_element_type=F32)
                o = jnp.dot(a.astype(BF16), vb, preferred_element_type=F32)
                o = o + jnp.sum(qs * k, axis=-1, keepdims=True) * vb.astype(F32)
                qg = (qs * q_dec_all[:, hl]).astype(BF16)
                o = o + lax.dot_general(qg, st.astype(BF16), _NT, preferred_element_type=F32)
                kt = (k * k_dec_all[:, hl]).astype(BF16)
                st_scr[h] = st * st_dec_all[:, hl] + lax.dot_general(vb, kt, _TN, preferred_element_type=F32)
            ms = jnp.mean(o * o, axis=-1, keepdims=True)
            y = o * lax.rsqrt(ms + EPS) * on_ref[...] * gate_ref[0, rows, hl].astype(F32)
            o_ref[0, rows, hl] = y.astype(o_ref.dtype)

    unroll = HGRN_UNROLL if nchunks % HGRN_UNROLL == 0 else 1

    def all_chunks(mild_decay):
        def trip(u, carry):
            for r in range(unroll):
                chunk(u * unroll + r, mild_decay)
            return carry

        lax.fori_loop(0, nchunks // unroll, trip, 0)

    pl.when(mild)(lambda: all_chunks(True))
    pl.when(jnp.logical_not(mild))(lambda: all_chunks(False))


def _hgrn(qig, log_f, onorm, *, name, rows=1024, hb=4):
    b, s, width = log_f.shape
    c = HGRN_CHUNK
    rows = min(rows, s)
    hw = hb * LANES
    nsb = width // hw
    tri, masks = _hgrn_constants(c)
    nl = masks.shape[0]

    def sec(k):
        return pl.BlockSpec((1, rows, hw), lambda bi, hg, r: (bi, r, k * nsb + hg))

    kern = functools.partial(_hgrn_kernel, c=c, hb=hb, nchunks=rows // c)
    return pl.pallas_call(
        kern,
        grid=(b, nsb, s // rows),
        in_specs=[
            sec(0), sec(0), sec(1), sec(2),
            pl.BlockSpec((c, c), lambda bi, hg, r: (0, 0)),
            pl.BlockSpec((nl, c, c), lambda bi, hg, r: (0, 0, 0)),
            pl.BlockSpec((1, LANES), lambda bi, hg, r: (0, 0)),
        ],
        out_specs=pl.BlockSpec((1, rows, hw), lambda bi, hg, r: (bi, r, hg)),
        out_shape=jax.ShapeDtypeStruct((b, s, width), BF16),
        scratch_shapes=[pltpu.VMEM((hb, LANES, LANES), F32)],
        compiler_params=_params("parallel", "parallel", "arbitrary"),
        name=name,
    )(qig, log_f, qig, qig, jnp.asarray(tri, BF16), jnp.asarray(masks, F32), onorm.reshape(1, LANES))


def _attn_kernel(q_ref, k_ref, v_ref, qg_ref, kg_ref, sub_ref, lq1_ref, lk1_ref, lq2_ref, lk2_ref, o_ref,
                 m_scr, l_scr, acc_scr, shift_scr, *, tq, lam_init):
    hw = acc_scr.shape[-1]
    nq = q_ref.shape[1] // tq
    maps = [slice(c * LANES, (c + 1) * LANES) for c in range(2)]

    bounds = []
    for c, sl in enumerate(maps):
        q_max = jnp.max(jnp.abs(qg_ref[:, sl]), axis=-1, keepdims=True)
        k_max = jnp.max(jnp.abs(kg_ref[:, sl]), axis=-1, keepdims=True)
        bounds.append(BOUND_SLACK * LANES * q_max * k_max)
        shift_scr[c] = jnp.broadcast_to(bounds[c], (1, LANES))
    bounded = jnp.max(jnp.maximum(bounds[0], bounds[1])) < SAFE_LOG2_SHIFT

    def causal(shape, row0):
        row = lax.broadcasted_iota(jnp.int32, shape, 0) + row0
        return lax.broadcasted_iota(jnp.int32, shape, 1) <= row

    def block(qi, slot, j, masked, running_max, first=False):
        for qrows in _row_chunks(tq):
            nk = qrows.stop if masked else tq
            keys = pl.ds(_aligned(j * tq, tq), nk)
            qsel = pl.ds(_aligned(qi * tq + qrows.start, qrows.stop - qrows.start), qrows.stop - qrows.start)
            vb = v_ref[0, keys, :]
            for c, sl in enumerate(maps):
                s = lax.dot_general(q_ref[0, qsel, sl], k_ref[0, keys, sl], _NT,
                                    preferred_element_type=F32)
                if running_max:
                    if masked:
                        s = jnp.where(causal(s.shape, qrows.start), s, -jnp.inf)
                    m_prev = m_scr[c, qrows, :]
                    m_new = jnp.maximum(m_prev, jnp.max(s, axis=-1, keepdims=True))
                    alpha = jnp.exp2(m_prev - m_new)
                    p = jnp.exp2(s - jnp.tile(m_new, (1, nk // LANES)))
                    l_scr[slot, c, qrows, :] = (alpha * l_scr[slot, c, qrows, :]
                                                + jnp.sum(p, axis=-1, keepdims=True))
                    acc_scr[slot, c, qrows, :] = (jnp.tile(alpha, (1, hw // LANES)) * acc_scr[slot, c, qrows, :]
                                                  + jnp.dot(p.astype(BF16), vb, preferred_element_type=F32))
                    m_scr[c, qrows, :] = m_new
                else:
                    p = jnp.exp2(s - jnp.tile(shift_scr[c], (1, nk // LANES)))
                    if masked:
                        p = jnp.where(causal(p.shape, qrows.start), p, 0.0)
                    lane_sums = p[:, 0:LANES]
                    for g in range(1, nk // LANES):
                        lane_sums = lane_sums + p[:, g * LANES:(g + 1) * LANES]
                    pv = jnp.dot(p.astype(BF16), vb, preferred_element_type=F32)
                    if first:
                        l_scr[slot, c, qrows, :] = lane_sums
                        acc_scr[slot, c, qrows, :] = pv
                    else:
                        l_scr[slot, c, qrows, :] += lane_sums
                        acc_scr[slot, c, qrows, :] += pv

    lam = (jnp.exp(jnp.sum(lq1_ref[...] * lk1_ref[...], axis=-1, keepdims=True))
           - jnp.exp(jnp.sum(lq2_ref[...] * lk2_ref[...], axis=-1, keepdims=True))
           + lam_init)

    def finalize(qi, slot, lanes_summed):
        ls = [l_scr[slot, c] for c in range(2)]
        if not lanes_summed:
            ls = [jnp.sum(l, axis=-1, keepdims=True) for l in ls]
            invs = [jnp.broadcast_to(1.0 / l, (tq, hw)) for l in ls]
        else:
            invs = [jnp.tile(1.0 / l, (1, hw // LANES)) for l in ls]
        o = acc_scr[slot, 0] * invs[0] - lam * (acc_scr[slot, 1] * invs[1])
        ms = jnp.mean(o * o, axis=-1, keepdims=True)
        o_ref[0, pl.ds(_aligned(qi * tq, tq), tq), :] = (o * lax.rsqrt(ms + EPS)
                                           * (sub_ref[...] * (1.0 - lam_init))).astype(o_ref.dtype)

    @pl.when(bounded)
    def _():
        for qi in range(nq):
            slot = qi % 2
            block(qi, slot, qi, True, False, first=True)
            for j in range(qi):
                block(qi, slot, j, False, False)
            finalize(qi, slot, lanes_summed=False)

    @pl.when(jnp.logical_not(bounded))
    def _():
        def query_block(qi, carry):
            m_scr[...] = jnp.full_like(m_scr, -1e30)
            l_scr[0] = jnp.zeros_like(l_scr[0])
            acc_scr[0] = jnp.zeros_like(acc_scr[0])

            def body(j, c2):
                block(qi, 0, j, False, True)
                return c2

            lax.fori_loop(0, qi, body, 0)
            block(qi, 0, qi, True, True)
            finalize(qi, 0, lanes_summed=True)
            return carry

        lax.fori_loop(0, nq, query_block, 0)


def _diff_attention(q, kv, q_gain, k_gain, subln, lq1, lk1, lq2, lk2, *, lam_init, name, tq=1024):
    b, s, width = q.shape
    hw = 2 * LANES
    nh = width // hw
    tq = min(tq, s)
    vec = pl.BlockSpec((1, LANES), lambda bi, h: (0, 0))
    head_vec = pl.BlockSpec((1, hw), lambda bi, h: (0, 0))
    return pl.pallas_call(
        functools.partial(_attn_kernel, tq=tq, lam_init=lam_init),
        grid=(b, nh),
        in_specs=[
            pl.BlockSpec((1, s, hw), lambda bi, h: (bi, 0, h)),
            pl.BlockSpec((1, s, hw), lambda bi, h: (bi, 0, h)),
            pl.BlockSpec((1, s, hw), lambda bi, h: (bi, 0, nh + h)),
            head_vec, head_vec, head_vec,
            vec, vec, vec, vec,
        ],
        out_specs=pl.BlockSpec((1, s, hw), lambda bi, h: (bi, 0, h)),
        out_shape=jax.ShapeDtypeStruct((b, s, width), BF16),
        scratch_shapes=[
            pltpu.VMEM((2, tq, LANES), F32),
            pltpu.VMEM((2, 2, tq, LANES), F32),
            pltpu.VMEM((2, 2, tq, hw), F32),
            pltpu.VMEM((2, 1, LANES), F32),
        ],
        compiler_params=_params("parallel", "parallel"),
        name=name,
    )(q, kv, kv, q_gain.reshape(1, hw), k_gain.reshape(1, hw), subln.reshape(1, hw), lq1.reshape(1, LANES),
      lk1.reshape(1, LANES),
      lq2.reshape(1, LANES), lk2.reshape(1, LANES))


def kernel(x, p, ln_mix, ln_mlp, ln_ple, a_w_in, a_lb, a_onorm, a_w_out, kv_norm, w_k, w_v, k_norm, b_w_q, q_norm, lam_q1, lam_k1, lam_q2, lam_k2, b_subln, b_w_out, mlp_up, mlp_down, ple_proj, ple_gate):
    b, s, d = x.shape
    depth = ln_mix.shape[0]
    n_a = a_w_in.shape[0]
    t = b * s
    x2 = x.reshape(t, d)
    kv = None

    def in_proj_weight(i):
        return (a_w_in, i) if i < n_a else (b_w_q, i - n_a)

    def layer_weights(i):
        w_out = (a_w_out, i) if i < n_a else (b_w_out, i - n_a)
        return [(("w_out", i), w_out), (("up", i), (mlp_up, i)), (("down", i), (mlp_down, i)),
                (("gate", i), (ple_gate, i)), (("proj", i), (ple_proj, i))]

    bf16w = {("in", 0): _cast_bf16(*in_proj_weight(0), name="cast_in_proj_0")}
    for i in range(depth):
        carried = layer_weights(i) if i == 0 else []
        side = [w for _, w in carried]
        if i < n_a:
            j = i
            width = a_w_out.shape[1]
            qig, log_f, *cast = _norm_linear(
                x2, ln_mix[i], (bf16w["in", i],), a_lb,
                sections=(("silu", 0, 0, 0), ("plain", 0, 0, 2), ("silu", 0, 0, 3), ("log_forget", 1, 0, 1)),
                aux_section=3, sec_width=width, out_dtypes=(BF16, F32), lb_rows=i + 1, side=side,
                name=f"hgrn_in_{i}")
            o = _hgrn(qig.reshape(b, s, 3 * width), log_f.reshape(b, s, width), a_onorm[j], name=f"hgrn_{i}")
        else:
            j = i - n_a
            width = b_w_q.shape[2]
            q_scale = math.log2(math.e) / math.sqrt(LANES)
            head_gain = jnp.tile(q_norm[j].reshape(1, -1), (1, width // (2 * LANES)))
            q, *cast = _norm_linear(x2, ln_mix[i], (bf16w["in", i],), head_gain,
                                    sections=(("head_rms", 0, 0, 0),), aux_section=0, sec_width=width,
                                    out_dtypes=(BF16,), scale=q_scale, side=side, name=f"attn_q_{i}")
            lam_init = 0.8 - 0.6 * math.exp(-0.3 * i)
            o = _diff_attention(q.reshape(b, s, width), kv, q_norm[j] * q_scale, k_norm, b_subln[j], lam_q1[j],
                                lam_k1[j], lam_q2[j], lam_k2[j], lam_init=lam_init, name=f"attn_{i}")
        bf16w.update({key: w for (key, _), w in zip(carried, cast)})

        carried = []
        if i == n_a - 1:
            carried += [(("k", 0), (w_k[None], 0)), (("v", 0), (w_v[None], 0))]
        if i + 1 < depth:
            carried += [(("in", i + 1), in_proj_weight(i + 1))] + layer_weights(i + 1)
        x2, *cast = _mixer_out_mlp(x2, o.reshape(t, width), bf16w["w_out", i], ln_mlp[i], bf16w["up", i],
                                   bf16w["down", i], side=[w for _, w in carried], name=f"mlp_{i}")
        bf16w.update({key: w for (key, _), w in zip(carried, cast)})
        x2 = _ple(x2, ln_ple[i], p.reshape(depth, t, -1), i, bf16w["gate", i], bf16w["proj", i], name=f"ple_{i}")
        if i == n_a - 1:
            width = w_k.shape[1]
            head_gain = jnp.tile(k_norm.reshape(1, -1), (1, width // (2 * LANES)))
            (kv,) = _norm_linear(x2, kv_norm, (bf16w["k", 0], bf16w["v", 0]), head_gain,
                                 sections=(("head_rms", 0, 0, 0), ("plain", 0, 1, 0)), aux_section=0,
                                 sec_width=width, out_dtypes=(BF16,), name="shared_kv")
            kv = kv.reshape(b, s, 2 * width)
    return x2.reshape(b, s, d)
```

```python
import functools
import math

import numpy as np
import jax
import jax.numpy as jnp
from jax import lax
from jax.experimental import pallas as pl
from jax.experimental.pallas import tpu as pltpu

EPS = 1e-6
LANES = 128
HGRN_CHUNK = 128
HGRN_UNROLL = 8
HGRN_SAFE_LOG2_RANGE = 100.0
CAST_BLOCK_ELEMS = 2 * 1024 * 1024
ROW_CHUNK = 256
SAFE_LOG2_SHIFT = 60.0
BOUND_SLACK = 1.01
BF16_TILE_ROWS = 16
F32 = jnp.float32
BF16 = jnp.bfloat16
VMEM_LIMIT_BYTES = 56 * 1024 * 1024

_NT = (((1,), (1,)), ((), ()))
_TN = (((0,), (0,)), ((), ()))


def _params(*sem, **extra):
    return pltpu.CompilerParams(dimension_semantics=sem, vmem_limit_bytes=VMEM_LIMIT_BYTES, **extra)


def _rms_bf16(x, gain):
    ms = jnp.mean(x * x, axis=-1, keepdims=True)
    return (x * lax.rsqrt(ms + EPS) * gain).astype(BF16)


def _silu(x):
    return x * jax.nn.sigmoid(x)


def _group_rms(y, gain, scale):
    outs = []
    for c in range(y.shape[1] // LANES):
        sl = slice(c * LANES, (c + 1) * LANES)
        yc = y[:, sl]
        ms = jnp.mean(yc * yc, axis=-1, keepdims=True)
        outs.append(yc * lax.rsqrt(ms + EPS) * (gain[:, sl] * scale))
    return jnp.concatenate(outs, axis=1)


def _aligned(start, multiple):
    return start if isinstance(start, int) else pl.multiple_of(start, multiple)


def _row_chunks(tm):
    rc = min(ROW_CHUNK, tm)
    return [slice(r * rc, (r + 1) * rc) for r in range(tm // rc)]


def _side_cast_plan(side, n_steps):
    plans = []
    for w, _ in side:
        r = w.shape[1]
        nb = next(n for n in range(min(n_steps, r // BF16_TILE_ROWS), 0, -1)
                  if r % n == 0 and (r // n) % BF16_TILE_ROWS == 0)
        plans.append((r // nb, nb))
    return plans


def _side_cast_specs(side, plans, step_of):
    in_specs, out_specs, out_shapes = [], [], []
    for (w, layer), (br, nb) in zip(side, plans):
        _, r, c = w.shape
        in_specs.append(pl.BlockSpec((None, br, c),
                                     lambda *g, layer=layer, nb=nb: (layer, jnp.minimum(step_of(*g), nb - 1), 0)))
        out_specs.append(pl.BlockSpec((br, c), lambda *g, nb=nb: (jnp.minimum(step_of(*g), nb - 1), 0)))
        out_shapes.append(jax.ShapeDtypeStruct((r, c), BF16))
    return in_specs, out_specs, out_shapes


def _run_side_casts(in_refs, out_refs):
    for i_ref, o_ref in zip(in_refs, out_refs):
        o_ref[...] = i_ref[...].astype(o_ref.dtype)


def _norm_linear_kernel(x_ref, g_ref, aux_ref, *rest, sections, n_weights, n_side, sec_blocks, lb_rows, scale):
    n_out = len(rest) - 1 - n_weights - 2 * n_side
    w_refs, side_in = rest[:n_weights], rest[n_weights:n_weights + n_side]
    o_refs = rest[n_weights + n_side:n_weights + n_side + n_out]
    side_out, h_scr = rest[n_weights + n_side + n_out:-1], rest[-1]
    j = pl.program_id(1)

    def epilogue(name, acc):
        if name == "silu":
            return _silu(acc)
        if name == "plain":
            return acc
        if name == "head_rms":
            return _group_rms(acc, aux_ref[...], scale)
        if name == "log_forget":
            a = aux_ref[...]
            e = jnp.exp(a - jnp.max(a, axis=0, keepdims=True))
            lb = jnp.sum(e[:lb_rows], axis=0, keepdims=True) / jnp.sum(e, axis=0, keepdims=True)
            return jnp.log2(lb + (1.0 - lb) * jax.nn.sigmoid(acc))
        raise ValueError(name)

    def run(first, name, o_ref, w_ref):
        _run_side_casts(side_in, side_out)
        for rows in _row_chunks(x_ref.shape[0]):
            if first:
                h = _rms_bf16(x_ref[rows, :], g_ref[...])
                h_scr[rows, :] = h
            else:
                h = h_scr[rows, :]
            acc = jnp.dot(h, w_ref[...], preferred_element_type=F32)
            o_ref[rows, :] = epilogue(name, acc).astype(o_ref.dtype)

    for s, (name, out_idx, w_idx, _) in enumerate(sections):
        lo, hi = s * sec_blocks, (s + 1) * sec_blocks
        if s == 0:
            pl.when(j == 0)(functools.partial(run, True, name, o_refs[out_idx], w_refs[w_idx]))
            lo = 1
        if lo < hi:
            pl.when(jnp.logical_and(j >= lo, j < hi))(
                functools.partial(run, False, name, o_refs[out_idx], w_refs[w_idx]))


def _step_table(j, table):
    out = table[0]
    for jj in range(1, len(table)):
        if table[jj] != table[jj - 1]:
            out = jnp.where(j >= jj, table[jj], out)
    return out


def _norm_linear(x2, gain, ws, aux, *, sections, aux_section, sec_width, out_dtypes, name, lb_rows=1, scale=1.0,
                 side=(), tm=1024, tn=1024):
    t, k = x2.shape
    tm, tn = min(tm, t), min(tn, sec_width)
    sb = sec_width // tn
    n_steps = len(sections) * sb
    plans = _side_cast_plan(side, (t // tm) * n_steps)
    side_in, side_out, side_shapes = _side_cast_specs(side, plans, lambda i, j: i * n_steps + j)
    first_step = [None] * len(out_dtypes)
    n_blocks = [0] * len(out_dtypes)
    for s, (_, o, _, _) in enumerate(sections):
        if first_step[o] is None:
            first_step[o] = s * sb
        assert s * sb == first_step[o] + n_blocks[o], "an output's sections must be consecutive"
        n_blocks[o] += sb

    def out_spec(o):
        return pl.BlockSpec((tm, tn), lambda i, j: (i, jnp.clip(j - first_step[o], 0, n_blocks[o] - 1)))

    def weight_spec(w_idx):
        cols = [None] * n_steps
        for s, (_, _, wi, w_sec) in enumerate(sections):
            if wi == w_idx:
                for r in range(sb):
                    cols[s * sb + r] = w_sec * sb + r
        held = next(c for c in reversed(cols) if c is not None)
        for jj in range(n_steps):
            if cols[jj] is None:
                cols[jj] = held
            held = cols[jj]
        return pl.BlockSpec((k, tn), lambda i, j: (0, _step_table(j, cols)))

    kern = functools.partial(_norm_linear_kernel, sections=tuple(sections), n_weights=len(ws), n_side=len(side),
                             sec_blocks=sb, lb_rows=lb_rows, scale=scale)
    return pl.pallas_call(
        kern,
        grid=(t // tm, n_steps),
        in_specs=[
            pl.BlockSpec((tm, k), lambda i, j: (i, 0)),
            pl.BlockSpec((1, k), lambda i, j: (0, 0)),
            pl.BlockSpec((aux.shape[0], tn), lambda i, j: (0, jnp.clip(j - aux_section * sb, 0, sb - 1))),
        ] + [weight_spec(wi) for wi in range(len(ws))] + side_in,
        out_specs=[out_spec(o) for o in range(len(out_dtypes))] + side_out,
        out_shape=[jax.ShapeDtypeStruct((t, n_blocks[o] * tn), out_dtypes[o])
                   for o in range(len(out_dtypes))] + side_shapes,
        scratch_shapes=[pltpu.VMEM((tm, k), BF16)],
        compiler_params=_params("arbitrary" if side else "parallel", "arbitrary"),
        name=name,
    )(x2, gain.reshape(1, k), aux, *ws, *[w for w, _ in side])


def _cast_kernel(w_ref, o_ref):
    o_ref[...] = w_ref[...].astype(o_ref.dtype)


def _cast_bf16(w, layer, *, name):
    _, r, c = w.shape
    tc = min(c, 2048)
    tr = min(r, max(8, CAST_BLOCK_ELEMS // tc))
    return pl.pallas_call(
        _cast_kernel,
        grid=(r // tr, c // tc),
        in_specs=[pl.BlockSpec((None, tr, tc), lambda i, j: (layer, i, j))],
        out_specs=pl.BlockSpec((tr, tc), lambda i, j: (i, j)),
        out_shape=jax.ShapeDtypeStruct((r, c), BF16),
        compiler_params=_params("parallel", "parallel"),
        name=name,
    )(w)


def _mixer_out_mlp_kernel(x_ref, a_ref, wo_ref, g_ref, up_ref, down_ref, *rest, n_side):
    side_in, o_ref, side_out, h_scr = rest[:n_side], rest[n_side], rest[n_side + 1:-1], rest[-1]

    def mixer_out():
        _run_side_casts(side_in, side_out)
        for rows in _row_chunks(x_ref.shape[0]):
            x1 = x_ref[rows, :] + jnp.dot(a_ref[rows, :], wo_ref[...], preferred_element_type=F32)
            o_ref[rows, :] = x1
            h_scr[rows, :] = _rms_bf16(x1, g_ref[...])

    def mlp_tile():
        _run_side_casts(side_in, side_out)
        for rows in _row_chunks(x_ref.shape[0]):
            u = jnp.dot(h_scr[rows, :], up_ref[...], preferred_element_type=F32)
            a = jnp.square(jnp.maximum(u, 0.0)).astype(BF16)
            o_ref[rows, :] += jnp.dot(a, down_ref[...], preferred_element_type=F32)

    j = pl.program_id(1)
    pl.when(j == 0)(mixer_out)
    pl.when(j > 0)(mlp_tile)


def _mixer_out_mlp(x2, a, w_out, gain, up, down, *, name, side=(), tm=512, tf=1024):
    t, d = x2.shape
    ka = a.shape[1]
    f = up.shape[1]
    tm, tf = min(tm, t), min(tf, f)
    n_inner = 1 + f // tf
    plans = _side_cast_plan(side, (t // tm) * n_inner)
    side_in, side_out, side_shapes = _side_cast_specs(side, plans, lambda i, j: i * n_inner + j)
    return pl.pallas_call(
        functools.partial(_mixer_out_mlp_kernel, n_side=len(side)),
        grid=(t // tm, n_inner),
        in_specs=[
            pl.BlockSpec((tm, d), lambda i, j: (i, 0)),
            pl.BlockSpec((tm, ka), lambda i, j: (i, 0)),
            pl.BlockSpec((ka, d), lambda i, j: (0, 0), pipeline_mode=pl.Buffered(1)),
            pl.BlockSpec((1, d), lambda i, j: (0, 0)),
            pl.BlockSpec((d, tf), lambda i, j: (0, jnp.maximum(j - 1, 0))),
            pl.BlockSpec((tf, d), lambda i, j: (jnp.maximum(j - 1, 0), 0)),
        ] + side_in,
        out_specs=[pl.BlockSpec((tm, d), lambda i, j: (i, 0))] + side_out,
        out_shape=[jax.ShapeDtypeStruct((t, d), F32)] + side_shapes,
        scratch_shapes=[pltpu.VMEM((tm, d), BF16)],
        compiler_params=_params("arbitrary" if side else "parallel", "arbitrary"),
        name=name,
    )(x2, a, w_out, gain.reshape(1, d), up, down, *[w for w, _ in side])


def _ple_kernel(x_ref, g_ref, p_ref, gw_ref, pw_ref, o_ref, h_scr, *, tn):
    j = pl.program_id(1)
    cols = pl.ds(pl.multiple_of(j * tn, tn), tn)

    def run(first):
        for rows in _row_chunks(x_ref.shape[0]):
            if first:
                h = _rms_bf16(x_ref[rows, :], g_ref[...])
                h_scr[rows, :] = h
            else:
                h = h_scr[rows, :]
            gate = jax.nn.sigmoid(jnp.dot(h, gw_ref[...], preferred_element_type=F32))
            emb = jnp.dot(p_ref[rows, :].astype(BF16), pw_ref[...], preferred_element_type=F32)
            o_ref[rows, :] = x_ref[rows, cols] + gate * emb

    pl.when(j == 0)(lambda: run(True))
    pl.when(j > 0)(lambda: run(False))


def _ple(x2, gain, p3, layer, gate_w, proj_w, *, name, tm=1024, tn=1024):
    t, d = x2.shape
    pd = p3.shape[2]
    tm, tn = min(tm, t), min(tn, d)
    return pl.pallas_call(
        functools.partial(_ple_kernel, tn=tn),
        grid=(t // tm, d // tn),
        in_specs=[
            pl.BlockSpec((tm, d), lambda i, j: (i, 0)),
            pl.BlockSpec((1, d), lambda i, j: (0, 0)),
            pl.BlockSpec((None, tm, pd), lambda i, j: (layer, i, 0)),
            pl.BlockSpec((d, tn), lambda i, j: (0, j)),
            pl.BlockSpec((pd, tn), lambda i, j: (0, j)),
        ],
        out_specs=pl.BlockSpec((tm, tn), lambda i, j: (i, j)),
        out_shape=jax.ShapeDtypeStruct((t, d), F32),
        scratch_shapes=[pltpu.VMEM((tm, d), BF16)],
        compiler_params=_params("parallel", "arbitrary"),
        name=name,
    )(x2, gain.reshape(1, d), p3, gate_w, proj_w)


def _hgrn_levels(c):
    return [c >> (l + 1) for l in range(int(math.log2(c)))]


def _hgrn_constants(c):
    idx = np.arange(c)
    x = idx[:, None] ^ idx[None, :]
    masks = [((x >= m) & (x < 2 * m) & ((idx[:, None] & m) != 0)).astype(np.float32) for m in _hgrn_levels(c)]
    return np.tril(np.ones((c, c), np.float32)), np.stack(masks, 0)


def _level_log_decay(g_cum, g, m):
    c, w = g_cum.shape
    if m == 1:
        odd = (lax.broadcasted_iota(jnp.int32, (c, w), 0) & 1) != 0
        return jnp.where(odd, g, 0.0)
    if m < 8:
        g3 = g_cum.reshape(c // 8, 8, w)
        sub = lax.broadcasted_iota(jnp.int32, g3.shape, 1)
        if m == 4:
            mid = g3[:, 3:4, :]
        else:
            mid = jnp.where(sub < 4, g3[:, 1:2, :], g3[:, 5:6, :])
        return (-jnp.abs(g3 - mid)).reshape(c, w)
    g3 = g_cum.reshape(c // (2 * m), 2 * m, w)
    mid = g3[:, m - 1:m, :]
    return jnp.concatenate([mid - g3[:, :m, :], g3[:, m:, :] - mid], axis=1).reshape(c, w)


def _pair_rows(lower, upper, m):
    c, w = lower.shape
    if m < 8:
        return jnp.where((lax.broadcasted_iota(jnp.int32, (c, w), 0) & m) != 0, lower, upper)
    lo3 = lower.reshape(c // (2 * m), 2 * m, w)
    up3 = upper.reshape(c // (2 * m), 2 * m, w)
    return jnp.concatenate([up3[:, :m, :], lo3[:, m:, :]], axis=1).reshape(c, w)


def _hgrn_kernel(q_ref, f_ref, v_ref, gate_ref, tri_ref, msk_ref, on_ref, *rest, c, hb, nchunks, n_side):
    side_in, o_ref, side_out, st_scr = rest[:n_side], rest[n_side], rest[n_side + 1:-1], rest[-1]

    @pl.when(pl.program_id(2) == 0)
    def _():
        st_scr[...] = jnp.zeros_like(st_scr)

    levels = _hgrn_levels(c)
    hw = hb * LANES

    def chunk_decay(ci, worst):
        g = f_ref[0, pl.ds(pl.multiple_of(ci * c, c), c), :]
        halves = jnp.minimum(jnp.sum(g[:c // 2], axis=0, keepdims=True), jnp.sum(g[c // 2:], axis=0, keepdims=True))
        return jnp.minimum(worst, halves)

    worst = lax.fori_loop(0, nchunks, chunk_decay, jnp.zeros((1, hw), F32))
    mild = jnp.min(worst) > -HGRN_SAFE_LOG2_RANGE

    def chunk(ci, mild_decay):
        rows = pl.ds(pl.multiple_of(ci * c, c), c)
        g_all = f_ref[0, rows, :]
        g_hi = g_all.astype(BF16)
        g_lo = (g_all - g_hi.astype(F32)).astype(BF16)
        tri = tri_ref[...]
        g_cum_all = (jnp.dot(tri, g_hi, preferred_element_type=F32)
                     + jnp.dot(tri, g_lo, preferred_element_type=F32))
        k_all = 1.0 - jnp.exp2(g_all)
        g_last = g_cum_all[c - 1:c, :]
        if mild_decay:
            g_mid = g_cum_all[c // 2 - 1:c // 2, :]
            q_mid_all = jnp.exp2(g_cum_all - g_mid)
            k_mid_all = jnp.exp2(g_mid - g_cum_all)
            to_mid_all = jnp.exp2(g_mid)
            mid_to_end_all = jnp.exp2(g_last - g_mid)
            causal = (lax.broadcasted_iota(jnp.int32, (c, c), 0) >= lax.broadcasted_iota(jnp.int32, (c, c), 1))
        else:
            q_dec_all = jnp.exp2(g_cum_all)
            k_dec_all = jnp.exp2(g_last - g_cum_all)
            st_dec_all = jnp.exp2(g_last)
            lvl_dec_all = [jnp.exp2(_level_log_decay(g_cum_all, g_all, m)) for m in levels]
        for h in range(hb):
            hl = slice(h * LANES, (h + 1) * LANES)
            qs = q_ref[0, rows, hl].astype(F32)
            vb = v_ref[0, rows, hl]
            k = k_all[:, hl]
            st = st_scr[h]
            if mild_decay:
                qm = (qs * q_mid_all[:, hl]).astype(BF16)
                km = (k * k_mid_all[:, hl]).astype(BF16)
                a = jnp.where(causal, lax.dot_general(qm, km, _NT, preferred_element_type=F32), 0.0)
                st_mid = st * to_mid_all[:, hl]
                o = (jnp.dot(a.astype(BF16), vb, preferred_element_type=F32)
                     + lax.dot_general(qm, st_mid.astype(BF16), _NT, preferred_element_type=F32))
                st_scr[h] = ((st_mid + lax.dot_general(vb, km, _TN, preferred_element_type=F32))
                             * mid_to_end_all[:, hl])
            else:
                a = jnp.zeros((c, c), F32)
                for l, m in enumerate(levels):
                    x = (_pair_rows(qs, k, m) * lvl_dec_all[l][:, hl]).astype(BF16)
                    a = a + msk_ref[l] * lax.dot_general(x, x, _NT, preferred_element_type=F32)
                o = jnp.dot(a.astype(BF16), vb, preferred_element_type=F32)
                o = o + jnp.sum(qs * k, axis=-1, keepdims=True) * vb.astype(F32)
                qg = (qs * q_dec_all[:, hl]).astype(BF16)
                o = o + lax.dot_general(qg, st.astype(BF16), _NT, preferred_element_type=F32)
                kt = (k * k_dec_all[:, hl]).astype(BF16)
                st_scr[h] = st * st_dec_all[:, hl] + lax.dot_general(vb, kt, _TN, preferred_element_type=F32)
            ms = jnp.mean(o * o, axis=-1, keepdims=True)
            y = o * lax.rsqrt(ms + EPS) * on_ref[...] * gate_ref[0, rows, hl].astype(F32)
            o_ref[0, rows, hl] = y.astype(o_ref.dtype)

    unroll = HGRN_UNROLL if nchunks % HGRN_UNROLL == 0 else 1

    def all_chunks(mild_decay):
        def trip(u, carry):
            _run_side_casts(side_in, side_out)
            for r in range(unroll):
                chunk(u * unroll + r, mild_decay)
            return carry

        lax.fori_loop(0, nchunks // unroll, trip, 0)

    pl.when(mild)(lambda: all_chunks(True))
    pl.when(jnp.logical_not(mild))(lambda: all_chunks(False))


def _hgrn(qig, log_f, onorm, *, name, side=(), rows=1024, hb=4):
    b, s, width = log_f.shape
    c = HGRN_CHUNK
    rows = min(rows, s)
    hw = hb * LANES
    nsb = width // hw
    nr = s // rows
    tri, masks = _hgrn_constants(c)
    nl = masks.shape[0]
    plans = _side_cast_plan(side, b * nsb * nr)
    side_in, side_out, side_shapes = _side_cast_specs(side, plans, lambda bi, hg, r: (bi * nsb + hg) * nr + r)

    def sec(k):
        return pl.BlockSpec((1, rows, hw), lambda bi, hg, r: (bi, r, k * nsb + hg))

    kern = functools.partial(_hgrn_kernel, c=c, hb=hb, nchunks=rows // c, n_side=len(side))
    outer = "arbitrary" if side else "parallel"
    return pl.pallas_call(
        kern,
        grid=(b, nsb, nr),
        in_specs=[
            sec(0), sec(0), sec(1), sec(2),
            pl.BlockSpec((c, c), lambda bi, hg, r: (0, 0)),
            pl.BlockSpec((nl, c, c), lambda bi, hg, r: (0, 0, 0)),
            pl.BlockSpec((1, LANES), lambda bi, hg, r: (0, 0)),
        ] + side_in,
        out_specs=[pl.BlockSpec((1, rows, hw), lambda bi, hg, r: (bi, r, hg))] + side_out,
        out_shape=[jax.ShapeDtypeStruct((b, s, width), BF16)] + side_shapes,
        scratch_shapes=[pltpu.VMEM((hb, LANES, LANES), F32)],
        compiler_params=_params(outer, outer, "arbitrary"),
        name=name,
    )(qig, log_f, qig, qig, jnp.asarray(tri, BF16), jnp.asarray(masks, F32), onorm.reshape(1, LANES),
      *[w for w, _ in side])


def _attn_kernel(q_ref, k_ref, v_ref, qg_ref, kg_ref, sub_ref, lq1_ref, lk1_ref, lq2_ref, lk2_ref, o_ref,
                 m_scr, l_scr, acc_scr, shift_scr, *, tq, lam_init):
    hw = acc_scr.shape[-1]
    nq = q_ref.shape[1] // tq
    maps = [slice(c * LANES, (c + 1) * LANES) for c in range(2)]

    bounds = []
    for c, sl in enumerate(maps):
        q_max = jnp.max(jnp.abs(qg_ref[:, sl]), axis=-1, keepdims=True)
        k_max = jnp.max(jnp.abs(kg_ref[:, sl]), axis=-1, keepdims=True)
        bounds.append(BOUND_SLACK * LANES * q_max * k_max)
        shift_scr[c] = jnp.broadcast_to(bounds[c], (1, LANES))
    bounded = jnp.max(jnp.maximum(bounds[0], bounds[1])) < SAFE_LOG2_SHIFT

    def causal(shape, row0):
        row = lax.broadcasted_iota(jnp.int32, shape, 0) + row0
        return lax.broadcasted_iota(jnp.int32, shape, 1) <= row

    def block(qi, slot, j, masked, running_max, first=False):
        for qrows in _row_chunks(tq):
            nk = qrows.stop if masked else tq
            keys = pl.ds(_aligned(j * tq, tq), nk)
            qsel = pl.ds(_aligned(qi * tq + qrows.start, qrows.stop - qrows.start), qrows.stop - qrows.start)
            vb = v_ref[0, keys, :]
            for c, sl in enumerate(maps):
                s = lax.dot_general(q_ref[0, qsel, sl], k_ref[0, keys, sl], _NT,
                                    preferred_element_type=F32)
                if running_max:
                    if masked:
                        s = jnp.where(causal(s.shape, qrows.start), s, -jnp.inf)
                    m_prev = m_scr[c, qrows, :]
                    m_new = jnp.maximum(m_prev, jnp.max(s, axis=-1, keepdims=True))
                    alpha = jnp.exp2(m_prev - m_new)
                    p = jnp.exp2(s - jnp.tile(m_new, (1, nk // LANES)))
                    l_scr[slot, c, qrows, :] = (alpha * l_scr[slot, c, qrows, :]
                                                + jnp.sum(p, axis=-1, keepdims=True))
                    acc_scr[slot, c, qrows, :] = (jnp.tile(alpha, (1, hw // LANES)) * acc_scr[slot, c, qrows, :]
                                                  + jnp.dot(p.astype(BF16), vb, preferred_element_type=F32))
                    m_scr[c, qrows, :] = m_new
                else:
                    p = jnp.exp2(s - jnp.tile(shift_scr[c], (1, nk // LANES)))
                    if masked:
                        p = jnp.where(causal(p.shape, qrows.start), p, 0.0)
                    lane_sums = p[:, 0:LANES]
                    for g in range(1, nk // LANES):
                        lane_sums = lane_sums + p[:, g * LANES:(g + 1) * LANES]
                    pv = jnp.dot(p.astype(BF16), vb, preferred_element_type=F32)
                    if first:
                        l_scr[slot, c, qrows, :] = lane_sums
                        acc_scr[slot, c, qrows, :] = pv
                    else:
                        l_scr[slot, c, qrows, :] += lane_sums
                        acc_scr[slot, c, qrows, :] += pv

    lam = (jnp.exp(jnp.sum(lq1_ref[...] * lk1_ref[...], axis=-1, keepdims=True))
           - jnp.exp(jnp.sum(lq2_ref[...] * lk2_ref[...], axis=-1, keepdims=True))
           + lam_init)

    def finalize(qi, slot, lanes_summed):
        ls = [l_scr[slot, c] for c in range(2)]
        if not lanes_summed:
            ls = [jnp.sum(l, axis=-1, keepdims=True) for l in ls]
            invs = [jnp.broadcast_to(1.0 / l, (tq, hw)) for l in ls]
        else:
            invs = [jnp.tile(1.0 / l, (1, hw // LANES)) for l in ls]
        o = acc_scr[slot, 0] * invs[0] - lam * (acc_scr[slot, 1] * invs[1])
        ms = jnp.mean(o * o, axis=-1, keepdims=True)
        o_ref[0, pl.ds(_aligned(qi * tq, tq), tq), :] = (o * lax.rsqrt(ms + EPS)
                                           * (sub_ref[...] * (1.0 - lam_init))).astype(o_ref.dtype)

    @pl.when(bounded)
    def _():
        for qi in range(nq):
            slot = qi % 2
            block(qi, slot, qi, True, False, first=True)
            for j in range(qi):
                block(qi, slot, j, False, False)
            finalize(qi, slot, lanes_summed=False)

    @pl.when(jnp.logical_not(bounded))
    def _():
        def query_block(qi, carry):
            m_scr[...] = jnp.full_like(m_scr, -1e30)
            l_scr[0] = jnp.zeros_like(l_scr[0])
            acc_scr[0] = jnp.zeros_like(acc_scr[0])

            def body(j, c2):
                block(qi, 0, j, False, True)
                return c2

            lax.fori_loop(0, qi, body, 0)
            block(qi, 0, qi, True, True)
            finalize(qi, 0, lanes_summed=True)
            return carry

        lax.fori_loop(0, nq, query_block, 0)


def _diff_attention(q, kv, q_gain, k_gain, subln, lq1, lk1, lq2, lk2, *, lam_init, name, tq=1024):
    b, s, width = q.shape
    hw = 2 * LANES
    nh = width // hw
    tq = min(tq, s)
    vec = pl.BlockSpec((1, LANES), lambda bi, h: (0, 0))
    head_vec = pl.BlockSpec((1, hw), lambda bi, h: (0, 0))
    return pl.pallas_call(
        functools.partial(_attn_kernel, tq=tq, lam_init=lam_init),
        grid=(b, nh),
        in_specs=[
            pl.BlockSpec((1, s, hw), lambda bi, h: (bi, 0, h)),
            pl.BlockSpec((1, s, hw), lambda bi, h: (bi, 0, h)),
            pl.BlockSpec((1, s, hw), lambda bi, h: (bi, 0, nh + h)),
            head_vec, head_vec, head_vec,
            vec, vec, vec, vec,
        ],
        out_specs=pl.BlockSpec((1, s, hw), lambda bi, h: (bi, 0, h)),
        out_shape=jax.ShapeDtypeStruct((b, s, width), BF16),
        scratch_shapes=[
            pltpu.VMEM((2, tq, LANES), F32),
            pltpu.VMEM((2, 2, tq, LANES), F32),
            pltpu.VMEM((2, 2, tq, hw), F32),
            pltpu.VMEM((2, 1, LANES), F32),
        ],
        compiler_params=_params("parallel", "parallel"),
        name=name,
    )(q, kv, kv, q_gain.reshape(1, hw), k_gain.reshape(1, hw), subln.reshape(1, hw), lq1.reshape(1, LANES),
      lk1.reshape(1, LANES),
      lq2.reshape(1, LANES), lk2.reshape(1, LANES))


def kernel(x, p, ln_mix, ln_mlp, ln_ple, a_w_in, a_lb, a_onorm, a_w_out, kv_norm, w_k, w_v, k_norm, b_w_q, q_norm, lam_q1, lam_k1, lam_q2, lam_k2, b_subln, b_w_out, mlp_up, mlp_down, ple_proj, ple_gate):
    b, s, d = x.shape
    depth = ln_mix.shape[0]
    n_a = a_w_in.shape[0]
    t = b * s
    x2 = x.reshape(t, d)
    kv = None

    def in_proj_weight(i):
        return (a_w_in, i) if i < n_a else (b_w_q, i - n_a)

    def layer_weights(i):
        w_out = (a_w_out, i) if i < n_a else (b_w_out, i - n_a)
        return [(("w_out", i), w_out), (("up", i), (mlp_up, i)), (("down", i), (mlp_down, i)),
                (("gate", i), (ple_gate, i)), (("proj", i), (ple_proj, i))]

    bf16w = {("in", 0): _cast_bf16(*in_proj_weight(0), name="cast_in_proj_0")}

    def pending(i, include_own):
        wanted = layer_weights(i) if include_own else []
        if i == n_a - 1:
            wanted += [(("k", 0), (w_k[None], 0)), (("v", 0), (w_v[None], 0))]
        if i + 1 < depth:
            wanted += [(("in", i + 1), in_proj_weight(i + 1))] + layer_weights(i + 1)
        return [(key, w) for key, w in wanted if key not in bf16w]

    for i in range(depth):
        if i < n_a:
            j = i
            width = a_w_out.shape[1]
            qig, log_f = _norm_linear(
                x2, ln_mix[i], (bf16w["in", i],), a_lb,
                sections=(("silu", 0, 0, 0), ("plain", 0, 0, 2), ("silu", 0, 0, 3), ("log_forget", 1, 0, 1)),
                aux_section=3, sec_width=width, out_dtypes=(BF16, F32), lb_rows=i + 1, name=f"hgrn_in_{i}")
            carried = pending(i, include_own=True)
            o, *cast = _hgrn(qig.reshape(b, s, 3 * width), log_f.reshape(b, s, width), a_onorm[j],
                             side=[w for _, w in carried], name=f"hgrn_{i}")
            bf16w.update({key: w for (key, _), w in zip(carried, cast)})
        else:
            j = i - n_a
            width = b_w_q.shape[2]
            q_scale = math.log2(math.e) / math.sqrt(LANES)
            head_gain = jnp.tile(q_norm[j].reshape(1, -1), (1, width // (2 * LANES)))
            (q,) = _norm_linear(x2, ln_mix[i], (bf16w["in", i],), head_gain,
                                sections=(("head_rms", 0, 0, 0),), aux_section=0, sec_width=width,
                                out_dtypes=(BF16,), scale=q_scale, name=f"attn_q_{i}")
            lam_init = 0.8 - 0.6 * math.exp(-0.3 * i)
            o = _diff_attention(q.reshape(b, s, width), kv, q_norm[j] * q_scale, k_norm, b_subln[j], lam_q1[j],
                                lam_k1[j], lam_q2[j], lam_k2[j], lam_init=lam_init, name=f"attn_{i}")
            for key, (w, layer) in layer_weights(i):
                if key not in bf16w:
                    bf16w[key] = _cast_bf16(w, layer, name=f"cast_{key[0]}_{i}")

        carried = pending(i, include_own=False)
        x2, *cast = _mixer_out_mlp(x2, o.reshape(t, width), bf16w["w_out", i], ln_mlp[i], bf16w["up", i],
                                   bf16w["down", i], side=[w for _, w in carried], name=f"mlp_{i}")
        bf16w.update({key: w for (key, _), w in zip(carried, cast)})
        x2 = _ple(x2, ln_ple[i], p.reshape(depth, t, -1), i, bf16w["gate", i], bf16w["proj", i], name=f"ple_{i}")
        if i == n_a - 1:
            width = w_k.shape[1]
            head_gain = jnp.tile(k_norm.reshape(1, -1), (1, width // (2 * LANES)))
            (kv,) = _norm_linear(x2, kv_norm, (bf16w["k", 0], bf16w["v", 0]), head_gain,
                                 sections=(("head_rms", 0, 0, 0), ("plain", 0, 1, 0)), aux_section=0,
                                 sec_width=width, out_dtypes=(BF16,), name="shared_kv")
            kv = kv.reshape(b, s, 2 * width)
    return x2.reshape(b, s, d)
```

```python
import functools
import math

import numpy as np
import jax
import jax.numpy as jnp
from jax import lax
from jax.experimental import pallas as pl
from jax.experimental.pallas import tpu as pltpu

EPS = 1e-6
LANES = 128
HGRN_CHUNK = 128
HGRN_UNROLL = 8
HGRN_SAFE_LOG2_RANGE = 100.0
CAST_BLOCK_ELEMS = 2 * 1024 * 1024
ROW_CHUNK = 256
SAFE_LOG2_SHIFT = 60.0
BOUND_SLACK = 1.01
BF16_TILE_ROWS = 16
F32 = jnp.float32
BF16 = jnp.bfloat16
VMEM_LIMIT_BYTES = 56 * 1024 * 1024

_NT = (((1,), (1,)), ((), ()))
_TN = (((0,), (0,)), ((), ()))


def _params(*sem, **extra):
    return pltpu.CompilerParams(dimension_semantics=sem, vmem_limit_bytes=VMEM_LIMIT_BYTES, **extra)


def _rms_bf16(x, gain):
    ms = jnp.mean(x * x, axis=-1, keepdims=True)
    return (x * lax.rsqrt(ms + EPS) * gain).astype(BF16)


def _silu(x):
    return x * jax.nn.sigmoid(x)


def _group_rms(y, gain, scale):
    outs = []
    for c in range(y.shape[1] // LANES):
        sl = slice(c * LANES, (c + 1) * LANES)
        yc = y[:, sl]
        ms = jnp.mean(yc * yc, axis=-1, keepdims=True)
        outs.append(yc * lax.rsqrt(ms + EPS) * (gain[:, sl] * scale))
    return jnp.concatenate(outs, axis=1)


def _aligned(start, multiple):
    return start if isinstance(start, int) else pl.multiple_of(start, multiple)


def _row_chunks(tm):
    rc = min(ROW_CHUNK, tm)
    return [slice(r * rc, (r + 1) * rc) for r in range(tm // rc)]


def _side_cast_plan(side, n_steps):
    plans = []
    for w, _ in side:
        r = w.shape[1]
        nb = next(n for n in range(min(n_steps, r // BF16_TILE_ROWS), 0, -1)
                  if r % n == 0 and (r // n) % BF16_TILE_ROWS == 0)
        plans.append((r // nb, nb))
    return plans


def _side_cast_specs(side, plans, step_of):
    in_specs, out_specs, out_shapes = [], [], []
    for (w, layer), (br, nb) in zip(side, plans):
        _, r, c = w.shape
        in_specs.append(pl.BlockSpec((None, br, c),
                                     lambda *g, layer=layer, nb=nb: (layer, jnp.minimum(step_of(*g), nb - 1), 0)))
        out_specs.append(pl.BlockSpec((br, c), lambda *g, nb=nb: (jnp.minimum(step_of(*g), nb - 1), 0)))
        out_shapes.append(jax.ShapeDtypeStruct((r, c), BF16))
    return in_specs, out_specs, out_shapes


def _run_side_casts(in_refs, out_refs):
    for i_ref, o_ref in zip(in_refs, out_refs):
        o_ref[...] = i_ref[...].astype(o_ref.dtype)


def _norm_linear_kernel(x_ref, g_ref, aux_ref, *rest, sections, n_weights, n_side, sec_blocks, lb_rows, scale):
    n_out = len(rest) - 1 - n_weights - 2 * n_side
    w_refs, side_in = rest[:n_weights], rest[n_weights:n_weights + n_side]
    o_refs = rest[n_weights + n_side:n_weights + n_side + n_out]
    side_out, h_scr = rest[n_weights + n_side + n_out:-1], rest[-1]
    j = pl.program_id(1)

    def epilogue(name, acc):
        if name == "silu":
            return _silu(acc)
        if name == "plain":
            return acc
        if name == "head_rms":
            return _group_rms(acc, aux_ref[...], scale)
        if name == "log_forget":
            a = aux_ref[...]
            e = jnp.exp(a - jnp.max(a, axis=0, keepdims=True))
            lb = jnp.sum(e[:lb_rows], axis=0, keepdims=True) / jnp.sum(e, axis=0, keepdims=True)
            return jnp.log2(lb + (1.0 - lb) * jax.nn.sigmoid(acc))
        raise ValueError(name)

    def run(first, name, o_ref, w_ref):
        _run_side_casts(side_in, side_out)
        for rows in _row_chunks(x_ref.shape[0]):
            if first:
                h = _rms_bf16(x_ref[rows, :], g_ref[...])
                h_scr[rows, :] = h
            else:
                h = h_scr[rows, :]
            acc = jnp.dot(h, w_ref[...], preferred_element_type=F32)
            o_ref[rows, :] = epilogue(name, acc).astype(o_ref.dtype)

    for s, (name, out_idx, w_idx, _) in enumerate(sections):
        lo, hi = s * sec_blocks, (s + 1) * sec_blocks
        if s == 0:
            pl.when(j == 0)(functools.partial(run, True, name, o_refs[out_idx], w_refs[w_idx]))
            lo = 1
        if lo < hi:
            pl.when(jnp.logical_and(j >= lo, j < hi))(
                functools.partial(run, False, name, o_refs[out_idx], w_refs[w_idx]))


def _step_table(j, table):
    out = table[0]
    for jj in range(1, len(table)):
        if table[jj] != table[jj - 1]:
            out = jnp.where(j >= jj, table[jj], out)
    return out


def _norm_linear(x2, gain, ws, aux, *, sections, aux_section, sec_width, out_dtypes, name, lb_rows=1, scale=1.0,
                 side=(), tm=1024, tn=1024):
    t, k = x2.shape
    tm, tn = min(tm, t), min(tn, sec_width)
    sb = sec_width // tn
    n_steps = len(sections) * sb
    plans = _side_cast_plan(side, (t // tm) * n_steps)
    side_in, side_out, side_shapes = _side_cast_specs(side, plans, lambda i, j: i * n_steps + j)
    first_step = [None] * len(out_dtypes)
    n_blocks = [0] * len(out_dtypes)
    for s, (_, o, _, _) in enumerate(sections):
        if first_step[o] is None:
            first_step[o] = s * sb
        assert s * sb == first_step[o] + n_blocks[o], "an output's sections must be consecutive"
        n_blocks[o] += sb

    def out_spec(o):
        return pl.BlockSpec((tm, tn), lambda i, j: (i, jnp.clip(j - first_step[o], 0, n_blocks[o] - 1)))

    def weight_spec(w_idx):
        cols = [None] * n_steps
        for s, (_, _, wi, w_sec) in enumerate(sections):
            if wi == w_idx:
                for r in range(sb):
                    cols[s * sb + r] = w_sec * sb + r
        held = next(c for c in reversed(cols) if c is not None)
        for jj in range(n_steps):
            if cols[jj] is None:
                cols[jj] = held
            held = cols[jj]
        if len(set(cols)) == 1:
            return pl.BlockSpec((k, tn), lambda i, j: (0, cols[0]), pipeline_mode=pl.Buffered(1))
        return pl.BlockSpec((k, tn), lambda i, j: (0, _step_table(j, cols)))

    kern = functools.partial(_norm_linear_kernel, sections=tuple(sections), n_weights=len(ws), n_side=len(side),
                             sec_blocks=sb, lb_rows=lb_rows, scale=scale)
    return pl.pallas_call(
        kern,
        grid=(t // tm, n_steps),
        in_specs=[
            pl.BlockSpec((tm, k), lambda i, j: (i, 0)),
            pl.BlockSpec((1, k), lambda i, j: (0, 0)),
            pl.BlockSpec((aux.shape[0], tn), lambda i, j: (0, jnp.clip(j - aux_section * sb, 0, sb - 1))),
        ] + [weight_spec(wi) for wi in range(len(ws))] + side_in,
        out_specs=[out_spec(o) for o in range(len(out_dtypes))] + side_out,
        out_shape=[jax.ShapeDtypeStruct((t, n_blocks[o] * tn), out_dtypes[o])
                   for o in range(len(out_dtypes))] + side_shapes,
        scratch_shapes=[pltpu.VMEM((tm, k), BF16)],
        compiler_params=_params("arbitrary" if side else "parallel", "arbitrary"),
        name=name,
    )(x2, gain.reshape(1, k), aux, *ws, *[w for w, _ in side])


def _cast_kernel(w_ref, o_ref):
    o_ref[...] = w_ref[...].astype(o_ref.dtype)


def _cast_bf16(w, layer, *, name):
    _, r, c = w.shape
    tc = min(c, 2048)
    tr = min(r, max(8, CAST_BLOCK_ELEMS // tc))
    return pl.pallas_call(
        _cast_kernel,
        grid=(r // tr, c // tc),
        in_specs=[pl.BlockSpec((None, tr, tc), lambda i, j: (layer, i, j))],
        out_specs=pl.BlockSpec((tr, tc), lambda i, j: (i, j)),
        out_shape=jax.ShapeDtypeStruct((r, c), BF16),
        compiler_params=_params("parallel", "parallel"),
        name=name,
    )(w)


def _mixer_out_mlp_kernel(x_ref, a_ref, wo_ref, g_ref, up_ref, down_ref, *rest, n_side):
    side_in, o_ref, side_out, h_scr = rest[:n_side], rest[n_side], rest[n_side + 1:-1], rest[-1]

    def mixer_out():
        _run_side_casts(side_in, side_out)
        for rows in _row_chunks(x_ref.shape[0]):
            x1 = x_ref[rows, :] + jnp.dot(a_ref[rows, :], wo_ref[...], preferred_element_type=F32)
            o_ref[rows, :] = x1
            h_scr[rows, :] = _rms_bf16(x1, g_ref[...])

    def mlp_tile():
        _run_side_casts(side_in, side_out)
        for rows in _row_chunks(x_ref.shape[0]):
            u = jnp.dot(h_scr[rows, :], up_ref[...], preferred_element_type=F32)
            a = jnp.square(jnp.maximum(u, 0.0)).astype(BF16)
            o_ref[rows, :] += jnp.dot(a, down_ref[...], preferred_element_type=F32)

    j = pl.program_id(1)
    pl.when(j == 0)(mixer_out)
    pl.when(j > 0)(mlp_tile)


def _mixer_out_mlp(x2, a, w_out, gain, up, down, *, name, side=(), tm=512, tf=1024):
    t, d = x2.shape
    ka = a.shape[1]
    f = up.shape[1]
    tm, tf = min(tm, t), min(tf, f)
    n_inner = 1 + f // tf
    plans = _side_cast_plan(side, (t // tm) * n_inner)
    side_in, side_out, side_shapes = _side_cast_specs(side, plans, lambda i, j: i * n_inner + j)
    return pl.pallas_call(
        functools.partial(_mixer_out_mlp_kernel, n_side=len(side)),
        grid=(t // tm, n_inner),
        in_specs=[
            pl.BlockSpec((tm, d), lambda i, j: (i, 0)),
            pl.BlockSpec((tm, ka), lambda i, j: (i, 0)),
            pl.BlockSpec((ka, d), lambda i, j: (0, 0), pipeline_mode=pl.Buffered(1)),
            pl.BlockSpec((1, d), lambda i, j: (0, 0)),
            pl.BlockSpec((d, tf), lambda i, j: (0, jnp.maximum(j - 1, 0))),
            pl.BlockSpec((tf, d), lambda i, j: (jnp.maximum(j - 1, 0), 0)),
        ] + side_in,
        out_specs=[pl.BlockSpec((tm, d), lambda i, j: (i, 0))] + side_out,
        out_shape=[jax.ShapeDtypeStruct((t, d), F32)] + side_shapes,
        scratch_shapes=[pltpu.VMEM((tm, d), BF16)],
        compiler_params=_params("arbitrary" if side else "parallel", "arbitrary"),
        name=name,
    )(x2, a, w_out, gain.reshape(1, d), up, down, *[w for w, _ in side])


def _ple_kernel(x_ref, g_ref, p_ref, gw_ref, pw_ref, o_ref, h_scr, *, tn):
    j = pl.program_id(1)
    cols = pl.ds(pl.multiple_of(j * tn, tn), tn)

    def run(first):
        for rows in _row_chunks(x_ref.shape[0]):
            if first:
                h = _rms_bf16(x_ref[rows, :], g_ref[...])
                h_scr[rows, :] = h
            else:
                h = h_scr[rows, :]
            gate = jax.nn.sigmoid(jnp.dot(h, gw_ref[...], preferred_element_type=F32))
            emb = jnp.dot(p_ref[rows, :].astype(BF16), pw_ref[...], preferred_element_type=F32)
            o_ref[rows, :] = x_ref[rows, cols] + gate * emb

    pl.when(j == 0)(lambda: run(True))
    pl.when(j > 0)(lambda: run(False))


def _ple(x2, gain, p3, layer, gate_w, proj_w, *, name, tm=1024, tn=2048):
    t, d = x2.shape
    pd = p3.shape[2]
    tm, tn = min(tm, t), min(tn, d)
    resident = dict(pipeline_mode=pl.Buffered(1)) if d == tn else {}
    return pl.pallas_call(
        functools.partial(_ple_kernel, tn=tn),
        grid=(t // tm, d // tn),
        in_specs=[
            pl.BlockSpec((tm, d), lambda i, j: (i, 0)),
            pl.BlockSpec((1, d), lambda i, j: (0, 0)),
            pl.BlockSpec((None, tm, pd), lambda i, j: (layer, i, 0)),
            pl.BlockSpec((d, tn), lambda i, j: (0, j), **resident),
            pl.BlockSpec((pd, tn), lambda i, j: (0, j), **resident),
        ],
        out_specs=pl.BlockSpec((tm, tn), lambda i, j: (i, j)),
        out_shape=jax.ShapeDtypeStruct((t, d), F32),
        scratch_shapes=[pltpu.VMEM((tm, d), BF16)],
        compiler_params=_params("parallel", "arbitrary"),
        name=name,
    )(x2, gain.reshape(1, d), p3, gate_w, proj_w)


def _hgrn_levels(c):
    return [c >> (l + 1) for l in range(int(math.log2(c)))]


def _hgrn_constants(c):
    idx = np.arange(c)
    x = idx[:, None] ^ idx[None, :]
    masks = [((x >= m) & (x < 2 * m) & ((idx[:, None] & m) != 0)).astype(np.float32) for m in _hgrn_levels(c)]
    return np.tril(np.ones((c, c), np.float32)), np.stack(masks, 0)


def _level_log_decay(g_cum, g, m):
    c, w = g_cum.shape
    if m == 1:
        odd = (lax.broadcasted_iota(jnp.int32, (c, w), 0) & 1) != 0
        return jnp.where(odd, g, 0.0)
    if m < 8:
        g3 = g_cum.reshape(c // 8, 8, w)
        sub = lax.broadcasted_iota(jnp.int32, g3.shape, 1)
        if m == 4:
            mid = g3[:, 3:4, :]
        else:
            mid = jnp.where(sub < 4, g3[:, 1:2, :], g3[:, 5:6, :])
        return (-jnp.abs(g3 - mid)).reshape(c, w)
    g3 = g_cum.reshape(c // (2 * m), 2 * m, w)
    mid = g3[:, m - 1:m, :]
    return jnp.concatenate([mid - g3[:, :m, :], g3[:, m:, :] - mid], axis=1).reshape(c, w)


def _pair_rows(lower, upper, m):
    c, w = lower.shape
    if m < 8:
        return jnp.where((lax.broadcasted_iota(jnp.int32, (c, w), 0) & m) != 0, lower, upper)
    lo3 = lower.reshape(c // (2 * m), 2 * m, w)
    up3 = upper.reshape(c // (2 * m), 2 * m, w)
    return jnp.concatenate([up3[:, :m, :], lo3[:, m:, :]], axis=1).reshape(c, w)


def _hgrn_kernel(q_ref, f_ref, v_ref, gate_ref, tri_ref, msk_ref, on_ref, *rest, c, hb, nchunks, n_side):
    side_in, o_ref, side_out, st_scr = rest[:n_side], rest[n_side], rest[n_side + 1:-1], rest[-1]

    @pl.when(pl.program_id(2) == 0)
    def _():
        st_scr[...] = jnp.zeros_like(st_scr)

    levels = _hgrn_levels(c)
    hw = hb * LANES

    def chunk_decay(ci, worst):
        g = f_ref[0, pl.ds(pl.multiple_of(ci * c, c), c), :]
        halves = jnp.minimum(jnp.sum(g[:c // 2], axis=0, keepdims=True), jnp.sum(g[c // 2:], axis=0, keepdims=True))
        return jnp.minimum(worst, halves)

    worst = lax.fori_loop(0, nchunks, chunk_decay, jnp.zeros((1, hw), F32))
    mild = jnp.min(worst) > -HGRN_SAFE_LOG2_RANGE

    def chunk(ci, mild_decay):
        rows = pl.ds(pl.multiple_of(ci * c, c), c)
        g_all = f_ref[0, rows, :]
        g_hi = g_all.astype(BF16)
        g_lo = (g_all - g_hi.astype(F32)).astype(BF16)
        tri = tri_ref[...]
        g_cum_all = (jnp.dot(tri, g_hi, preferred_element_type=F32)
                     + jnp.dot(tri, g_lo, preferred_element_type=F32))
        k_all = 1.0 - jnp.exp2(g_all)
        g_last = g_cum_all[c - 1:c, :]
        if mild_decay:
            g_mid = g_cum_all[c // 2 - 1:c // 2, :]
            q_mid_all = jnp.exp2(g_cum_all - g_mid)
            k_mid_all = jnp.exp2(g_mid - g_cum_all)
            to_mid_all = jnp.exp2(g_mid)
            mid_to_end_all = jnp.exp2(g_last - g_mid)
            causal = (lax.broadcasted_iota(jnp.int32, (c, c), 0) >= lax.broadcasted_iota(jnp.int32, (c, c), 1))
        else:
            q_dec_all = jnp.exp2(g_cum_all)
            k_dec_all = jnp.exp2(g_last - g_cum_all)
            st_dec_all = jnp.exp2(g_last)
            lvl_dec_all = [jnp.exp2(_level_log_decay(g_cum_all, g_all, m)) for m in levels]
        for h in range(hb):
            hl = slice(h * LANES, (h + 1) * LANES)
            qs = q_ref[0, rows, hl].astype(F32)
            vb = v_ref[0, rows, hl]
            k = k_all[:, hl]
            st = st_scr[h]
            if mild_decay:
                qm = (qs * q_mid_all[:, hl]).astype(BF16)
                km = (k * k_mid_all[:, hl]).astype(BF16)
                a = jnp.where(causal, lax.dot_general(qm, km, _NT, preferred_element_type=F32), 0.0)
                st_mid = st * to_mid_all[:, hl]
                o = (jnp.dot(a.astype(BF16), vb, preferred_element_type=F32)
                     + lax.dot_general(qm, st_mid.astype(BF16), _NT, preferred_element_type=F32))
                st_scr[h] = ((st_mid + lax.dot_general(vb, km, _TN, preferred_element_type=F32))
                             * mid_to_end_all[:, hl])
            else:
                a = jnp.zeros((c, c), F32)
                for l, m in enumerate(levels):
                    x = (_pair_rows(qs, k, m) * lvl_dec_all[l][:, hl]).astype(BF16)
                    a = a + msk_ref[l] * lax.dot_general(x, x, _NT, preferred_element_type=F32)
                o = jnp.dot(a.astype(BF16), vb, preferred_element_type=F32)
                o = o + jnp.sum(qs * k, axis=-1, keepdims=True) * vb.astype(F32)
                qg = (qs * q_dec_all[:, hl]).astype(BF16)
                o = o + lax.dot_general(qg, st.astype(BF16), _NT, preferred_element_type=F32)
                kt = (k * k_dec_all[:, hl]).astype(BF16)
                st_scr[h] = st * st_dec_all[:, hl] + lax.dot_general(vb, kt, _TN, preferred_element_type=F32)
            ms = jnp.mean(o * o, axis=-1, keepdims=True)
            y = o * lax.rsqrt(ms + EPS) * on_ref[...] * gate_ref[0, rows, hl].astype(F32)
            o_ref[0, rows, hl] = y.astype(o_ref.dtype)

    unroll = HGRN_UNROLL if nchunks % HGRN_UNROLL == 0 else 1

    def all_chunks(mild_decay):
        def trip(u, carry):
            _run_side_casts(side_in, side_out)
            for r in range(unroll):
                chunk(u * unroll + r, mild_decay)
            return carry

        lax.fori_loop(0, nchunks // unroll, trip, 0)

    pl.when(mild)(lambda: all_chunks(True))
    pl.when(jnp.logical_not(mild))(lambda: all_chunks(False))


def _hgrn(qig, log_f, onorm, *, name, side=(), rows=1024, hb=4):
    b, s, width = log_f.shape
    c = HGRN_CHUNK
    rows = min(rows, s)
    hw = hb * LANES
    nsb = width // hw
    nr = s // rows
    tri, masks = _hgrn_constants(c)
    nl = masks.shape[0]
    plans = _side_cast_plan(side, b * nsb * nr)
    side_in, side_out, side_shapes = _side_cast_specs(side, plans, lambda bi, hg, r: (bi * nsb + hg) * nr + r)

    def sec(k):
        return pl.BlockSpec((1, rows, hw), lambda bi, hg, r: (bi, r, k * nsb + hg))

    kern = functools.partial(_hgrn_kernel, c=c, hb=hb, nchunks=rows // c, n_side=len(side))
    outer = "arbitrary" if side else "parallel"
    return pl.pallas_call(
        kern,
        grid=(b, nsb, nr),
        in_specs=[
            sec(0), sec(0), sec(1), sec(2),
            pl.BlockSpec((c, c), lambda bi, hg, r: (0, 0)),
            pl.BlockSpec((nl, c, c), lambda bi, hg, r: (0, 0, 0)),
            pl.BlockSpec((1, LANES), lambda bi, hg, r: (0, 0)),
        ] + side_in,
        out_specs=[pl.BlockSpec((1, rows, hw), lambda bi, hg, r: (bi, r, hg))] + side_out,
        out_shape=[jax.ShapeDtypeStruct((b, s, width), BF16)] + side_shapes,
        scratch_shapes=[pltpu.VMEM((hb, LANES, LANES), F32)],
        compiler_params=_params(outer, outer, "arbitrary"),
        name=name,
    )(qig, log_f, qig, qig, jnp.asarray(tri, BF16), jnp.asarray(masks, F32), onorm.reshape(1, LANES),
      *[w for w, _ in side])


def _attn_kernel(q_ref, k_ref, v_ref, qg_ref, kg_ref, sub_ref, lq1_ref, lk1_ref, lq2_ref, lk2_ref, o_ref,
                 m_scr, l_scr, acc_scr, shift_scr, *, tq, lam_init):
    hw = acc_scr.shape[-1]
    nq = q_ref.shape[1] // tq
    maps = [slice(c * LANES, (c + 1) * LANES) for c in range(2)]

    bounds = []
    for c, sl in enumerate(maps):
        q_max = jnp.max(jnp.abs(qg_ref[:, sl]), axis=-1, keepdims=True)
        k_max = jnp.max(jnp.abs(kg_ref[:, sl]), axis=-1, keepdims=True)
        bounds.append(BOUND_SLACK * LANES * q_max * k_max)
        shift_scr[c] = jnp.broadcast_to(bounds[c], (1, LANES))
    bounded = jnp.max(jnp.maximum(bounds[0], bounds[1])) < SAFE_LOG2_SHIFT

    def causal(shape, row0):
        row = lax.broadcasted_iota(jnp.int32, shape, 0) + row0
        return lax.broadcasted_iota(jnp.int32, shape, 1) <= row

    def block(qi, slot, j, masked, running_max, first=False):
        for qrows in _row_chunks(tq):
            nk = qrows.stop if masked else tq
            keys = pl.ds(_aligned(j * tq, tq), nk)
            qsel = pl.ds(_aligned(qi * tq + qrows.start, qrows.stop - qrows.start), qrows.stop - qrows.start)
            vb = v_ref[0, keys, :]
            for c, sl in enumerate(maps):
                s = lax.dot_general(q_ref[0, qsel, sl], k_ref[0, keys, sl], _NT,
                                    preferred_element_type=F32)
                if running_max:
                    if masked:
                        s = jnp.where(causal(s.shape, qrows.start), s, -jnp.inf)
                    m_prev = m_scr[c, qrows, :]
                    m_new = jnp.maximum(m_prev, jnp.max(s, axis=-1, keepdims=True))
                    alpha = jnp.exp2(m_prev - m_new)
                    p = jnp.exp2(s - jnp.tile(m_new, (1, nk // LANES)))
                    l_scr[slot, c, qrows, :] = (alpha * l_scr[slot, c, qrows, :]
                                                + jnp.sum(p, axis=-1, keepdims=True))
                    acc_scr[slot, c, qrows, :] = (jnp.tile(alpha, (1, hw // LANES)) * acc_scr[slot, c, qrows, :]
                                                  + jnp.dot(p.astype(BF16), vb, preferred_element_type=F32))
                    m_scr[c, qrows, :] = m_new
                else:
                    p = jnp.exp2(s - jnp.tile(shift_scr[c], (1, nk // LANES)))
                    if masked:
                        p = jnp.where(causal(p.shape, qrows.start), p, 0.0)
                    lane_sums = p[:, 0:LANES]
                    for g in range(1, nk // LANES):
                        lane_sums = lane_sums + p[:, g * LANES:(g + 1) * LANES]
                    pv = jnp.dot(p.astype(BF16), vb, preferred_element_type=F32)
                    if first:
                        l_scr[slot, c, qrows, :] = lane_sums
                        acc_scr[slot, c, qrows, :] = pv
                    else:
                        l_scr[slot, c, qrows, :] += lane_sums
                        acc_scr[slot, c, qrows, :] += pv

    lam = (jnp.exp(jnp.sum(lq1_ref[...] * lk1_ref[...], axis=-1, keepdims=True))
           - jnp.exp(jnp.sum(lq2_ref[...] * lk2_ref[...], axis=-1, keepdims=True))
           + lam_init)

    def finalize(qi, slot, lanes_summed):
        ls = [l_scr[slot, c] for c in range(2)]
        if not lanes_summed:
            ls = [jnp.sum(l, axis=-1, keepdims=True) for l in ls]
            invs = [jnp.broadcast_to(1.0 / l, (tq, hw)) for l in ls]
        else:
            invs = [jnp.tile(1.0 / l, (1, hw // LANES)) for l in ls]
        o = acc_scr[slot, 0] * invs[0] - lam * (acc_scr[slot, 1] * invs[1])
        ms = jnp.mean(o * o, axis=-1, keepdims=True)
        o_ref[0, pl.ds(_aligned(qi * tq, tq), tq), :] = (o * lax.rsqrt(ms + EPS)
                                           * (sub_ref[...] * (1.0 - lam_init))).astype(o_ref.dtype)

    @pl.when(bounded)
    def _():
        for qi in range(nq):
            slot = qi % 2
            block(qi, slot, qi, True, False, first=True)
            for j in range(qi):
                block(qi, slot, j, False, False)
            finalize(qi, slot, lanes_summed=False)

    @pl.when(jnp.logical_not(bounded))
    def _():
        def query_block(qi, carry):
            m_scr[...] = jnp.full_like(m_scr, -1e30)
            l_scr[0] = jnp.zeros_like(l_scr[0])
            acc_scr[0] = jnp.zeros_like(acc_scr[0])

            def body(j, c2):
                block(qi, 0, j, False, True)
                return c2

            lax.fori_loop(0, qi, body, 0)
            block(qi, 0, qi, True, True)
            finalize(qi, 0, lanes_summed=True)
            return carry

        lax.fori_loop(0, nq, query_block, 0)


def _diff_attention(q, kv, q_gain, k_gain, subln, lq1, lk1, lq2, lk2, *, lam_init, name, tq=1024):
    b, s, width = q.shape
    hw = 2 * LANES
    nh = width // hw
    tq = min(tq, s)
    vec = pl.BlockSpec((1, LANES), lambda bi, h: (0, 0))
    head_vec = pl.BlockSpec((1, hw), lambda bi, h: (0, 0))
    return pl.pallas_call(
        functools.partial(_attn_kernel, tq=tq, lam_init=lam_init),
        grid=(b, nh),
        in_specs=[
            pl.BlockSpec((1, s, hw), lambda bi, h: (bi, 0, h)),
            pl.BlockSpec((1, s, hw), lambda bi, h: (bi, 0, h)),
            pl.BlockSpec((1, s, hw), lambda bi, h: (bi, 0, nh + h)),
            head_vec, head_vec, head_vec,
            vec, vec, vec, vec,
        ],
        out_specs=pl.BlockSpec((1, s, hw), lambda bi, h: (bi, 0, h)),
        out_shape=jax.ShapeDtypeStruct((b, s, width), BF16),
        scratch_shapes=[
            pltpu.VMEM((2, tq, LANES), F32),
            pltpu.VMEM((2, 2, tq, LANES), F32),
            pltpu.VMEM((2, 2, tq, hw), F32),
            pltpu.VMEM((2, 1, LANES), F32),
        ],
        compiler_params=_params("parallel", "parallel"),
        name=name,
    )(q, kv, kv, q_gain.reshape(1, hw), k_gain.reshape(1, hw), subln.reshape(1, hw), lq1.reshape(1, LANES),
      lk1.reshape(1, LANES),
      lq2.reshape(1, LANES), lk2.reshape(1, LANES))


def kernel(x, p, ln_mix, ln_mlp, ln_ple, a_w_in, a_lb, a_onorm, a_w_out, kv_norm, w_k, w_v, k_norm, b_w_q, q_norm, lam_q1, lam_k1, lam_q2, lam_k2, b_subln, b_w_out, mlp_up, mlp_down, ple_proj, ple_gate):
    b, s, d = x.shape
    depth = ln_mix.shape[0]
    n_a = a_w_in.shape[0]
    t = b * s
    x2 = x.reshape(t, d)
    kv = None

    def in_proj_weight(i):
        return (a_w_in, i) if i < n_a else (b_w_q, i - n_a)

    def layer_weights(i):
        w_out = (a_w_out, i) if i < n_a else (b_w_out, i - n_a)
        return [(("w_out", i), w_out), (("up", i), (mlp_up, i)), (("down", i), (mlp_down, i)),
                (("gate", i), (ple_gate, i)), (("proj", i), (ple_proj, i))]

    bf16w = {("in", 0): _cast_bf16(*in_proj_weight(0), name="cast_in_proj_0")}

    def pending(i, include_own):
        wanted = layer_weights(i) if include_own else []
        if i == n_a - 1:
            wanted += [(("k", 0), (w_k[None], 0)), (("v", 0), (w_v[None], 0))]
        if i + 1 < depth:
            wanted += [(("in", i + 1), in_proj_weight(i + 1))] + layer_weights(i + 1)
        return [(key, w) for key, w in wanted if key not in bf16w]

    for i in range(depth):
        if i < n_a:
            j = i
            width = a_w_out.shape[1]
            qig, log_f = _norm_linear(
                x2, ln_mix[i], (bf16w["in", i],), a_lb,
                sections=(("silu", 0, 0, 0), ("plain", 0, 0, 2), ("silu", 0, 0, 3), ("log_forget", 1, 0, 1)),
                aux_section=3, sec_width=width, out_dtypes=(BF16, F32), lb_rows=i + 1, name=f"hgrn_in_{i}")
            carried = pending(i, include_own=True)
            o, *cast = _hgrn(qig.reshape(b, s, 3 * width), log_f.reshape(b, s, width), a_onorm[j],
                             side=[w for _, w in carried], name=f"hgrn_{i}")
            bf16w.update({key: w for (key, _), w in zip(carried, cast)})
        else:
            j = i - n_a
            width = b_w_q.shape[2]
            q_scale = math.log2(math.e) / math.sqrt(LANES)
            head_gain = jnp.tile(q_norm[j].reshape(1, -1), (1, width // (2 * LANES)))
            (q,) = _norm_linear(x2, ln_mix[i], (bf16w["in", i],), head_gain,
                                sections=(("head_rms", 0, 0, 0),), aux_section=0, sec_width=width,
                                out_dtypes=(BF16,), scale=q_scale, tn=width, name=f"attn_q_{i}")
            lam_init = 0.8 - 0.6 * math.exp(-0.3 * i)
            o = _diff_attention(q.reshape(b, s, width), kv, q_norm[j] * q_scale, k_norm, b_subln[j], lam_q1[j],
                                lam_k1[j], lam_q2[j], lam_k2[j], lam_init=lam_init, name=f"attn_{i}")
            for key, (w, layer) in layer_weights(i):
                if key not in bf16w:
                    bf16w[key] = _cast_bf16(w, layer, name=f"cast_{key[0]}_{i}")

        carried = pending(i, include_own=False)
        x2, *cast = _mixer_out_mlp(x2, o.reshape(t, width), bf16w["w_out", i], ln_mlp[i], bf16w["up", i],
                                   bf16w["down", i], side=[w for _, w in carried], name=f"mlp_{i}")
        bf16w.update({key: w for (key, _), w in zip(carried, cast)})
        x2 = _ple(x2, ln_ple[i], p.reshape(depth, t, -1), i, bf16w["gate", i], bf16w["proj", i], name=f"ple_{i}")
        if i == n_a - 1:
            width = w_k.shape[1]
            head_gain = jnp.tile(k_norm.reshape(1, -1), (1, width // (2 * LANES)))
            (kv,) = _norm_linear(x2, kv_norm, (bf16w["k", 0], bf16w["v", 0]), head_gain,
                                 sections=(("head_rms", 0, 0, 0), ("plain", 0, 1, 0)), aux_section=0,
                                 sec_width=width, out_dtypes=(BF16,), name="shared_kv")
            kv = kv.reshape(b, s, 2 * width)
    return x2.reshape(b, s, d)
```
